```python
import math
import jax, jax.numpy as jnp
from jax import lax
import numpy as np

D_MODEL = 2048
BATCH = 2
SEQ = 8192
DEPTH = 2

N_MIXERS = 2
ROPE_THETA = 500000.0
ROPE_FRACTION = 4
DEEPNORM_ALPHA = (2 * DEPTH) ** 0.25
DEEPNORM_BETA = (8 * DEPTH) ** -0.25
LN_EPS = 1e-5

DA_HEADS = 8
DA_HEAD_DIM = D_MODEL // DA_HEADS // 2
DA_Q_BLOCK = 128

MB_HEADS = 16
MB_HEAD_DIM = D_MODEL // MB_HEADS
MB_BLOCK = 256
MB_TOPK = 3
MB_Q_CHUNK = 16

N_EXPERTS = 64
TOP_K = 8
N_GROUPS = 8
TOPK_GROUPS = 4
D_EXPERT = 512
D_SHARED = 512
ROUTED_SCALE = 2.5

kernel_name = "hybrid_diffattn_moba_moe_deepnorm"


def layer_norm(x, g, b):
    xf = x.astype(jnp.float32)
    mu = xf.mean(-1, keepdims=True)
    var = jnp.square(xf - mu).mean(-1, keepdims=True)
    return ((xf - mu) * lax.rsqrt(var + LN_EPS) * g.astype(jnp.float32) + b.astype(jnp.float32)).astype(x.dtype)


def rms_norm(x, g):
    xf = x.astype(jnp.float32)
    return (xf * lax.rsqrt(jnp.mean(xf * xf, -1, keepdims=True) + LN_EPS) * g.astype(jnp.float32)).astype(x.dtype)


def rope_tables(positions, head_dim):
    rot = head_dim // ROPE_FRACTION
    inv_freq = ROPE_THETA ** (-jnp.arange(0, rot, 2, dtype=jnp.float32) / rot)
    ang = positions.astype(jnp.float32)[..., None] * inv_freq
    return jnp.cos(ang)[:, :, None, :], jnp.sin(ang)[:, :, None, :]


def apply_partial_rope(x, cos, sin):
    rot = 2 * cos.shape[-1]
    xr = x[..., :rot].astype(jnp.float32)
    x1, x2 = xr[..., : rot // 2], xr[..., rot // 2:]
    xr = jnp.concatenate([x1 * cos - x2 * sin, x2 * cos + x1 * sin], axis=-1).astype(x.dtype)
    return jnp.concatenate([xr, x[..., rot:]], axis=-1)


def diff_attention(x, positions, w_in, w_out, lq1, lk1, lq2, lk2, subln_g, lambda_init):
    B, S, _ = x.shape
    H, hd = DA_HEADS, DA_HEAD_DIM
    qkv = x @ w_in
    q, k, v = jnp.split(qkv, [2 * H * hd, 4 * H * hd], axis=-1)
    cos, sin = rope_tables(positions, hd)
    q = apply_partial_rope(q.reshape(B, S, 2 * H, hd), cos, sin) * (hd ** -0.5)
    k = apply_partial_rope(k.reshape(B, S, 2 * H, hd), cos, sin)
    v = v.reshape(B, S, H, 2 * hd)
    f32 = jnp.float32
    lam = (jnp.exp(jnp.sum(lq1.astype(f32) * lk1.astype(f32)))
           - jnp.exp(jnp.sum(lq2.astype(f32) * lk2.astype(f32))) + lambda_init)
    k_pos = jnp.arange(S)

    def q_block(i):
        start = i * DA_Q_BLOCK
        qb = lax.dynamic_slice_in_dim(q, start, DA_Q_BLOCK, axis=1)
        s = jnp.einsum('bqmd,bkmd->bmqk', qb, k).astype(f32)
        q_pos = start + jnp.arange(DA_Q_BLOCK)
        s = jnp.where(k_pos[None, :] <= q_pos[:, None], s, -jnp.inf)
        p = jax.nn.softmax(s, axis=-1).reshape(B, H, 2, DA_Q_BLOCK, S)
        a = (p[:, :, 0] - lam * p[:, :, 1]).astype(v.dtype)
        return jnp.einsum('bhqk,bkhe->bqhe', a, v)

    o = lax.map(q_block, jnp.arange(S // DA_Q_BLOCK))
    o = o.transpose(1, 0, 2, 3, 4).reshape(B, S, H, 2 * hd)
    o = rms_norm(o, subln_g) * (1.0 - lambda_init)
    return o.reshape(B, S, H * 2 * hd) @ w_out


def moba_attention(x, positions, w_in, w_out):
    B, S, _ = x.shape
    H, hd = MB_HEADS, MB_HEAD_DIM
    q, k, v = jnp.split(x @ w_in, 3, axis=-1)
    cos, sin = rope_tables(positions, hd)
    q = apply_partial_rope(q.reshape(B, S, H, hd), cos, sin) * (hd ** -0.5)
    k = apply_partial_rope(k.reshape(B, S, H, hd), cos, sin)
    v = v.reshape(B, S, H, hd)
    nb = -(-S // MB_BLOCK)
    pad = nb * MB_BLOCK - S
    q = q.transpose(0, 2, 1, 3)
    kb = jnp.pad(k, ((0, 0), (0, pad), (0, 0), (0, 0))).transpose(0, 2, 1, 3).reshape(B, H, nb, MB_BLOCK, hd)
    vb = jnp.pad(v, ((0, 0), (0, pad), (0, 0), (0, 0))).transpose(0, 2, 1, 3).reshape(B, H, nb, MB_BLOCK, hd)
    k_mean = kb.astype(jnp.float32).mean(axis=3).astype(kb.dtype)
    k_sel = min(MB_TOPK, nb)
    blk_ids = jnp.arange(nb)
    l_ids = jnp.arange(MB_BLOCK)
    gather_blocks = jax.vmap(jax.vmap(lambda t, idx: t[idx]))

    def q_chunk(c):
        start = c * MB_Q_CHUNK
        j = start // MB_BLOCK
        qc = lax.dynamic_slice_in_dim(q, start, MB_Q_CHUNK, axis=2)
        q_pos = start + jnp.arange(MB_Q_CHUNK)
        gate = jnp.einsum('bhqd,bhnd->bhqn', qc, k_mean).astype(jnp.float32)
        gate = jnp.where(blk_ids < j, gate, -jnp.inf)
        _, idx = lax.top_k(gate, k_sel)
        valid = idx < j
        kg = gather_blocks(kb, idx)
        vg = gather_blocks(vb, idx)
        s_sel = jnp.einsum('bhqd,bhqkld->bhqkl', qc, kg).astype(jnp.float32)
        s_sel = jnp.where(valid[..., None], s_sel, -jnp.inf).reshape(B, H, MB_Q_CHUNK, k_sel * MB_BLOCK)
        ko = lax.dynamic_index_in_dim(kb, j, axis=2, keepdims=False)
        vo = lax.dynamic_index_in_dim(vb, j, axis=2, keepdims=False)
        s_own = jnp.einsum('bhqd,bhld->bhql', qc, ko).astype(jnp.float32)
        s_own = jnp.where((j * MB_BLOCK + l_ids)[None, :] <= q_pos[:, None], s_own, -jnp.inf)
        p = jax.nn.softmax(jnp.concatenate([s_sel, s_own], axis=-1), axis=-1).astype(v.dtype)
        p_sel = p[..., : k_sel * MB_BLOCK].reshape(B, H, MB_Q_CHUNK, k_sel, MB_BLOCK)
        p_own = p[..., k_sel * MB_BLOCK:]
        return (jnp.einsum('bhqkl,bhqkld->bhqd', p_sel, vg)
                + jnp.einsum('bhql,bhld->bhqd', p_own, vo))

    o = lax.map(q_chunk, jnp.arange(S // MB_Q_CHUNK))
    o = o.transpose(1, 0, 3, 2, 4).reshape(B, S, H * hd)
    return o @ w_out


def swiglu(t, wg, wu, wd):
    return (jax.nn.silu(t @ wg) * (t @ wu)) @ wd


def moe(x, w_router, router_bias, w_gate, w_up, w_down, ws_gate, ws_up, ws_down):
    B, S, D = x.shape
    t = x.reshape(B * S, D)
    T = t.shape[0]
    f32 = jnp.float32
    scores = jax.nn.sigmoid((t @ w_router).astype(f32))
    biased = scores + router_bias.astype(f32)
    per_group = N_EXPERTS // N_GROUPS
    group_score = lax.top_k(biased.reshape(T, N_GROUPS, per_group), 2)[0].sum(-1)
    _, gidx = lax.top_k(group_score, TOPK_GROUPS)
    gmask = jax.nn.one_hot(gidx, N_GROUPS, dtype=f32).sum(1) > 0
    masked = jnp.where(jnp.repeat(gmask, per_group, axis=1), biased, -jnp.inf)
    _, eidx = lax.top_k(masked, TOP_K)
    w = jnp.take_along_axis(scores, eidx, axis=1)
    w = w / w.sum(-1, keepdims=True) * ROUTED_SCALE
    gates = jnp.einsum('tk,tke->te', w, jax.nn.one_hot(eidx, N_EXPERTS, dtype=f32))

    def add_expert(acc, p):
        wg, wu, wd, g = p
        return acc + swiglu(t, wg, wu, wd).astype(f32) * g[:, None], None

    routed, _ = lax.scan(add_expert, jnp.zeros((T, D), f32), (w_gate, w_up, w_down, gates.T))
    out = routed.astype(x.dtype) + swiglu(t, ws_gate, ws_up, ws_down)
    return out.reshape(B, S, D)


def setup_inputs(seed: int = 0) -> dict:
    key = jax.random.key(seed)
    ks = jax.random.split(key, 24)
    D = D_MODEL
    n_a = (DEPTH + 1) // 2
    n_b = DEPTH // 2
    f32 = jnp.float32

    def nrm(k, shape, fan_in, scale=1.0):
        return jax.random.normal(k, shape, f32) * (scale * fan_in ** -0.5)

    def small(k, shape, s):
        return jax.random.normal(k, shape, f32) * s

    da_width = 4 * DA_HEADS * DA_HEAD_DIM + DA_HEADS * 2 * DA_HEAD_DIM
    mb_width = MB_HEADS * MB_HEAD_DIM
    return {
        "x": jax.random.normal(ks[0], (BATCH, SEQ, D), f32),
        "positions": (jax.random.randint(ks[1], (BATCH, 1), 0, 1024, dtype=jnp.int32)
                      + jnp.arange(SEQ, dtype=jnp.int32)[None, :]),
        "ln1_g": 1.0 + small(ks[2], (DEPTH, D), 0.02),
        "ln1_b": small(ks[3], (DEPTH, D), 0.02),
        "ln2_g": 1.0 + small(ks[4], (DEPTH, D), 0.02),
        "ln2_b": small(ks[5], (DEPTH, D), 0.02),
        "da_w_in": nrm(ks[6], (n_a, D, da_width), D),
        "da_w_out": nrm(ks[7], (n_a, DA_HEADS * 2 * DA_HEAD_DIM, D), DA_HEADS * 2 * DA_HEAD_DIM, DEEPNORM_BETA),
        "da_lq1": small(ks[8], (n_a, DA_HEAD_DIM), 0.1),
        "da_lk1": small(ks[9], (n_a, DA_HEAD_DIM), 0.1),
        "da_lq2": small(ks[10], (n_a, DA_HEAD_DIM), 0.1),
        "da_lk2": small(ks[11], (n_a, DA_HEAD_DIM), 0.1),
        "da_subln_g": 1.0 + small(ks[12], (n_a, 2 * DA_HEAD_DIM), 0.02),
        "mb_w_in": nrm(ks[13], (n_b, D, 3 * mb_width), D),
        "mb_w_out": nrm(ks[14], (n_b, mb_width, D), mb_width, DEEPNORM_BETA),
        "w_router": nrm(ks[15], (DEPTH, D, N_EXPERTS), D),
        "router_bias": small(ks[16], (DEPTH, N_EXPERTS), 0.01),
        "w_gate": nrm(ks[17], (DEPTH, N_EXPERTS, D, D_EXPERT), D),
        "w_up": nrm(ks[18], (DEPTH, N_EXPERTS, D, D_EXPERT), D),
        "w_down": nrm(ks[19], (DEPTH, N_EXPERTS, D_EXPERT, D), D_EXPERT, DEEPNORM_BETA),
        "ws_gate": nrm(ks[20], (DEPTH, D, D_SHARED), D),
        "ws_up": nrm(ks[21], (DEPTH, D, D_SHARED), D),
        "ws_down": nrm(ks[22], (DEPTH, D_SHARED, D), D_SHARED, DEEPNORM_BETA),
    }


def reference(x, positions, ln1_g, ln1_b, ln2_g, ln2_b, da_w_in, da_w_out, da_lq1, da_lk1, da_lq2,
              da_lk2, da_subln_g, mb_w_in, mb_w_out, w_router, router_bias, w_gate, w_up, w_down,
              ws_gate, ws_up, ws_down):
    for i in range(DEPTH):
        m = i // N_MIXERS
        if i % N_MIXERS == 0:
            lambda_init = 0.8 - 0.6 * math.exp(-0.3 * i)
            h = diff_attention(x, positions, da_w_in[m], da_w_out[m], da_lq1[m], da_lk1[m],
                               da_lq2[m], da_lk2[m], da_subln_g[m], lambda_init)
        else:
            h = moba_attention(x, positions, mb_w_in[m], mb_w_out[m])
        x = layer_norm(DEEPNORM_ALPHA * x + h, ln1_g[i], ln1_b[i])
        f = moe(x, w_router[i], router_bias[i], w_gate[i], w_up[i], w_down[i],
                ws_gate[i], ws_up[i], ws_down[i])
        x = layer_norm(DEEPNORM_ALPHA * x + f, ln2_g[i], ln2_b[i])
    return x
```

```python
import functools
import math

import jax
import jax.numpy as jnp
from jax import lax
from jax.experimental import pallas as pl
from jax.experimental.pallas import tpu as pltpu

F32 = jnp.float32
BF16 = jnp.bfloat16

HEAD_DIM = 128
ROPE_THETA = 500000.0
ROPE_DIMS = HEAD_DIM // 4
ROPE_HALF = ROPE_DIMS // 2
LN_EPS = 1e-5
MB_BLOCK = 256
MB_TOPK = 3
TOP_K = 8
N_GROUPS = 8
TOPK_GROUPS = 4
ROUTED_SCALE = 2.5
MASK_VALUE = -1e30

LANES = 128
SUBLANES = 8
VMEM_LIMIT_BYTES = 56 * 1024 * 1024

_NT = (((1,), (1,)), ((), ()))


def _params(*sem):
    return pltpu.CompilerParams(dimension_semantics=sem, vmem_limit_bytes=VMEM_LIMIT_BYTES)


def _sigmoid(x):
    return 1.0 / (1.0 + jnp.exp(-x))


def _layer_norm(y, g, b):
    mu = jnp.mean(y, axis=-1, keepdims=True)
    d = y - mu
    var = jnp.mean(d * d, axis=-1, keepdims=True)
    return d * lax.rsqrt(var + LN_EPS) * g + b


def _qkv_kernel(x_ref, w_ref, cos_ref, sin_ref, o_ref, xb_ref, *, rope_blocks, q_blocks, q_scale):
    j = pl.program_id(1)

    @pl.when(j == 0)
    def _():
        xb_ref[...] = x_ref[...].astype(BF16)

    acc = jnp.dot(xb_ref[...], w_ref[...], preferred_element_type=F32)
    tn = acc.shape[1]

    @pl.when(j < rope_blocks)
    def _():
        cosf = cos_ref[...]
        sinf = sin_ref[...]
        lane = lax.broadcasted_iota(jnp.int32, (1, LANES), 1)
        scale = jnp.where(j < q_blocks, q_scale, 1.0).astype(F32)
        for c in range(tn // LANES):
            xc = acc[:, c * LANES:(c + 1) * LANES]
            partner = jnp.where(lane < ROPE_HALF,
                                pltpu.roll(xc, LANES - ROPE_HALF, 1),
                                pltpu.roll(xc, ROPE_HALF, 1))
            r = (xc * cosf + partner * sinf) * scale
            o_ref[:, c * LANES:(c + 1) * LANES] = r.astype(o_ref.dtype)

    @pl.when(j >= rope_blocks)
    def _():
        o_ref[...] = acc.astype(o_ref.dtype)


def _qkv_proj(x2d, w_bf16, cosf, sinf, *, rope_cols, q_cols, tm, tn):
    T, D = x2d.shape
    N = w_bf16.shape[1]
    kern = functools.partial(_qkv_kernel, rope_blocks=rope_cols // tn, q_blocks=q_cols // tn,
                             q_scale=HEAD_DIM ** -0.5)
    return pl.pallas_call(
        kern,
        grid=(T // tm, N // tn),
        in_specs=[
            pl.BlockSpec((tm, D), lambda i, j: (i, 0)),
            pl.BlockSpec((D, tn), lambda i, j: (0, j)),
            pl.BlockSpec((tm, LANES), lambda i, j: (i, 0)),
            pl.BlockSpec((tm, LANES), lambda i, j: (i, 0)),
        ],
        out_specs=pl.BlockSpec((tm, tn), lambda i, j: (i, j)),
        out_shape=jax.ShapeDtypeStruct((T, N), BF16),
        scratch_shapes=[pltpu.VMEM((tm, D), BF16)],
        compiler_params=_params("parallel", "arbitrary"),
        name="qkv_proj",
    )(x2d, w_bf16, cosf, sinf)


def _rope_tables(positions):
    inv_freq = ROPE_THETA ** (-jnp.arange(0, ROPE_DIMS, 2, dtype=F32) / ROPE_DIMS)
    ang = positions.astype(F32).reshape(-1)[:, None] * inv_freq
    cos, sin = jnp.cos(ang), jnp.sin(ang)
    T = ang.shape[0]
    rest = LANES - ROPE_DIMS
    cosf = jnp.concatenate([cos, cos, jnp.ones((T, rest), F32)], axis=1)
    sinf = jnp.concatenate([-sin, sin, jnp.zeros((T, rest), F32)], axis=1)
    return cosf, sinf


def _softmax_first(sT, vT):
    m = jnp.max(sT, axis=0, keepdims=True)
    p = jnp.exp(sT - m)
    l = jnp.sum(p, axis=0, keepdims=True)
    acc = jnp.dot(vT, p.astype(BF16), preferred_element_type=F32)
    return m, l, acc


def _softmax_update(sT, vT, m_ref, l_ref, acc_ref):
    m_prev = m_ref[...]
    m_new = jnp.maximum(m_prev, jnp.max(sT, axis=0, keepdims=True))
    alpha = jnp.exp(m_prev - m_new)
    p = jnp.exp(sT - m_new)
    l_ref[...] = alpha * l_ref[...] + jnp.sum(p, axis=0, keepdims=True)
    acc_ref[...] = alpha * acc_ref[...] + jnp.dot(vT, p.astype(BF16), preferred_element_type=F32)
    m_ref[...] = m_new


def _causal_mask(sT):
    tk, tq = sT.shape
    key = lax.broadcasted_iota(jnp.int32, (tk, tq), 0)
    qry = lax.broadcasted_iota(jnp.int32, (tk, tq), 1)
    return jnp.where(key <= qry, sT, MASK_VALUE)


def _da_kernel(lam_ref, q_ref, k_ref, vT_ref, g_ref, o_ref, m_ref, l_ref, acc_ref, *, tq, out_scale):
    qi = pl.program_id(2)
    hd = HEAD_DIM

    def scores(ki, c):
        koff = pl.multiple_of(ki * tq, tq)
        kblk = k_ref[pl.ds(koff, tq), c * hd:(c + 1) * hd]
        return lax.dot_general(kblk, q_ref[:, c * hd:(c + 1) * hd], _NT, preferred_element_type=F32)

    vT_diag = vT_ref[qi]
    for c in range(2):
        m, l, acc = _softmax_first(_causal_mask(scores(qi, c)), vT_diag)
        m_ref[c] = m
        l_ref[c] = l
        acc_ref[c] = acc

    def body(ki, carry):
        vT = vT_ref[ki]
        for c in range(2):
            _softmax_update(scores(ki, c), vT, m_ref.at[c], l_ref.at[c], acc_ref.at[c])
        return carry

    lax.fori_loop(0, qi, body, 0)

    lam = lam_ref[0]
    oT = acc_ref[0] / l_ref[0] - lam * (acc_ref[1] / l_ref[1])
    o = oT.T
    o = o * lax.rsqrt(jnp.mean(o * o, axis=-1, keepdims=True) + LN_EPS) * g_ref[...]
    o_ref[...] = (o * out_scale).astype(o_ref.dtype)


def _diff_attention(qkv, vT, lam, subln_g, *, B, S, H, tq, lambda_init):
    T = B * S
    nq = S // tq
    dv = 2 * HEAD_DIM
    kern = functools.partial(_da_kernel, tq=tq, out_scale=1.0 - lambda_init)
    return pl.pallas_call(
        kern,
        grid=(B, H, nq),
        in_specs=[
            pl.BlockSpec(memory_space=pltpu.SMEM),
            pl.BlockSpec((tq, dv), lambda b, h, i: (b * nq + i, h)),
            pl.BlockSpec((S, dv), lambda b, h, i: (b, H + h)),
            pl.BlockSpec((None, None, nq, dv, tq), lambda b, h, i: (b, h, 0, 0, 0)),
            pl.BlockSpec((1, dv), lambda b, h, i: (0, 0)),
        ],
        out_specs=pl.BlockSpec((tq, dv), lambda b, h, i: (b * nq + i, h)),
        out_shape=jax.ShapeDtypeStruct((T, H * dv), BF16),
        scratch_shapes=[
            pltpu.VMEM((2, 1, tq), F32),
            pltpu.VMEM((2, 1, tq), F32),
            pltpu.VMEM((2, dv, tq), F32),
        ],
        compiler_params=_params("parallel", "parallel", "arbitrary"),
        name="diff_attention",
    )(lam, qkv, qkv, vT, subln_g)


def _mb_kernel(q_ref, k_ref, vT_ref, o_ref, kmean_ref, bias_ref, m_ref, l_ref, acc_ref, *, nb):
    j = pl.program_id(2)
    blk = MB_BLOCK

    @pl.when(j == 0)
    def _():
        for n in range(nb):
            kb = k_ref[n * blk:(n + 1) * blk, :].astype(F32)
            kmean_ref[n:n + 1, :] = jnp.sum(kb, axis=0, keepdims=True) * (1.0 / blk)

    q = q_ref[...]
    gate = lax.dot_general(kmean_ref[...], q.astype(F32), _NT, precision=lax.Precision.HIGHEST,
                           preferred_element_type=F32)
    bid = lax.broadcasted_iota(jnp.int32, gate.shape, 0)
    rem = jnp.where(bid < j, gate, -jnp.inf)
    sel = jnp.zeros(gate.shape, jnp.bool_)
    for _ in range(MB_TOPK):
        mx = jnp.max(rem, axis=0, keepdims=True)
        first = jnp.min(jnp.where(rem == mx, bid, nb), axis=0, keepdims=True)
        pick = (bid == first) & (mx > -jnp.inf)
        sel = sel | pick
        rem = jnp.where(bid == first, -jnp.inf, rem)
    bias_ref[...] = jnp.where(sel, 0.0, MASK_VALUE).astype(F32)

    def scores(n):
        koff = pl.multiple_of(n * blk, blk)
        return lax.dot_general(k_ref[pl.ds(koff, blk), :], q, _NT, preferred_element_type=F32)

    m, l, acc = _softmax_first(_causal_mask(scores(j)), vT_ref[j])
    m_ref[...] = m
    l_ref[...] = l
    acc_ref[...] = acc

    def body(n, carry):
        sT = scores(n) + bias_ref[pl.ds(n, 1), :]
        _softmax_update(sT, vT_ref[n], m_ref, l_ref, acc_ref)
        return carry

    lax.fori_loop(0, j, body, 0)

    oT = acc_ref[...] / l_ref[...]
    o_ref[...] = oT.T.astype(o_ref.dtype)


def _moba_attention(qkv, vT, *, B, S, H):
    T = B * S
    blk = MB_BLOCK
    nb = S // blk
    hd = HEAD_DIM
    kern = functools.partial(_mb_kernel, nb=nb)
    return pl.pallas_call(
        kern,
        grid=(B, H, nb),
        in_specs=[
            pl.BlockSpec((blk, hd), lambda b, h, i: (b * nb + i, h)),
            pl.BlockSpec((S, hd), lambda b, h, i: (b, H + h)),
            pl.BlockSpec((None, None, nb, hd, blk), lambda b, h, i: (b, h, 0, 0, 0)),
        ],
        out_specs=pl.BlockSpec((blk, hd), lambda b, h, i: (b * nb + i, h)),
        out_shape=jax.ShapeDtypeStruct((T, H * hd), BF16),
        scratch_shapes=[
            pltpu.VMEM((nb, hd), F32),
            pltpu.VMEM((nb, blk), F32),
            pltpu.VMEM((1, blk), F32),
            pltpu.VMEM((1, blk), F32),
            pltpu.VMEM((hd, blk), F32),
        ],
        compiler_params=_params("parallel", "parallel", "arbitrary"),
        name="moba_attention",
    )(qkv, qkv, vT)


def _value_tiles(v2d, *, B, S, H, dv, tk):
    v = v2d.reshape(B, S // tk, tk, H, dv)
    return v.transpose(0, 3, 1, 4, 2)


def _router(x1, wrT_ref, rb_ref, eidx_ref, wgt_ref):
    E = wrT_ref.shape[0]
    per_group = E // N_GROUPS
    tm = x1.shape[0]
    logits = lax.dot_general(wrT_ref[...], x1, _NT, precision=lax.Precision.HIGHEST,
                             preferred_element_type=F32)
    scores = _sigmoid(logits)
    biased = scores + rb_ref[...]

    gid8 = lax.broadcasted_iota(jnp.int32, (per_group, tm), 0)
    gscores = []
    for g in range(N_GROUPS):
        bg = biased[g * per_group:(g + 1) * per_group, :]
        m1 = jnp.max(bg, axis=0, keepdims=True)
        i1 = jnp.min(jnp.where(bg == m1, gid8, per_group), axis=0, keepdims=True)
        m2 = jnp.max(jnp.where(gid8 == i1, -jnp.inf, bg), axis=0, keepdims=True)
        gscores.append(m1 + m2)
    gs = jnp.concatenate(gscores, axis=0)

    gidx = lax.broadcasted_iota(jnp.int32, (N_GROUPS, tm), 0)
    rank = jnp.zeros((N_GROUPS, tm), jnp.int32)
    for g in range(N_GROUPS):
        row = gs[g:g + 1, :]
        beats = (row > gs) | ((row == gs) & (g < gidx))
        rank = rank + beats.astype(jnp.int32)
    gsel = rank < TOPK_GROUPS

    masked = jnp.concatenate(
        [jnp.where(gsel[g:g + 1, :], biased[g * per_group:(g + 1) * per_group, :], -jnp.inf)
         for g in range(N_GROUPS)], axis=0)

    eid = lax.broadcasted_iota(jnp.int32, (E, tm), 0)
    rem = masked
    ids, vals = [], []
    for _ in range(TOP_K):
        mx = jnp.max(rem, axis=0, keepdims=True)
        first = jnp.min(jnp.where(rem == mx, eid, E), axis=0, keepdims=True)
        onehot = eid == first
        ids.append(first)
        vals.append(jnp.sum(jnp.where(onehot, scores, 0.0), axis=0, keepdims=True))
        rem = jnp.where(onehot, -jnp.inf, rem)
    w = jnp.concatenate(vals, axis=0)
    w = w / jnp.sum(w, axis=0, keepdims=True) * ROUTED_SCALE
    eidx_ref[...] = jnp.concatenate(ids, axis=0)
    wgt_ref[...] = w


def _proj_ln_router_kernel(o_ref, w_ref, x_ref, g_ref, b_ref, wrT_ref, rb_ref,
                           x1_ref, eidx_ref, wgt_ref, *, alpha):
    h = jnp.dot(o_ref[...], w_ref[...], preferred_element_type=F32)
    x1 = _layer_norm(alpha * x_ref[...] + h, g_ref[...], b_ref[...])
    x1_ref[...] = x1
    _router(x1, wrT_ref, rb_ref, eidx_ref, wgt_ref)


def _proj_ln_router(o2d, w_out_bf16, x2d, g, b, wrT, rb, *, alpha, tm):
    T, D = x2d.shape
    Do = o2d.shape[1]
    E = wrT.shape[0]
    kern = functools.partial(_proj_ln_router_kernel, alpha=alpha)
    const = lambda i: (0, 0)
    return pl.pallas_call(
        kern,
        grid=(T // tm,),
        in_specs=[
            pl.BlockSpec((tm, Do), lambda i: (i, 0)),
            pl.BlockSpec((Do, D), const),
            pl.BlockSpec((tm, D), lambda i: (i, 0)),
            pl.BlockSpec((1, D), const),
            pl.BlockSpec((1, D), const),
            pl.BlockSpec((E, D), const),
            pl.BlockSpec((E, 1), const),
        ],
        out_specs=[
            pl.BlockSpec((tm, D), lambda i: (i, 0)),
            pl.BlockSpec((TOP_K, tm), lambda i: (0, i)),
            pl.BlockSpec((TOP_K, tm), lambda i: (0, i)),
        ],
        out_shape=[
            jax.ShapeDtypeStruct((T, D), F32),
            jax.ShapeDtypeStruct((TOP_K, T), jnp.int32),
            jax.ShapeDtypeStruct((TOP_K, T), F32),
        ],
        compiler_params=_params("parallel"),
        name="outproj_ln_router",
    )(o2d, w_out_bf16, x2d, g, b, wrT, rb)


def _routing_plan(eidxT, *, n_experts, tm):
    K, T = eidxT.shape
    n_slots = T * K
    n_tiles = n_slots // tm + n_experts
    e_flat = eidxT.T.reshape(-1)
    order = jnp.argsort(e_flat, stable=True).astype(jnp.int32)
    counts = jnp.sum((e_flat[:, None] == jnp.arange(n_experts, dtype=jnp.int32)[None, :]).astype(jnp.int32), axis=0)
    tiles_per_e = (counts + tm - 1) // tm
    tile_end = jnp.cumsum(tiles_per_e)
    tile_start = tile_end - tiles_per_e
    run_start = jnp.cumsum(counts) - counts
    total_tiles = tile_end[-1]

    tile_ids = jnp.arange(n_tiles, dtype=jnp.int32)
    tile_valid = tile_ids < total_tiles
    last_tile = jnp.maximum(total_tiles - 1, 0)
    tile_e = jnp.searchsorted(tile_end, jnp.minimum(tile_ids, last_tile), side="right").astype(jnp.int32)
    tile_e = jnp.minimum(tile_e, n_experts - 1)

    tile_off = (tile_ids - tile_start[tile_e]) * tm
    n_valid = jnp.where(tile_valid, jnp.clip(counts[tile_e] - tile_off, 0, tm), 0).astype(jnp.int32)
    r = jnp.arange(tm, dtype=jnp.int32)[None, :]
    src = jnp.clip((run_start[tile_e] + tile_off)[:, None] + r, 0, n_slots - 1)
    slots = jnp.where(r < n_valid[:, None], order[src], 0).astype(jnp.int32)
    return tile_e, n_valid, slots


def _experts_kernel(te_ref, nv_ref, slot_hbm, x_hbm, wg_ref, wu_ref, wd_ref, y_hbm,
                    slot_smem, xbuf, ybuf, sems):
    i = pl.program_id(0)
    tm = xbuf.shape[0]
    n_valid = nv_ref[i]

    @pl.when(n_valid > 0)
    def _():
        ids = pltpu.make_async_copy(slot_hbm.at[i], slot_smem, sems.at[0])
        ids.start()
        ids.wait()

        def gather(r, carry):
            tok = slot_smem[r] // TOP_K
            pltpu.make_async_copy(x_hbm.at[pl.ds(tok, 1), :], xbuf.at[pl.ds(r, 1), :], sems.at[1]).start()
            return carry

        lax.fori_loop(0, tm, gather, 0)
        pltpu.make_async_copy(x_hbm.at[pl.ds(0, tm), :], xbuf, sems.at[1]).wait()

        xb = xbuf[...].astype(BF16)
        g = jnp.dot(xb, wg_ref[...], preferred_element_type=F32)
        u = jnp.dot(xb, wu_ref[...], preferred_element_type=F32)
        h = (g * _sigmoid(g) * u).astype(BF16)
        ybuf[...] = jnp.dot(h, wd_ref[...], preferred_element_type=F32)

        def scatter(r, carry):
            pltpu.make_async_copy(ybuf.at[pl.ds(r, 1), :], y_hbm.at[pl.ds(slot_smem[r], 1), :], sems.at[2]).start()
            return carry

        lax.fori_loop(0, n_valid, scatter, 0)
        n_groups = pl.multiple_of((n_valid // SUBLANES) * SUBLANES, SUBLANES)

        @pl.when(n_groups > 0)
        def _():
            pltpu.make_async_copy(ybuf.at[pl.ds(0, n_groups), :], y_hbm.at[pl.ds(0, n_groups), :], sems.at[2]).wait()

        def wait_row(r, carry):
            pltpu.make_async_copy(ybuf.at[pl.ds(0, 1), :], y_hbm.at[pl.ds(0, 1), :], sems.at[2]).wait()
            return carry

        lax.fori_loop(0, n_valid - n_groups, wait_row, 0)


def _routed_experts(x1, tile_e, tile_rows, slots, wg, wu, wd, *, tm):
    T, D = x1.shape
    F = wg.shape[2]
    n_tiles = slots.shape[0]
    grid_spec = pltpu.PrefetchScalarGridSpec(
        num_scalar_prefetch=2,
        grid=(n_tiles,),
        in_specs=[
            pl.BlockSpec(memory_space=pl.ANY),
            pl.BlockSpec(memory_space=pl.ANY),
            pl.BlockSpec((None, D, F), lambda i, te, nv: (te[i], 0, 0)),
            pl.BlockSpec((None, D, F), lambda i, te, nv: (te[i], 0, 0)),
            pl.BlockSpec((None, F, D), lambda i, te, nv: (te[i], 0, 0)),
        ],
        out_specs=pl.BlockSpec(memory_space=pl.ANY),
        scratch_shapes=[
            pltpu.SMEM((tm,), jnp.int32),
            pltpu.VMEM((tm, D), F32),
            pltpu.VMEM((tm, D), F32),
            pltpu.SemaphoreType.DMA((3,)),
        ],
    )
    return pl.pallas_call(
        _experts_kernel,
        grid_spec=grid_spec,
        out_shape=jax.ShapeDtypeStruct((T * TOP_K, D), F32),
        compiler_params=_params("arbitrary"),
        name="routed_experts",
    )(tile_e, tile_rows, slots, x1, wg, wu, wd)


def _combine_kernel(y_ref, w_ref, x1_ref, wsg_ref, wsu_ref, wsd_ref, g_ref, b_ref, out_ref, *, alpha):
    x1 = x1_ref[...]
    D = x1.shape[1]
    xb = x1.astype(BF16)
    gg = jnp.dot(xb, wsg_ref[...], preferred_element_type=F32)
    uu = jnp.dot(xb, wsu_ref[...], preferred_element_type=F32)
    hh = (gg * _sigmoid(gg) * uu).astype(BF16)
    shared = jnp.dot(hh, wsd_ref[...], preferred_element_type=F32)
    w = w_ref[...]
    routed = w[:, 0:1] * y_ref[:, 0:D]
    for k in range(1, TOP_K):
        routed = routed + w[:, k:k + 1] * y_ref[:, k * D:(k + 1) * D]
    out_ref[...] = _layer_norm(alpha * x1 + (routed + shared), g_ref[...], b_ref[...])


def _combine(y_wide, w_tk, x1, wsg, wsu, wsd, g, b, *, alpha, tm):
    T, D = x1.shape
    Fs = wsg.shape[1]
    kern = functools.partial(_combine_kernel, alpha=alpha)
    const = lambda i: (0, 0)
    return pl.pallas_call(
        kern,
        grid=(T // tm,),
        in_specs=[
            pl.BlockSpec((tm, TOP_K * D), lambda i: (i, 0)),
            pl.BlockSpec((tm, TOP_K), lambda i: (i, 0)),
            pl.BlockSpec((tm, D), lambda i: (i, 0)),
            pl.BlockSpec((D, Fs), const),
            pl.BlockSpec((D, Fs), const),
            pl.BlockSpec((Fs, D), const),
            pl.BlockSpec((1, D), const),
            pl.BlockSpec((1, D), const),
        ],
        out_specs=pl.BlockSpec((tm, D), lambda i: (i, 0)),
        out_shape=jax.ShapeDtypeStruct((T, D), F32),
        compiler_params=_params("parallel"),
        name="shared_combine_ln",
    )(y_wide, w_tk, x1, wsg, wsu, wsd, g, b)


def _pick(n, pref):
    t = min(n, pref)
    while n % t:
        t //= 2
    return t


def _moe_layer(x_attn_in, o2d, w_out, ln1_g, ln1_b, w_router, router_bias, w_gate, w_up, w_down,
               ws_gate, ws_up, ws_down, ln2_g, ln2_b, *, alpha):
    T, D = x_attn_in.shape
    E = w_router.shape[1]
    x1, eidxT, wgtT = _proj_ln_router(
        o2d, w_out.astype(BF16), x_attn_in, ln1_g[None, :], ln1_b[None, :],
        w_router.T, router_bias[:, None], alpha=alpha, tm=_pick(T, 256))
    tm_e = _pick(T * TOP_K, 256)
    tile_e, tile_rows, slots = _routing_plan(eidxT, n_experts=E, tm=tm_e)
    y = _routed_experts(x1, tile_e, tile_rows, slots, w_gate.astype(BF16), w_up.astype(BF16),
                        w_down.astype(BF16), tm=tm_e)
    y_wide = y.reshape(T, TOP_K * D)
    return _combine(y_wide, wgtT.T, x1, ws_gate.astype(BF16), ws_up.astype(BF16), ws_down.astype(BF16),
                    ln2_g[None, :], ln2_b[None, :], alpha=alpha, tm=_pick(T, 128))


def kernel(x, positions, ln1_g, ln1_b, ln2_g, ln2_b, da_w_in, da_w_out, da_lq1, da_lk1, da_lq2, da_lk2,
           da_subln_g, mb_w_in, mb_w_out, w_router, router_bias, w_gate, w_up, w_down, ws_gate, ws_up, ws_down):
    B, S, D = x.shape
    T = B * S
    depth = ln1_g.shape[0]
    alpha = (2 * depth) ** 0.25
    hd = HEAD_DIM
    cosf, sinf = _rope_tables(positions)
    xt = x.reshape(T, D)
    tm_proj = _pick(T, 1024)

    for i in range(depth):
        m = i // 2
        if i % 2 == 0:
            w_in = da_w_in[m]
            H = w_in.shape[1] // (6 * hd)
            qk_cols = 4 * H * hd
            tq = _pick(S, 256)
            qkv = _qkv_proj(xt, w_in.astype(BF16), cosf, sinf, rope_cols=qk_cols, q_cols=qk_cols // 2,
                            tm=tm_proj, tn=_pick(qk_cols // 2, 512))
            vT = _value_tiles(qkv[:, qk_cols:], B=B, S=S, H=H, dv=2 * hd, tk=tq)
            lambda_init = 0.8 - 0.6 * math.exp(-0.3 * i)
            lam = (jnp.exp(jnp.sum(da_lq1[m].astype(F32) * da_lk1[m].astype(F32)))
                   - jnp.exp(jnp.sum(da_lq2[m].astype(F32) * da_lk2[m].astype(F32))) + lambda_init)
            o2d = _diff_attention(qkv, vT, lam.reshape(1), da_subln_g[m][None, :], B=B, S=S, H=H, tq=tq,
                                  lambda_init=lambda_init)
            w_out = da_w_out[m]
        else:
            w_in = mb_w_in[m]
            H = w_in.shape[1] // (3 * hd)
            qk_cols = 2 * H * hd
            qkv = _qkv_proj(xt, w_in.astype(BF16), cosf, sinf, rope_cols=qk_cols, q_cols=qk_cols // 2,
                            tm=tm_proj, tn=_pick(qk_cols // 2, 512))
            vT = _value_tiles(qkv[:, qk_cols:], B=B, S=S, H=H, dv=hd, tk=MB_BLOCK)
            o2d = _moba_attention(qkv, vT, B=B, S=S, H=H)
            w_out = mb_w_out[m]
        xt = _moe_layer(xt, o2d, w_out, ln1_g[i], ln1_b[i], w_router[i], router_bias[i], w_gate[i], w_up[i],
                        w_down[i], ws_gate[i], ws_up[i], ws_down[i], ln2_g[i], ln2_b[i], alpha=alpha)
    return xt.reshape(B, S, D)
```

```python
import functools
import math

import jax
import jax.numpy as jnp
from jax import lax
from jax.experimental import pallas as pl
from jax.experimental.pallas import tpu as pltpu

F32 = jnp.float32
BF16 = jnp.bfloat16

HEAD_DIM = 128
ROPE_THETA = 500000.0
ROPE_DIMS = HEAD_DIM // 4
ROPE_HALF = ROPE_DIMS // 2
LN_EPS = 1e-5
MB_BLOCK = 256
MB_TOPK = 3
TOP_K = 8
N_GROUPS = 8
TOPK_GROUPS = 4
ROUTED_SCALE = 2.5
MASK_VALUE = -1e30

LANES = 128
SUBLANES = 8
ISSUE_UNROLL = 8
VMEM_LIMIT_BYTES = 56 * 1024 * 1024

_NT = (((1,), (1,)), ((), ()))


def _params(*sem):
    return pltpu.CompilerParams(dimension_semantics=sem, vmem_limit_bytes=VMEM_LIMIT_BYTES)


def _sigmoid(x):
    return 1.0 / (1.0 + jnp.exp(-x))


def _layer_norm(y, g, b):
    mu = jnp.mean(y, axis=-1, keepdims=True)
    d = y - mu
    var = jnp.mean(d * d, axis=-1, keepdims=True)
    return d * lax.rsqrt(var + LN_EPS) * g + b


def _qkv_kernel(x_ref, w_ref, cos_ref, sin_ref, o_ref, xb_ref, *, rope_blocks, q_blocks, q_scale):
    j = pl.program_id(1)

    @pl.when(j == 0)
    def _():
        xb_ref[...] = x_ref[...].astype(BF16)

    acc = jnp.dot(xb_ref[...], w_ref[...], preferred_element_type=F32)
    tn = acc.shape[1]

    @pl.when(j < rope_blocks)
    def _():
        cosf = cos_ref[...]
        sinf = sin_ref[...]
        lane = lax.broadcasted_iota(jnp.int32, (1, LANES), 1)
        scale = jnp.where(j < q_blocks, q_scale, 1.0).astype(F32)
        for c in range(tn // LANES):
            xc = acc[:, c * LANES:(c + 1) * LANES]
            partner = jnp.where(lane < ROPE_HALF,
                                pltpu.roll(xc, LANES - ROPE_HALF, 1),
                                pltpu.roll(xc, ROPE_HALF, 1))
            r = (xc * cosf + partner * sinf) * scale
            o_ref[:, c * LANES:(c + 1) * LANES] = r.astype(o_ref.dtype)

    @pl.when(j >= rope_blocks)
    def _():
        o_ref[...] = acc.astype(o_ref.dtype)


def _qkv_proj(x2d, w_bf16, cosf, sinf, *, rope_cols, q_cols, tm, tn):
    T, D = x2d.shape
    N = w_bf16.shape[1]
    kern = functools.partial(_qkv_kernel, rope_blocks=rope_cols // tn, q_blocks=q_cols // tn,
                             q_scale=HEAD_DIM ** -0.5)
    return pl.pallas_call(
        kern,
        grid=(T // tm, N // tn),
        in_specs=[
            pl.BlockSpec((tm, D), lambda i, j: (i, 0)),
            pl.BlockSpec((D, tn), lambda i, j: (0, j)),
            pl.BlockSpec((tm, LANES), lambda i, j: (i, 0)),
            pl.BlockSpec((tm, LANES), lambda i, j: (i, 0)),
        ],
        out_specs=pl.BlockSpec((tm, tn), lambda i, j: (i, j)),
        out_shape=jax.ShapeDtypeStruct((T, N), BF16),
        scratch_shapes=[pltpu.VMEM((tm, D), BF16)],
        compiler_params=_params("parallel", "arbitrary"),
        name="qkv_proj",
    )(x2d, w_bf16, cosf, sinf)


def _rope_tables(positions):
    inv_freq = ROPE_THETA ** (-jnp.arange(0, ROPE_DIMS, 2, dtype=F32) / ROPE_DIMS)
    ang = positions.astype(F32).reshape(-1)[:, None] * inv_freq
    cos, sin = jnp.cos(ang), jnp.sin(ang)
    T = ang.shape[0]
    rest = LANES - ROPE_DIMS
    cosf = jnp.concatenate([cos, cos, jnp.ones((T, rest), F32)], axis=1)
    sinf = jnp.concatenate([-sin, sin, jnp.zeros((T, rest), F32)], axis=1)
    return cosf, sinf


def _softmax_first(sT, vT):
    m = jnp.max(sT, axis=0, keepdims=True)
    p = jnp.exp(sT - m)
    l = jnp.sum(p, axis=0, keepdims=True)
    acc = jnp.dot(vT, p.astype(BF16), preferred_element_type=F32)
    return m, l, acc


def _softmax_update(sTs, vTs, m_ref, l_ref, acc_ref):
    m_prev = m_ref[...]
    m_new = m_prev
    for sT in sTs:
        m_new = jnp.maximum(m_new, jnp.max(sT, axis=0, keepdims=True))
    alpha = jnp.exp(m_prev - m_new)
    l = alpha * l_ref[...]
    pv = None
    for sT, vT in zip(sTs, vTs):
        p = jnp.exp(sT - m_new)
        l = l + jnp.sum(p, axis=0, keepdims=True)
        d = jnp.dot(vT, p.astype(BF16), preferred_element_type=F32)
        pv = d if pv is None else pv + d
    l_ref[...] = l
    acc_ref[...] = alpha * acc_ref[...] + pv
    m_ref[...] = m_new


def _causal_mask(sT, key0, qry0):
    tk, tq = sT.shape
    key = key0 + lax.broadcasted_iota(jnp.int32, (tk, tq), 0)
    qry = qry0 + lax.broadcasted_iota(jnp.int32, (tk, tq), 1)
    return jnp.where(key <= qry, sT, MASK_VALUE)


def _da_kernel(lam_ref, q_ref, k_ref, vT_ref, g_ref, o_ref, m_ref, l_ref, acc_ref, *, tq, tk, out_scale):
    qi = pl.program_id(2)
    hd = HEAD_DIM
    gd = (qi * tq) // tk

    def scores(g, c):
        koff = pl.multiple_of(g * tk, tk)
        kblk = k_ref[pl.ds(koff, tk), c * hd:(c + 1) * hd]
        return lax.dot_general(kblk, q_ref[:, c * hd:(c + 1) * hd], _NT, preferred_element_type=F32)

    vT_diag = vT_ref[gd]
    for c in range(2):
        m, l, acc = _softmax_first(_causal_mask(scores(gd, c), gd * tk, qi * tq), vT_diag)
        m_ref[c] = m
        l_ref[c] = l
        acc_ref[c] = acc

    def body(g, carry):
        vT = vT_ref[g]
        for c in range(2):
            _softmax_update([scores(g, c)], [vT], m_ref.at[c], l_ref.at[c], acc_ref.at[c])
        return carry

    lax.fori_loop(0, gd, body, 0)

    lam = lam_ref[0]
    oT = acc_ref[0] / l_ref[0] - lam * (acc_ref[1] / l_ref[1])
    o = oT.T
    o = o * lax.rsqrt(jnp.mean(o * o, axis=-1, keepdims=True) + LN_EPS) * g_ref[...]
    o_ref[...] = (o * out_scale).astype(o_ref.dtype)


def _diff_attention(qkv, vT, lam, subln_g, *, B, S, H, tq, tk, lambda_init):
    T = B * S
    nq = S // tq
    dv = 2 * HEAD_DIM
    kern = functools.partial(_da_kernel, tq=tq, tk=tk, out_scale=1.0 - lambda_init)
    return pl.pallas_call(
        kern,
        grid=(B, H, nq),
        in_specs=[
            pl.BlockSpec(memory_space=pltpu.SMEM),
            pl.BlockSpec((tq, dv), lambda b, h, i: (b * nq + i, h)),
            pl.BlockSpec((S, dv), lambda b, h, i: (b, H + h)),
            pl.BlockSpec((None, None, S // tk, dv, tk), lambda b, h, i: (b, h, 0, 0, 0)),
            pl.BlockSpec((1, dv), lambda b, h, i: (0, 0)),
        ],
        out_specs=pl.BlockSpec((tq, dv), lambda b, h, i: (b * nq + i, h)),
        out_shape=jax.ShapeDtypeStruct((T, H * dv), BF16),
        scratch_shapes=[
            pltpu.VMEM((2, 1, tq), F32),
            pltpu.VMEM((2, 1, tq), F32),
            pltpu.VMEM((2, dv, tq), F32),
        ],
        compiler_params=_params("parallel", "parallel", "arbitrary"),
        name="diff_attention",
    )(lam, qkv, qkv, vT, subln_g)


def _mb_kernel(q_ref, k_ref, vT_ref, o_ref, kmean_ref, bias_ref, m_ref, l_ref, acc_ref, *, nb, group):
    j = pl.program_id(2)
    blk = MB_BLOCK

    @pl.when(j == 0)
    def _():
        for n in range(nb):
            kb = k_ref[n * blk:(n + 1) * blk, :].astype(F32)
            kmean_ref[n:n + 1, :] = jnp.sum(kb, axis=0, keepdims=True) * (1.0 / blk)

    q = q_ref[...]
    gate = lax.dot_general(kmean_ref[...], q.astype(F32), _NT, precision=lax.Precision.HIGHEST,
                           preferred_element_type=F32)
    bid = lax.broadcasted_iota(jnp.int32, gate.shape, 0)
    rem = jnp.where(bid < j, gate, -jnp.inf)
    sel = jnp.zeros(gate.shape, jnp.bool_)
    for _ in range(MB_TOPK):
        mx = jnp.max(rem, axis=0, keepdims=True)
        first = jnp.min(jnp.where(rem == mx, bid, nb), axis=0, keepdims=True)
        pick = (bid == first) & (mx > -jnp.inf)
        sel = sel | pick
        rem = jnp.where(bid == first, -jnp.inf, rem)
    bias = jnp.where(sel, 0.0, MASK_VALUE).astype(F32)
    for gg in range(nb // group):
        bias_ref[gg] = bias[gg * group:(gg + 1) * group, :]

    def scores(n):
        koff = pl.multiple_of(n * blk, blk)
        return lax.dot_general(k_ref[pl.ds(koff, blk), :], q, _NT, preferred_element_type=F32)

    m, l, acc = _softmax_first(_causal_mask(scores(j), 0, 0), vT_ref[j])
    m_ref[...] = m
    l_ref[...] = l
    acc_ref[...] = acc

    def body(g, carry):
        gbias = bias_ref[g]
        sTs = [scores(g * group + t) + gbias[t:t + 1, :] for t in range(group)]
        vTs = [vT_ref[g * group + t] for t in range(group)]
        _softmax_update(sTs, vTs, m_ref, l_ref, acc_ref)
        return carry

    lax.fori_loop(0, (j + group - 1) // group, body, 0)

    oT = acc_ref[...] / l_ref[...]
    o_ref[...] = oT.T.astype(o_ref.dtype)


def _moba_attention(qkv, vT, *, B, S, H, group):
    T = B * S
    blk = MB_BLOCK
    nb = S // blk
    hd = HEAD_DIM
    kern = functools.partial(_mb_kernel, nb=nb, group=group)
    return pl.pallas_call(
        kern,
        grid=(B, H, nb),
        in_specs=[
            pl.BlockSpec((blk, hd), lambda b, h, i: (b * nb + i, h)),
            pl.BlockSpec((S, hd), lambda b, h, i: (b, H + h)),
            pl.BlockSpec((None, None, nb, hd, blk), lambda b, h, i: (b, h, 0, 0, 0)),
        ],
        out_specs=pl.BlockSpec((blk, hd), lambda b, h, i: (b * nb + i, h)),
        out_shape=jax.ShapeDtypeStruct((T, H * hd), BF16),
        scratch_shapes=[
            pltpu.VMEM((nb, hd), F32),
            pltpu.VMEM((nb // group, group, blk), F32),
            pltpu.VMEM((1, blk), F32),
            pltpu.VMEM((1, blk), F32),
            pltpu.VMEM((hd, blk), F32),
        ],
        compiler_params=_params("parallel", "parallel", "arbitrary"),
        name="moba_attention",
    )(qkv, qkv, vT)


def _value_tiles(v2d, *, B, S, H, dv, tk):
    v = v2d.reshape(B, S // tk, tk, H, dv)
    return v.transpose(0, 3, 1, 4, 2)


def _router(x1, wrT_ref, rb_ref, eidx_ref, wgt_ref):
    E = wrT_ref.shape[0]
    per_group = E // N_GROUPS
    tm = x1.shape[0]
    logits = lax.dot_general(wrT_ref[...], x1, _NT, precision=lax.Precision.HIGHEST,
                             preferred_element_type=F32)
    scores = _sigmoid(logits)
    biased = scores + rb_ref[...]

    gid8 = lax.broadcasted_iota(jnp.int32, (per_group, tm), 0)
    gscores = []
    for g in range(N_GROUPS):
        bg = biased[g * per_group:(g + 1) * per_group, :]
        m1 = jnp.max(bg, axis=0, keepdims=True)
        i1 = jnp.min(jnp.where(bg == m1, gid8, per_group), axis=0, keepdims=True)
        m2 = jnp.max(jnp.where(gid8 == i1, -jnp.inf, bg), axis=0, keepdims=True)
        gscores.append(m1 + m2)
    gs = jnp.concatenate(gscores, axis=0)

    gidx = lax.broadcasted_iota(jnp.int32, (N_GROUPS, tm), 0)
    rank = jnp.zeros((N_GROUPS, tm), jnp.int32)
    for g in range(N_GROUPS):
        row = gs[g:g + 1, :]
        beats = (row > gs) | ((row == gs) & (g < gidx))
        rank = rank + beats.astype(jnp.int32)
    gsel = rank < TOPK_GROUPS

    masked = jnp.concatenate(
        [jnp.where(gsel[g:g + 1, :], biased[g * per_group:(g + 1) * per_group, :], -jnp.inf)
         for g in range(N_GROUPS)], axis=0)

    eid = lax.broadcasted_iota(jnp.int32, (E, tm), 0)
    rem = masked
    ids, vals = [], []
    for _ in range(TOP_K):
        mx = jnp.max(rem, axis=0, keepdims=True)
        first = jnp.min(jnp.where(rem == mx, eid, E), axis=0, keepdims=True)
        onehot = eid == first
        ids.append(first)
        vals.append(jnp.sum(jnp.where(onehot, scores, 0.0), axis=0, keepdims=True))
        rem = jnp.where(onehot, -jnp.inf, rem)
    w = jnp.concatenate(vals, axis=0)
    w = w / jnp.sum(w, axis=0, keepdims=True) * ROUTED_SCALE
    eidx_ref[...] = jnp.concatenate(ids, axis=0)
    wgt_ref[...] = w


def _proj_ln_router_kernel(o_ref, w_ref, x_ref, g_ref, b_ref, wrT_ref, rb_ref,
                           x1_ref, eidx_ref, wgt_ref, *, alpha):
    h = jnp.dot(o_ref[...], w_ref[...], preferred_element_type=F32)
    x1 = _layer_norm(alpha * x_ref[...] + h, g_ref[...], b_ref[...])
    x1_ref[...] = x1
    _router(x1, wrT_ref, rb_ref, eidx_ref, wgt_ref)


def _proj_ln_router(o2d, w_out_bf16, x2d, g, b, wrT, rb, *, alpha, tm):
    T, D = x2d.shape
    Do = o2d.shape[1]
    E = wrT.shape[0]
    kern = functools.partial(_proj_ln_router_kernel, alpha=alpha)
    const = lambda i: (0, 0)
    return pl.pallas_call(
        kern,
        grid=(T // tm,),
        in_specs=[
            pl.BlockSpec((tm, Do), lambda i: (i, 0)),
            pl.BlockSpec((Do, D), const),
            pl.BlockSpec((tm, D), lambda i: (i, 0)),
            pl.BlockSpec((1, D), const),
            pl.BlockSpec((1, D), const),
            pl.BlockSpec((E, D), const),
            pl.BlockSpec((E, 1), const),
        ],
        out_specs=[
            pl.BlockSpec((tm, D), lambda i: (i, 0)),
            pl.BlockSpec((TOP_K, tm), lambda i: (0, i)),
            pl.BlockSpec((TOP_K, tm), lambda i: (0, i)),
        ],
        out_shape=[
            jax.ShapeDtypeStruct((T, D), F32),
            jax.ShapeDtypeStruct((TOP_K, T), jnp.int32),
            jax.ShapeDtypeStruct((TOP_K, T), F32),
        ],
        compiler_params=_params("parallel"),
        name="outproj_ln_router",
    )(o2d, w_out_bf16, x2d, g, b, wrT, rb)


def _routing_plan(eidxT, *, n_experts, tm):
    K, T = eidxT.shape
    n_slots = T * K
    n_tiles = n_slots // tm + n_experts
    e_flat = eidxT.reshape(-1)
    order = jnp.argsort(e_flat, stable=True).astype(jnp.int32)
    counts = jnp.sum((e_flat[:, None] == jnp.arange(n_experts, dtype=jnp.int32)[None, :]).astype(jnp.int32), axis=0)
    tiles_per_e = (counts + tm - 1) // tm
    tile_end = jnp.cumsum(tiles_per_e)
    tile_start = tile_end - tiles_per_e
    run_start = jnp.cumsum(counts) - counts
    total_tiles = tile_end[-1]

    tile_ids = jnp.arange(n_tiles, dtype=jnp.int32)
    tile_valid = tile_ids < total_tiles
    last_tile = jnp.maximum(total_tiles - 1, 0)
    tile_e = jnp.sum((jnp.minimum(tile_ids, last_tile)[:, None] >= tile_end[None, :]).astype(jnp.int32), axis=1)
    tile_e = jnp.minimum(tile_e, n_experts - 1)

    tile_off = (tile_ids - tile_start[tile_e]) * tm
    n_valid = jnp.where(tile_valid, jnp.clip(counts[tile_e] - tile_off, 0, tm), 0).astype(jnp.int32)
    r = jnp.arange(tm, dtype=jnp.int32)[None, :]
    src = jnp.clip((run_start[tile_e] + tile_off)[:, None] + r, 0, n_slots - 1)
    slots = jnp.where(r < n_valid[:, None], order[src], 0).astype(jnp.int32)
    return tile_e, n_valid, lax.rem(slots, T), slots


def _experts_kernel(te_ref, nv_ref, tok_hbm, slot_hbm, x_hbm, wg_ref, wu_ref, wd_ref, y_hbm,
                    tok_smem, slot_smem, xbuf, ybuf, id_sems, gather_sems, scatter_sems):
    i = pl.program_id(0)
    n_tiles = pl.num_programs(0)
    tm = xbuf.shape[1]
    cur = lax.rem(i, 2)
    nxt = 1 - cur
    n_valid = nv_ref[i]
    n_next = jnp.where(i + 1 < n_tiles, nv_ref[jnp.minimum(i + 1, n_tiles - 1)], 0)

    def id_copies(tile, buf):
        return (pltpu.make_async_copy(tok_hbm.at[tile], tok_smem.at[buf], id_sems.at[0]),
                pltpu.make_async_copy(slot_hbm.at[tile], slot_smem.at[buf], id_sems.at[1]))

    def start_gather(buf):
        def row(r, carry):
            tok = tok_smem[buf, r]
            pltpu.make_async_copy(x_hbm.at[pl.ds(tok, 1), :], xbuf.at[buf, pl.ds(r, 1), :],
                                  gather_sems.at[buf]).start()
            return carry
        lax.fori_loop(0, tm, row, 0, unroll=ISSUE_UNROLL)

    def wait_gather(buf):
        pltpu.make_async_copy(x_hbm.at[pl.ds(0, tm), :], xbuf.at[buf], gather_sems.at[buf]).wait()

    def start_scatter(buf, n):
        def row(r, carry):
            pltpu.make_async_copy(ybuf.at[buf, pl.ds(r, 1), :], y_hbm.at[pl.ds(slot_smem[buf, r], 1), :],
                                  scatter_sems.at[buf]).start()
            return carry
        lax.fori_loop(0, n, row, 0)

    def wait_scatter(buf, n):
        n_groups = pl.multiple_of((n // SUBLANES) * SUBLANES, SUBLANES)

        @pl.when(n_groups > 0)
        def _():
            pltpu.make_async_copy(ybuf.at[buf, pl.ds(0, n_groups), :], y_hbm.at[pl.ds(0, n_groups), :],
                                  scatter_sems.at[buf]).wait()

        def row(r, carry):
            pltpu.make_async_copy(ybuf.at[buf, pl.ds(0, 1), :], y_hbm.at[pl.ds(0, 1), :],
                                  scatter_sems.at[buf]).wait()
            return carry
        lax.fori_loop(0, n - n_groups, row, 0)

    @pl.when((i == 0) & (n_valid > 0))
    def _():
        for cp in id_copies(0, 0):
            cp.start()
        for cp in id_copies(0, 0):
            cp.wait()
        start_gather(0)

    @pl.when(n_valid > 0)
    def _():
        @pl.when(n_next > 0)
        def _():
            for cp in id_copies(i + 1, nxt):
                cp.start()

        wait_gather(cur)

        @pl.when(n_next > 0)
        def _():
            for cp in id_copies(i + 1, nxt):
                cp.wait()
            start_gather(nxt)

        xb = xbuf[cur].astype(BF16)
        g = jnp.dot(xb, wg_ref[...], preferred_element_type=F32)
        u = jnp.dot(xb, wu_ref[...], preferred_element_type=F32)
        h = (g * _sigmoid(g) * u).astype(BF16)
        ybuf[cur] = jnp.dot(h, wd_ref[...], preferred_element_type=F32)

        @pl.when(i > 0)
        def _():
            wait_scatter(nxt, nv_ref[jnp.maximum(i - 1, 0)])

        start_scatter(cur, n_valid)

        @pl.when(n_next == 0)
        def _():
            wait_scatter(cur, n_valid)


def _routed_experts(x1, tile_e, tile_rows, toks, slots, wg, wu, wd, *, tm):
    T, D = x1.shape
    F = wg.shape[2]
    n_tiles = slots.shape[0]
    grid_spec = pltpu.PrefetchScalarGridSpec(
        num_scalar_prefetch=2,
        grid=(n_tiles,),
        in_specs=[
            pl.BlockSpec(memory_space=pl.ANY),
            pl.BlockSpec(memory_space=pl.ANY),
            pl.BlockSpec(memory_space=pl.ANY),
            pl.BlockSpec((None, D, F), lambda i, te, nv: (te[i], 0, 0)),
            pl.BlockSpec((None, D, F), lambda i, te, nv: (te[i], 0, 0)),
            pl.BlockSpec((None, F, D), lambda i, te, nv: (te[i], 0, 0)),
        ],
        out_specs=pl.BlockSpec(memory_space=pl.ANY),
        scratch_shapes=[
            pltpu.SMEM((2, tm), jnp.int32),
            pltpu.SMEM((2, tm), jnp.int32),
            pltpu.VMEM((2, tm, D), F32),
            pltpu.VMEM((2, tm, D), F32),
            pltpu.SemaphoreType.DMA((2,)),
            pltpu.SemaphoreType.DMA((2,)),
            pltpu.SemaphoreType.DMA((2,)),
        ],
    )
    return pl.pallas_call(
        _experts_kernel,
        grid_spec=grid_spec,
        out_shape=jax.ShapeDtypeStruct((T * TOP_K, D), F32),
        compiler_params=_params("arbitrary"),
        name="routed_experts",
    )(tile_e, tile_rows, toks, slots, x1, wg, wu, wd)


def _combine_kernel(y_ref, w_ref, x1_ref, wsg_ref, wsu_ref, wsd_ref, g_ref, b_ref, out_ref, *, alpha):
    x1 = x1_ref[...]
    xb = x1.astype(BF16)
    gg = jnp.dot(xb, wsg_ref[...], preferred_element_type=F32)
    uu = jnp.dot(xb, wsu_ref[...], preferred_element_type=F32)
    hh = (gg * _sigmoid(gg) * uu).astype(BF16)
    shared = jnp.dot(hh, wsd_ref[...], preferred_element_type=F32)
    w = w_ref[...]
    routed = w[:, 0:1] * y_ref[0]
    for k in range(1, TOP_K):
        routed = routed + w[:, k:k + 1] * y_ref[k]
    out_ref[...] = _layer_norm(alpha * x1 + (routed + shared), g_ref[...], b_ref[...])


def _combine(y_kt, w_tk, x1, wsg, wsu, wsd, g, b, *, alpha, tm):
    T, D = x1.shape
    Fs = wsg.shape[1]
    kern = functools.partial(_combine_kernel, alpha=alpha)
    const = lambda i: (0, 0)
    return pl.pallas_call(
        kern,
        grid=(T // tm,),
        in_specs=[
            pl.BlockSpec((TOP_K, tm, D), lambda i: (0, i, 0)),
            pl.BlockSpec((tm, TOP_K), lambda i: (i, 0)),
            pl.BlockSpec((tm, D), lambda i: (i, 0)),
            pl.BlockSpec((D, Fs), const),
            pl.BlockSpec((D, Fs), const),
            pl.BlockSpec((Fs, D), const),
            pl.BlockSpec((1, D), const),
            pl.BlockSpec((1, D), const),
        ],
        out_specs=pl.BlockSpec((tm, D), lambda i: (i, 0)),
        out_shape=jax.ShapeDtypeStruct((T, D), F32),
        compiler_params=_params("parallel"),
        name="shared_combine_ln",
    )(y_kt, w_tk, x1, wsg, wsu, wsd, g, b)


def _pick(n, pref):
    t = min(n, pref)
    while n % t:
        t //= 2
    return t


def _moe_layer(x_attn_in, o2d, w_out, ln1_g, ln1_b, w_router, router_bias, w_gate, w_up, w_down,
               ws_gate, ws_up, ws_down, ln2_g, ln2_b, *, alpha):
    T, D = x_attn_in.shape
    E = w_router.shape[1]
    x1, eidxT, wgtT = _proj_ln_router(
        o2d, w_out.astype(BF16), x_attn_in, ln1_g[None, :], ln1_b[None, :],
        w_router.T, router_bias[:, None], alpha=alpha, tm=_pick(T, 256))
    tm_e = _pick(T * TOP_K, 256)
    tile_e, tile_rows, toks, slots = _routing_plan(eidxT, n_experts=E, tm=tm_e)
    y = _routed_experts(x1, tile_e, tile_rows, toks, slots, w_gate.astype(BF16), w_up.astype(BF16),
                        w_down.astype(BF16), tm=tm_e)
    return _combine(y.reshape(TOP_K, T, D), wgtT.T, x1, ws_gate.astype(BF16), ws_up.astype(BF16), ws_down.astype(BF16),
                    ln2_g[None, :], ln2_b[None, :], alpha=alpha, tm=_pick(T, 128))


def kernel(x, positions, ln1_g, ln1_b, ln2_g, ln2_b, da_w_in, da_w_out, da_lq1, da_lk1, da_lq2, da_lk2,
           da_subln_g, mb_w_in, mb_w_out, w_router, router_bias, w_gate, w_up, w_down, ws_gate, ws_up, ws_down):
    B, S, D = x.shape
    T = B * S
    depth = ln1_g.shape[0]
    alpha = (2 * depth) ** 0.25
    hd = HEAD_DIM
    cosf, sinf = _rope_tables(positions)
    xt = x.reshape(T, D)
    tm_proj = _pick(T, 1024)

    for i in range(depth):
        m = i // 2
        if i % 2 == 0:
            w_in = da_w_in[m]
            H = w_in.shape[1] // (6 * hd)
            qk_cols = 4 * H * hd
            tq = _pick(S, 256)
            qkv = _qkv_proj(xt, w_in.astype(BF16), cosf, sinf, rope_cols=qk_cols, q_cols=qk_cols // 2,
                            tm=tm_proj, tn=_pick(qk_cols // 2, 512))
            tk = _pick(S, 1024)
            vT = _value_tiles(qkv[:, qk_cols:], B=B, S=S, H=H, dv=2 * hd, tk=tk)
            lambda_init = 0.8 - 0.6 * math.exp(-0.3 * i)
            lam = (jnp.exp(jnp.sum(da_lq1[m].astype(F32) * da_lk1[m].astype(F32)))
                   - jnp.exp(jnp.sum(da_lq2[m].astype(F32) * da_lk2[m].astype(F32))) + lambda_init)
            o2d = _diff_attention(qkv, vT, lam.reshape(1), da_subln_g[m][None, :], B=B, S=S, H=H, tq=tq,
                                  tk=tk, lambda_init=lambda_init)
            w_out = da_w_out[m]
        else:
            w_in = mb_w_in[m]
            H = w_in.shape[1] // (3 * hd)
            qk_cols = 2 * H * hd
            qkv = _qkv_proj(xt, w_in.astype(BF16), cosf, sinf, rope_cols=qk_cols, q_cols=qk_cols // 2,
                            tm=tm_proj, tn=_pick(qk_cols // 2, 512))
            vT = _value_tiles(qkv[:, qk_cols:], B=B, S=S, H=H, dv=hd, tk=MB_BLOCK)
            o2d = _moba_attention(qkv, vT, B=B, S=S, H=H, group=_pick(S // MB_BLOCK, 4))
            w_out = mb_w_out[m]
        xt = _moe_layer(xt, o2d, w_out, ln1_g[i], ln1_b[i], w_router[i], router_bias[i], w_gate[i], w_up[i],
                        w_down[i], ws_gate[i], ws_up[i], ws_down[i], ln2_g[i], ln2_b[i], alpha=alpha)
    return xt.reshape(B, S, D)
```

```python
import functools
import math

import jax
import jax.numpy as jnp
from jax import lax
from jax.experimental import pallas as pl
from jax.experimental.pallas import tpu as pltpu

F32 = jnp.float32
BF16 = jnp.bfloat16

HEAD_DIM = 128
ROPE_THETA = 500000.0
ROPE_DIMS = HEAD_DIM // 4
ROPE_HALF = ROPE_DIMS // 2
LN_EPS = 1e-5
MB_BLOCK = 256
MB_TOPK = 3
TOP_K = 8
N_GROUPS = 8
TOPK_GROUPS = 4
ROUTED_SCALE = 2.5
MASK_VALUE = -1e30

LANES = 128
SUBLANES = 8
ISSUE_UNROLL = 8
VMEM_LIMIT_BYTES = 56 * 1024 * 1024

_NT = (((1,), (1,)), ((), ()))


def _params(*sem):
    return pltpu.CompilerParams(dimension_semantics=sem, vmem_limit_bytes=VMEM_LIMIT_BYTES)


def _sigmoid(x):
    return 1.0 / (1.0 + jnp.exp(-x))


def _pack_rows(x, o_ref):
    half = x.shape[1] // 2
    lo = pltpu.bitcast(x[:, :half].astype(BF16).astype(F32), jnp.uint32) >> 16
    hi = pltpu.bitcast(x[:, half:].astype(BF16).astype(F32), jnp.uint32) & jnp.uint32(0xFFFF0000)
    packed = hi | lo
    for s in range(half // LANES):
        o_ref[:, s, :] = packed[:, s * LANES:(s + 1) * LANES]


def _unpack_rows(p_ref):
    p = jnp.concatenate([p_ref[:, s, :] for s in range(p_ref.shape[1])], axis=1)
    return pltpu.bitcast(p << 16, F32), pltpu.bitcast(p & jnp.uint32(0xFFFF0000), F32)


def _layer_norm(y, g, b):
    mu = jnp.mean(y, axis=-1, keepdims=True)
    d = y - mu
    var = jnp.mean(d * d, axis=-1, keepdims=True)
    return d * lax.rsqrt(var + LN_EPS) * g + b


def _qkv_kernel(x_ref, w_ref, cos_ref, sin_ref, o_ref, xb_ref, *, rope_blocks, q_blocks, q_scale):
    j = pl.program_id(1)

    @pl.when(j == 0)
    def _():
        xb_ref[...] = x_ref[...].astype(BF16)

    acc = jnp.dot(xb_ref[...], w_ref[...], preferred_element_type=F32)
    tn = acc.shape[1]

    @pl.when(j < rope_blocks)
    def _():
        cosf = cos_ref[...]
        sinf = sin_ref[...]
        lane = lax.broadcasted_iota(jnp.int32, (1, LANES), 1)
        scale = jnp.where(j < q_blocks, q_scale, 1.0).astype(F32)
        for c in range(tn // LANES):
            xc = acc[:, c * LANES:(c + 1) * LANES]
            partner = jnp.where(lane < ROPE_HALF,
                                pltpu.roll(xc, LANES - ROPE_HALF, 1),
                                pltpu.roll(xc, ROPE_HALF, 1))
            r = (xc * cosf + partner * sinf) * scale
            o_ref[:, c * LANES:(c + 1) * LANES] = r.astype(o_ref.dtype)

    @pl.when(j >= rope_blocks)
    def _():
        o_ref[...] = acc.astype(o_ref.dtype)


def _qkv_proj(x2d, w_bf16, cosf, sinf, *, rope_cols, q_cols, tm, tn):
    T, D = x2d.shape
    N = w_bf16.shape[1]
    kern = functools.partial(_qkv_kernel, rope_blocks=rope_cols // tn, q_blocks=q_cols // tn,
                             q_scale=HEAD_DIM ** -0.5)
    return pl.pallas_call(
        kern,
        grid=(T // tm, N // tn),
        in_specs=[
            pl.BlockSpec((tm, D), lambda i, j: (i, 0)),
            pl.BlockSpec((D, tn), lambda i, j: (0, j)),
            pl.BlockSpec((tm, LANES), lambda i, j: (i, 0)),
            pl.BlockSpec((tm, LANES), lambda i, j: (i, 0)),
        ],
        out_specs=pl.BlockSpec((tm, tn), lambda i, j: (i, j)),
        out_shape=jax.ShapeDtypeStruct((T, N), BF16),
        scratch_shapes=[pltpu.VMEM((tm, D), BF16)],
        compiler_params=_params("parallel", "arbitrary"),
        name="qkv_proj",
    )(x2d, w_bf16, cosf, sinf)


def _rope_tables(positions):
    inv_freq = ROPE_THETA ** (-jnp.arange(0, ROPE_DIMS, 2, dtype=F32) / ROPE_DIMS)
    ang = positions.astype(F32).reshape(-1)[:, None] * inv_freq
    cos, sin = jnp.cos(ang), jnp.sin(ang)
    T = ang.shape[0]
    rest = LANES - ROPE_DIMS
    cosf = jnp.concatenate([cos, cos, jnp.ones((T, rest), F32)], axis=1)
    sinf = jnp.concatenate([-sin, sin, jnp.zeros((T, rest), F32)], axis=1)
    return cosf, sinf


def _softmax_first(sT, vT):
    m = jnp.max(sT, axis=0, keepdims=True)
    p = jnp.exp(sT - m)
    l = jnp.sum(p, axis=0, keepdims=True)
    acc = jnp.dot(vT, p.astype(BF16), preferred_element_type=F32)
    return m, l, acc


def _softmax_update(sTs, vTs, m_ref, l_ref, acc_ref):
    m_prev = m_ref[...]
    m_new = m_prev
    for sT in sTs:
        m_new = jnp.maximum(m_new, jnp.max(sT, axis=0, keepdims=True))
    alpha = jnp.exp(m_prev - m_new)
    l = alpha * l_ref[...]
    pv = None
    for sT, vT in zip(sTs, vTs):
        p = jnp.exp(sT - m_new)
        l = l + jnp.sum(p, axis=0, keepdims=True)
        d = jnp.dot(vT, p.astype(BF16), preferred_element_type=F32)
        pv = d if pv is None else pv + d
    l_ref[...] = l
    acc_ref[...] = alpha * acc_ref[...] + pv
    m_ref[...] = m_new


def _causal_mask(sT, key0, qry0):
    tk, tq = sT.shape
    key = key0 + lax.broadcasted_iota(jnp.int32, (tk, tq), 0)
    qry = qry0 + lax.broadcasted_iota(jnp.int32, (tk, tq), 1)
    return jnp.where(key <= qry, sT, MASK_VALUE)


def _da_kernel(lam_ref, q_ref, k_ref, vT_ref, g_ref, o_ref, m_ref, l_ref, acc_ref, *, tq, tk, out_scale):
    qi = pl.program_id(2)
    hd = HEAD_DIM
    gd = (qi * tq) // tk

    def scores(g, c):
        koff = pl.multiple_of(g * tk, tk)
        kblk = k_ref[pl.ds(koff, tk), c * hd:(c + 1) * hd]
        return lax.dot_general(kblk, q_ref[:, c * hd:(c + 1) * hd], _NT, preferred_element_type=F32)

    vT_diag = vT_ref[gd]
    for c in range(2):
        m, l, acc = _softmax_first(_causal_mask(scores(gd, c), gd * tk, qi * tq), vT_diag)
        m_ref[c] = m
        l_ref[c] = l
        acc_ref[c] = acc

    def body(g, carry):
        vT = vT_ref[g]
        for c in range(2):
            _softmax_update([scores(g, c)], [vT], m_ref.at[c], l_ref.at[c], acc_ref.at[c])
        return carry

    lax.fori_loop(0, gd, body, 0)

    lam = lam_ref[0]
    oT = acc_ref[0] / l_ref[0] - lam * (acc_ref[1] / l_ref[1])
    o = oT.T
    o = o * lax.rsqrt(jnp.mean(o * o, axis=-1, keepdims=True) + LN_EPS) * g_ref[...]
    o_ref[...] = (o * out_scale).astype(o_ref.dtype)


def _diff_attention(qkv, vT, lam, subln_g, *, B, S, H, tq, tk, lambda_init):
    T = B * S
    nq = S // tq
    dv = 2 * HEAD_DIM
    kern = functools.partial(_da_kernel, tq=tq, tk=tk, out_scale=1.0 - lambda_init)
    return pl.pallas_call(
        kern,
        grid=(B, H, nq),
        in_specs=[
            pl.BlockSpec(memory_space=pltpu.SMEM),
            pl.BlockSpec((tq, dv), lambda b, h, i: (b * nq + i, h)),
            pl.BlockSpec((S, dv), lambda b, h, i: (b, H + h)),
            pl.BlockSpec((None, None, S // tk, dv, tk), lambda b, h, i: (b, h, 0, 0, 0)),
            pl.BlockSpec((1, dv), lambda b, h, i: (0, 0)),
        ],
        out_specs=pl.BlockSpec((tq, dv), lambda b, h, i: (b * nq + i, h)),
        out_shape=jax.ShapeDtypeStruct((T, H * dv), BF16),
        scratch_shapes=[
            pltpu.VMEM((2, 1, tq), F32),
            pltpu.VMEM((2, 1, tq), F32),
            pltpu.VMEM((2, dv, tq), F32),
        ],
        compiler_params=_params("parallel", "parallel", "arbitrary"),
        name="diff_attention",
    )(lam, qkv, qkv, vT, subln_g)


def _mb_kernel(q_ref, k_ref, vT_ref, o_ref, kmean_ref, bias_ref, m_ref, l_ref, acc_ref, *, nb, group):
    j = pl.program_id(2)
    blk = MB_BLOCK

    @pl.when(j == 0)
    def _():
        for n in range(nb):
            kb = k_ref[n * blk:(n + 1) * blk, :].astype(F32)
            kmean_ref[n:n + 1, :] = jnp.sum(kb, axis=0, keepdims=True) * (1.0 / blk)

    q = q_ref[...]
    gate = lax.dot_general(kmean_ref[...], q.astype(F32), _NT, precision=lax.Precision.HIGHEST,
                           preferred_element_type=F32)
    bid = lax.broadcasted_iota(jnp.int32, gate.shape, 0)
    rem = jnp.where(bid < j, gate, -jnp.inf)
    sel = jnp.zeros(gate.shape, jnp.bool_)
    for _ in range(MB_TOPK):
        mx = jnp.max(rem, axis=0, keepdims=True)
        first = jnp.min(jnp.where(rem == mx, bid, nb), axis=0, keepdims=True)
        pick = (bid == first) & (mx > -jnp.inf)
        sel = sel | pick
        rem = jnp.where(bid == first, -jnp.inf, rem)
    bias = jnp.where(sel, 0.0, MASK_VALUE).astype(F32)
    for gg in range(nb // group):
        bias_ref[gg] = bias[gg * group:(gg + 1) * group, :]

    def scores(n):
        koff = pl.multiple_of(n * blk, blk)
        return lax.dot_general(k_ref[pl.ds(koff, blk), :], q, _NT, preferred_element_type=F32)

    m, l, acc = _softmax_first(_causal_mask(scores(j), 0, 0), vT_ref[j])
    m_ref[...] = m
    l_ref[...] = l
    acc_ref[...] = acc

    def body(g, carry):
        gbias = bias_ref[g]
        sTs = [scores(g * group + t) + gbias[t:t + 1, :] for t in range(group)]
        vTs = [vT_ref[g * group + t] for t in range(group)]
        _softmax_update(sTs, vTs, m_ref, l_ref, acc_ref)
        return carry

    lax.fori_loop(0, (j + group - 1) // group, body, 0)

    oT = acc_ref[...] / l_ref[...]
    o_ref[...] = oT.T.astype(o_ref.dtype)


def _moba_attention(qkv, vT, *, B, S, H, group):
    T = B * S
    blk = MB_BLOCK
    nb = S // blk
    hd = HEAD_DIM
    kern = functools.partial(_mb_kernel, nb=nb, group=group)
    return pl.pallas_call(
        kern,
        grid=(B, H, nb),
        in_specs=[
            pl.BlockSpec((blk, hd), lambda b, h, i: (b * nb + i, h)),
            pl.BlockSpec((S, hd), lambda b, h, i: (b, H + h)),
            pl.BlockSpec((None, None, nb, hd, blk), lambda b, h, i: (b, h, 0, 0, 0)),
        ],
        out_specs=pl.BlockSpec((blk, hd), lambda b, h, i: (b * nb + i, h)),
        out_shape=jax.ShapeDtypeStruct((T, H * hd), BF16),
        scratch_shapes=[
            pltpu.VMEM((nb, hd), F32),
            pltpu.VMEM((nb // group, group, blk), F32),
            pltpu.VMEM((1, blk), F32),
            pltpu.VMEM((1, blk), F32),
            pltpu.VMEM((hd, blk), F32),
        ],
        compiler_params=_params("parallel", "parallel", "arbitrary"),
        name="moba_attention",
    )(qkv, qkv, vT)


def _value_tiles(v2d, *, B, S, H, dv, tk):
    v = v2d.reshape(B, S // tk, tk, H, dv)
    return v.transpose(0, 3, 1, 4, 2)


def _router(x1, wrT_ref, rb_ref, eidx_ref, wgt_ref):
    E = wrT_ref.shape[0]
    per_group = E // N_GROUPS
    tm = x1.shape[0]
    logits = lax.dot_general(wrT_ref[...], x1, _NT, precision=lax.Precision.HIGHEST,
                             preferred_element_type=F32)
    scores = _sigmoid(logits)
    biased = scores + rb_ref[...]

    gid8 = lax.broadcasted_iota(jnp.int32, (per_group, tm), 0)
    gscores = []
    for g in range(N_GROUPS):
        bg = biased[g * per_group:(g + 1) * per_group, :]
        m1 = jnp.max(bg, axis=0, keepdims=True)
        i1 = jnp.min(jnp.where(bg == m1, gid8, per_group), axis=0, keepdims=True)
        m2 = jnp.max(jnp.where(gid8 == i1, -jnp.inf, bg), axis=0, keepdims=True)
        gscores.append(m1 + m2)
    gs = jnp.concatenate(gscores, axis=0)

    gidx = lax.broadcasted_iota(jnp.int32, (N_GROUPS, tm), 0)
    rank = jnp.zeros((N_GROUPS, tm), jnp.int32)
    for g in range(N_GROUPS):
        row = gs[g:g + 1, :]
        beats = (row > gs) | ((row == gs) & (g < gidx))
        rank = rank + beats.astype(jnp.int32)
    gsel = rank < TOPK_GROUPS

    masked = jnp.concatenate(
        [jnp.where(gsel[g:g + 1, :], biased[g * per_group:(g + 1) * per_group, :], -jnp.inf)
         for g in range(N_GROUPS)], axis=0)

    eid = lax.broadcasted_iota(jnp.int32, (E, tm), 0)
    rem = masked
    ids, vals = [], []
    for _ in range(TOP_K):
        mx = jnp.max(rem, axis=0, keepdims=True)
        first = jnp.min(jnp.where(rem == mx, eid, E), axis=0, keepdims=True)
        onehot = eid == first
        ids.append(first)
        vals.append(jnp.sum(jnp.where(onehot, scores, 0.0), axis=0, keepdims=True))
        rem = jnp.where(onehot, -jnp.inf, rem)
    w = jnp.concatenate(vals, axis=0)
    w = w / jnp.sum(w, axis=0, keepdims=True) * ROUTED_SCALE
    eidx_ref[...] = jnp.concatenate(ids, axis=0)
    wgt_ref[...] = w


def _proj_ln_router_kernel(o_ref, w_ref, x_ref, g_ref, b_ref, wrT_ref, rb_ref,
                           x1_ref, x1p_ref, eidx_ref, wgt_ref, *, alpha):
    h = jnp.dot(o_ref[...], w_ref[...], preferred_element_type=F32)
    x1 = _layer_norm(alpha * x_ref[...] + h, g_ref[...], b_ref[...])
    x1_ref[...] = x1
    _pack_rows(x1, x1p_ref)
    _router(x1, wrT_ref, rb_ref, eidx_ref, wgt_ref)


def _proj_ln_router(o2d, w_out_bf16, x2d, g, b, wrT, rb, *, alpha, tm):
    T, D = x2d.shape
    Do = o2d.shape[1]
    E = wrT.shape[0]
    kern = functools.partial(_proj_ln_router_kernel, alpha=alpha)
    const = lambda i: (0, 0)
    return pl.pallas_call(
        kern,
        grid=(T // tm,),
        in_specs=[
            pl.BlockSpec((tm, Do), lambda i: (i, 0)),
            pl.BlockSpec((Do, D), const),
            pl.BlockSpec((tm, D), lambda i: (i, 0)),
            pl.BlockSpec((1, D), const),
            pl.BlockSpec((1, D), const),
            pl.BlockSpec((E, D), const),
            pl.BlockSpec((E, 1), const),
        ],
        out_specs=[
            pl.BlockSpec((tm, D), lambda i: (i, 0)),
            pl.BlockSpec((tm, D // (2 * LANES), LANES), lambda i: (i, 0, 0)),
            pl.BlockSpec((TOP_K, tm), lambda i: (0, i)),
            pl.BlockSpec((TOP_K, tm), lambda i: (0, i)),
        ],
        out_shape=[
            jax.ShapeDtypeStruct((T, D), F32),
            jax.ShapeDtypeStruct((T, D // (2 * LANES), LANES), jnp.uint32),
            jax.ShapeDtypeStruct((TOP_K, T), jnp.int32),
            jax.ShapeDtypeStruct((TOP_K, T), F32),
        ],
        compiler_params=_params("parallel"),
        name="outproj_ln_router",
    )(o2d, w_out_bf16, x2d, g, b, wrT, rb)


def _routing_plan(eidxT, *, n_experts, tm):
    K, T = eidxT.shape
    n_slots = T * K
    n_tiles = n_slots // tm + n_experts
    e_flat = eidxT.reshape(-1)
    order = jnp.argsort(e_flat, stable=True).astype(jnp.int32)
    counts = jnp.sum((e_flat[:, None] == jnp.arange(n_experts, dtype=jnp.int32)[None, :]).astype(jnp.int32), axis=0)
    tiles_per_e = (counts + tm - 1) // tm
    tile_end = jnp.cumsum(tiles_per_e)
    tile_start = tile_end - tiles_per_e
    run_start = jnp.cumsum(counts) - counts
    total_tiles = tile_end[-1]

    tile_ids = jnp.arange(n_tiles, dtype=jnp.int32)
    tile_valid = tile_ids < total_tiles
    last_tile = jnp.maximum(total_tiles - 1, 0)
    tile_e = jnp.sum((jnp.minimum(tile_ids, last_tile)[:, None] >= tile_end[None, :]).astype(jnp.int32), axis=1)
    tile_e = jnp.minimum(tile_e, n_experts - 1)

    tile_off = (tile_ids - tile_start[tile_e]) * tm
    n_valid = jnp.where(tile_valid, jnp.clip(counts[tile_e] - tile_off, 0, tm), 0).astype(jnp.int32)
    r = jnp.arange(tm, dtype=jnp.int32)[None, :]
    src = jnp.clip((run_start[tile_e] + tile_off)[:, None] + r, 0, n_slots - 1)
    slots = jnp.where(r < n_valid[:, None], order[src], 0).astype(jnp.int32)
    return tile_e, n_valid, lax.rem(slots, T), slots


def _experts_kernel(te_ref, nv_ref, tok_hbm, slot_hbm, x_hbm, wg_ref, wu_ref, wd_ref, y_hbm,
                    tok_smem, slot_smem, xbuf, ybuf, id_sems, gather_sems, scatter_sems):
    i = pl.program_id(0)
    n_tiles = pl.num_programs(0)
    tm = xbuf.shape[1]
    cur = lax.rem(i, 2)
    nxt = 1 - cur
    n_valid = nv_ref[i]
    n_next = jnp.where(i + 1 < n_tiles, nv_ref[jnp.minimum(i + 1, n_tiles - 1)], 0)

    def id_copies(tile, buf):
        return (pltpu.make_async_copy(tok_hbm.at[tile], tok_smem.at[buf], id_sems.at[0]),
                pltpu.make_async_copy(slot_hbm.at[tile], slot_smem.at[buf], id_sems.at[1]))

    def start_gather(buf):
        def row(r, carry):
            tok = tok_smem[buf, r]
            pltpu.make_async_copy(x_hbm.at[tok], xbuf.at[buf, r], gather_sems.at[buf]).start()
            return carry
        lax.fori_loop(0, tm, row, 0, unroll=ISSUE_UNROLL)

    def wait_gather(buf):
        pltpu.make_async_copy(x_hbm.at[pl.ds(0, tm)], xbuf.at[buf], gather_sems.at[buf]).wait()

    def start_scatter(buf, n):
        def row(r):
            pltpu.make_async_copy(ybuf.at[buf, r], y_hbm.at[slot_smem[buf, r]], scatter_sems.at[buf]).start()

        def group(g, carry):
            for u in range(ISSUE_UNROLL):
                row(g * ISSUE_UNROLL + u)
            return carry

        def single(r, carry):
            row(r)
            return carry

        n_groups = n // ISSUE_UNROLL
        lax.fori_loop(0, n_groups, group, 0)
        lax.fori_loop(n_groups * ISSUE_UNROLL, n, single, 0)

    def wait_scatter(buf, n):
        pltpu.make_async_copy(ybuf.at[buf, pl.ds(0, n)], y_hbm.at[pl.ds(0, n)], scatter_sems.at[buf]).wait()

    @pl.when((i == 0) & (n_valid > 0))
    def _():
        for cp in id_copies(0, 0):
            cp.start()
        for cp in id_copies(0, 0):
            cp.wait()
        start_gather(0)

    @pl.when(n_valid > 0)
    def _():
        @pl.when(n_next > 0)
        def _():
            for cp in id_copies(i + 1, nxt):
                cp.start()

        wait_gather(cur)

        @pl.when(n_next > 0)
        def _():
            for cp in id_copies(i + 1, nxt):
                cp.wait()
            start_gather(nxt)

        x_lo, x_hi = _unpack_rows(xbuf.at[cur])
        xb = jnp.concatenate([x_lo, x_hi], axis=1).astype(BF16)
        g = jnp.dot(xb, wg_ref[...], preferred_element_type=F32)
        u = jnp.dot(xb, wu_ref[...], preferred_element_type=F32)
        h = (g * _sigmoid(g) * u).astype(BF16)
        _pack_rows(jnp.dot(h, wd_ref[...], preferred_element_type=F32), ybuf.at[cur])

        @pl.when(i > 0)
        def _():
            wait_scatter(nxt, nv_ref[jnp.maximum(i - 1, 0)])

        start_scatter(cur, n_valid)

        @pl.when(n_next == 0)
        def _():
            wait_scatter(cur, n_valid)


def _routed_experts(x1p, tile_e, tile_rows, toks, slots, wg, wu, wd, *, tm):
    T, S, _ = x1p.shape
    D = wg.shape[1]
    F = wg.shape[2]
    n_tiles = slots.shape[0]
    grid_spec = pltpu.PrefetchScalarGridSpec(
        num_scalar_prefetch=2,
        grid=(n_tiles,),
        in_specs=[
            pl.BlockSpec(memory_space=pl.ANY),
            pl.BlockSpec(memory_space=pl.ANY),
            pl.BlockSpec(memory_space=pl.ANY),
            pl.BlockSpec((None, D, F), lambda i, te, nv: (te[i], 0, 0)),
            pl.BlockSpec((None, D, F), lambda i, te, nv: (te[i], 0, 0)),
            pl.BlockSpec((None, F, D), lambda i, te, nv: (te[i], 0, 0)),
        ],
        out_specs=pl.BlockSpec(memory_space=pl.ANY),
        scratch_shapes=[
            pltpu.SMEM((2, tm), jnp.int32),
            pltpu.SMEM((2, tm), jnp.int32),
            pltpu.VMEM((2, tm, S, LANES), jnp.uint32),
            pltpu.VMEM((2, tm, S, LANES), jnp.uint32),
            pltpu.SemaphoreType.DMA((2,)),
            pltpu.SemaphoreType.DMA((2,)),
            pltpu.SemaphoreType.DMA((2,)),
        ],
    )
    return pl.pallas_call(
        _experts_kernel,
        grid_spec=grid_spec,
        out_shape=jax.ShapeDtypeStruct((T * TOP_K, S, LANES), jnp.uint32),
        compiler_params=_params("arbitrary"),
        name="routed_experts",
    )(tile_e, tile_rows, toks, slots, x1p, wg, wu, wd)


def _combine_kernel(y_ref, w_ref, x1_ref, wsg_ref, wsu_ref, wsd_ref, g_ref, b_ref, out_ref, *, alpha):
    x1 = x1_ref[...]
    xb = x1.astype(BF16)
    gg = jnp.dot(xb, wsg_ref[...], preferred_element_type=F32)
    uu = jnp.dot(xb, wsu_ref[...], preferred_element_type=F32)
    hh = (gg * _sigmoid(gg) * uu).astype(BF16)
    shared = jnp.dot(hh, wsd_ref[...], preferred_element_type=F32)
    w = w_ref[...]
    routed = None
    for k in range(TOP_K):
        y_lo, y_hi = _unpack_rows(y_ref.at[k])
        term = w[:, k:k + 1] * jnp.concatenate([y_lo, y_hi], axis=1)
        routed = term if routed is None else routed + term
    out_ref[...] = _layer_norm(alpha * x1 + (routed + shared), g_ref[...], b_ref[...])


def _combine(y_kt, w_tk, x1, wsg, wsu, wsd, g, b, *, alpha, tm):
    T, D = x1.shape
    Fs = wsg.shape[1]
    kern = functools.partial(_combine_kernel, alpha=alpha)
    const = lambda i: (0, 0)
    return pl.pallas_call(
        kern,
        grid=(T // tm,),
        in_specs=[
            pl.BlockSpec((TOP_K, tm, D // (2 * LANES), LANES), lambda i: (0, i, 0, 0)),
            pl.BlockSpec((tm, TOP_K), lambda i: (i, 0)),
            pl.BlockSpec((tm, D), lambda i: (i, 0)),
            pl.BlockSpec((D, Fs), const),
            pl.BlockSpec((D, Fs), const),
            pl.BlockSpec((Fs, D), const),
            pl.BlockSpec((1, D), const),
            pl.BlockSpec((1, D), const),
        ],
        out_specs=pl.BlockSpec((tm, D), lambda i: (i, 0)),
        out_shape=jax.ShapeDtypeStruct((T, D), F32),
        compiler_params=_params("parallel"),
        name="shared_combine_ln",
    )(y_kt, w_tk, x1, wsg, wsu, wsd, g, b)


def _pick(n, pref):
    t = min(n, pref)
    while n % t:
        t //= 2
    return t


def _moe_layer(x_attn_in, o2d, w_out, ln1_g, ln1_b, w_router, router_bias, w_gate, w_up, w_down,
               ws_gate, ws_up, ws_down, ln2_g, ln2_b, *, alpha):
    T, D = x_attn_in.shape
    E = w_router.shape[1]
    x1, x1p, eidxT, wgtT = _proj_ln_router(
        o2d, w_out.astype(BF16), x_attn_in, ln1_g[None, :], ln1_b[None, :],
        w_router.T, router_bias[:, None], alpha=alpha, tm=_pick(T, 256))
    tm_e = _pick(T * TOP_K, 256)
    tile_e, tile_rows, toks, slots = _routing_plan(eidxT, n_experts=E, tm=tm_e)
    y = _routed_experts(x1p, tile_e, tile_rows, toks, slots, w_gate.astype(BF16), w_up.astype(BF16),
                        w_down.astype(BF16), tm=tm_e)
    return _combine(y.reshape(TOP_K, T, D // (2 * LANES), LANES), wgtT.T, x1, ws_gate.astype(BF16), ws_up.astype(BF16), ws_down.astype(BF16),
                    ln2_g[None, :], ln2_b[None, :], alpha=alpha, tm=_pick(T, 128))


def kernel(x, positions, ln1_g, ln1_b, ln2_g, ln2_b, da_w_in, da_w_out, da_lq1, da_lk1, da_lq2, da_lk2,
           da_subln_g, mb_w_in, mb_w_out, w_router, router_bias, w_gate, w_up, w_down, ws_gate, ws_up, ws_down):
    B, S, D = x.shape
    T = B * S
    depth = ln1_g.shape[0]
    alpha = (2 * depth) ** 0.25
    hd = HEAD_DIM
    cosf, sinf = _rope_tables(positions)
    xt = x.reshape(T, D)
    tm_proj = _pick(T, 1024)

    for i in range(depth):
        m = i // 2
        if i % 2 == 0:
            w_in = da_w_in[m]
            H = w_in.shape[1] // (6 * hd)
            qk_cols = 4 * H * hd
            tq = _pick(S, 256)
            qkv = _qkv_proj(xt, w_in.astype(BF16), cosf, sinf, rope_cols=qk_cols, q_cols=qk_cols // 2,
                            tm=tm_proj, tn=_pick(qk_cols // 2, 512))
            tk = _pick(S, 1024)
            vT = _value_tiles(qkv[:, qk_cols:], B=B, S=S, H=H, dv=2 * hd, tk=tk)
            lambda_init = 0.8 - 0.6 * math.exp(-0.3 * i)
            lam = (jnp.exp(jnp.sum(da_lq1[m].astype(F32) * da_lk1[m].astype(F32)))
                   - jnp.exp(jnp.sum(da_lq2[m].astype(F32) * da_lk2[m].astype(F32))) + lambda_init)
            o2d = _diff_attention(qkv, vT, lam.reshape(1), da_subln_g[m][None, :], B=B, S=S, H=H, tq=tq,
                                  tk=tk, lambda_init=lambda_init)
            w_out = da_w_out[m]
        else:
            w_in = mb_w_in[m]
            H = w_in.shape[1] // (3 * hd)
            qk_cols = 2 * H * hd
            qkv = _qkv_proj(xt, w_in.astype(BF16), cosf, sinf, rope_cols=qk_cols, q_cols=qk_cols // 2,
                            tm=tm_proj, tn=_pick(qk_cols // 2, 512))
            vT = _value_tiles(qkv[:, qk_cols:], B=B, S=S, H=H, dv=hd, tk=MB_BLOCK)
            o2d = _moba_attention(qkv, vT, B=B, S=S, H=H, group=_pick(S // MB_BLOCK, 4))
            w_out = mb_w_out[m]
        xt = _moe_layer(xt, o2d, w_out, ln1_g[i], ln1_b[i], w_router[i], router_bias[i], w_gate[i], w_up[i],
                        w_down[i], ws_gate[i], ws_up[i], ws_down[i], ln2_g[i], ln2_b[i], alpha=alpha)
    return xt.reshape(B, S, D)
```

```python
import functools
import math

import jax
import jax.numpy as jnp
from jax import lax
from jax.experimental import pallas as pl
from jax.experimental.pallas import tpu as pltpu

F32 = jnp.float32
BF16 = jnp.bfloat16

HEAD_DIM = 128
ROPE_THETA = 500000.0
ROPE_DIMS = HEAD_DIM // 4
ROPE_HALF = ROPE_DIMS // 2
LN_EPS = 1e-5
MB_BLOCK = 256
MB_TOPK = 3
TOP_K = 8
N_GROUPS = 8
TOPK_GROUPS = 4
ROUTED_SCALE = 2.5
MASK_VALUE = -1e30

LANES = 128
SUBLANES = 8
ISSUE_UNROLL = 8
VMEM_LIMIT_BYTES = 56 * 1024 * 1024

_NT = (((1,), (1,)), ((), ()))


def _params(*sem):
    return pltpu.CompilerParams(dimension_semantics=sem, vmem_limit_bytes=VMEM_LIMIT_BYTES)


def _sigmoid(x):
    return 1.0 / (1.0 + jnp.exp(-x))


def _pack_rows(x, o_ref):
    tm, d = x.shape
    half = d // 2
    n_sub = half // LANES
    lo = pltpu.bitcast(x[:, :half].astype(BF16).astype(F32), jnp.uint32) >> 16
    hi = pltpu.bitcast(x[:, half:].astype(BF16).astype(F32), jnp.uint32) & jnp.uint32(0xFFFF0000)
    packed = hi | lo
    for s in range(n_sub):
        o_ref[pl.ds(s, tm, stride=n_sub), :] = packed[:, s * LANES:(s + 1) * LANES]


def _unpack_rows(p_ref, n_sub):
    tm = p_ref.shape[0] // n_sub
    p = jnp.concatenate([p_ref[pl.ds(s, tm, stride=n_sub), :] for s in range(n_sub)], axis=1)
    return pltpu.bitcast(p << 16, F32), pltpu.bitcast(p & jnp.uint32(0xFFFF0000), F32)


def _layer_norm(y, g, b):
    mu = jnp.mean(y, axis=-1, keepdims=True)
    d = y - mu
    var = jnp.mean(d * d, axis=-1, keepdims=True)
    return d * lax.rsqrt(var + LN_EPS) * g + b


def _qkv_kernel(x_ref, w_ref, cos_ref, sin_ref, o_ref, xb_ref, *, rope_blocks, q_blocks, q_scale):
    j = pl.program_id(1)

    @pl.when(j == 0)
    def _():
        xb_ref[...] = x_ref[...].astype(BF16)

    acc = jnp.dot(xb_ref[...], w_ref[...], preferred_element_type=F32)
    tn = acc.shape[1]

    @pl.when(j < rope_blocks)
    def _():
        cosf = cos_ref[...]
        sinf = sin_ref[...]
        lane = lax.broadcasted_iota(jnp.int32, (1, LANES), 1)
        scale = jnp.where(j < q_blocks, q_scale, 1.0).astype(F32)
        for c in range(tn // LANES):
            xc = acc[:, c * LANES:(c + 1) * LANES]
            partner = jnp.where(lane < ROPE_HALF,
                                pltpu.roll(xc, LANES - ROPE_HALF, 1),
                                pltpu.roll(xc, ROPE_HALF, 1))
            r = (xc * cosf + partner * sinf) * scale
            o_ref[:, c * LANES:(c + 1) * LANES] = r.astype(o_ref.dtype)

    @pl.when(j >= rope_blocks)
    def _():
        o_ref[...] = acc.astype(o_ref.dtype)


def _qkv_proj(x2d, w_bf16, cosf, sinf, *, rope_cols, q_cols, tm, tn):
    T, D = x2d.shape
    N = w_bf16.shape[1]
    kern = functools.partial(_qkv_kernel, rope_blocks=rope_cols // tn, q_blocks=q_cols // tn,
                             q_scale=HEAD_DIM ** -0.5)
    return pl.pallas_call(
        kern,
        grid=(T // tm, N // tn),
        in_specs=[
            pl.BlockSpec((tm, D), lambda i, j: (i, 0)),
            pl.BlockSpec((D, tn), lambda i, j: (0, j)),
            pl.BlockSpec((tm, LANES), lambda i, j: (i, 0)),
            pl.BlockSpec((tm, LANES), lambda i, j: (i, 0)),
        ],
        out_specs=pl.BlockSpec((tm, tn), lambda i, j: (i, j)),
        out_shape=jax.ShapeDtypeStruct((T, N), BF16),
        scratch_shapes=[pltpu.VMEM((tm, D), BF16)],
        compiler_params=_params("parallel", "arbitrary"),
        name="qkv_proj",
    )(x2d, w_bf16, cosf, sinf)


def _rope_tables(positions):
    inv_freq = ROPE_THETA ** (-jnp.arange(0, ROPE_DIMS, 2, dtype=F32) / ROPE_DIMS)
    ang = positions.astype(F32).reshape(-1)[:, None] * inv_freq
    cos, sin = jnp.cos(ang), jnp.sin(ang)
    T = ang.shape[0]
    rest = LANES - ROPE_DIMS
    cosf = jnp.concatenate([cos, cos, jnp.ones((T, rest), F32)], axis=1)
    sinf = jnp.concatenate([-sin, sin, jnp.zeros((T, rest), F32)], axis=1)
    return cosf, sinf


def _softmax_first(sT, vT):
    m = jnp.max(sT, axis=0, keepdims=True)
    p = jnp.exp(sT - m)
    l = jnp.sum(p, axis=0, keepdims=True)
    acc = jnp.dot(vT, p.astype(BF16), preferred_element_type=F32)
    return m, l, acc


def _softmax_update(sTs, vTs, m_ref, l_ref, acc_ref):
    m_prev = m_ref[...]
    m_new = m_prev
    for sT in sTs:
        m_new = jnp.maximum(m_new, jnp.max(sT, axis=0, keepdims=True))
    alpha = jnp.exp(m_prev - m_new)
    l = alpha * l_ref[...]
    pv = None
    for sT, vT in zip(sTs, vTs):
        p = jnp.exp(sT - m_new)
        l = l + jnp.sum(p, axis=0, keepdims=True)
        d = jnp.dot(vT, p.astype(BF16), preferred_element_type=F32)
        pv = d if pv is None else pv + d
    l_ref[...] = l
    acc_ref[...] = alpha * acc_ref[...] + pv
    m_ref[...] = m_new


def _causal_mask(sT, key0, qry0):
    tk, tq = sT.shape
    key = key0 + lax.broadcasted_iota(jnp.int32, (tk, tq), 0)
    qry = qry0 + lax.broadcasted_iota(jnp.int32, (tk, tq), 1)
    return jnp.where(key <= qry, sT, MASK_VALUE)


def _da_kernel(lam_ref, q_ref, k_ref, vT_ref, g_ref, o_ref, m_ref, l_ref, acc_ref, *, tq, tk, out_scale):
    qi = pl.program_id(2)
    hd = HEAD_DIM
    gd = (qi * tq) // tk

    def scores(g, c):
        koff = pl.multiple_of(g * tk, tk)
        kblk = k_ref[pl.ds(koff, tk), c * hd:(c + 1) * hd]
        return lax.dot_general(kblk, q_ref[:, c * hd:(c + 1) * hd], _NT, preferred_element_type=F32)

    vT_diag = vT_ref[gd]
    for c in range(2):
        m, l, acc = _softmax_first(_causal_mask(scores(gd, c), gd * tk, qi * tq), vT_diag)
        m_ref[c] = m
        l_ref[c] = l
        acc_ref[c] = acc

    def body(g, carry):
        vT = vT_ref[g]
        for c in range(2):
            _softmax_update([scores(g, c)], [vT], m_ref.at[c], l_ref.at[c], acc_ref.at[c])
        return carry

    lax.fori_loop(0, gd, body, 0)

    lam = lam_ref[0]
    oT = acc_ref[0] / l_ref[0] - lam * (acc_ref[1] / l_ref[1])
    o = oT.T
    o = o * lax.rsqrt(jnp.mean(o * o, axis=-1, keepdims=True) + LN_EPS) * g_ref[...]
    o_ref[...] = (o * out_scale).astype(o_ref.dtype)


def _diff_attention(qkv, vT, lam, subln_g, *, B, S, H, tq, tk, lambda_init):
    T = B * S
    nq = S // tq
    dv = 2 * HEAD_DIM
    kern = functools.partial(_da_kernel, tq=tq, tk=tk, out_scale=1.0 - lambda_init)
    return pl.pallas_call(
        kern,
        grid=(B, H, nq),
        in_specs=[
            pl.BlockSpec(memory_space=pltpu.SMEM),
            pl.BlockSpec((tq, dv), lambda b, h, i: (b * nq + i, h)),
            pl.BlockSpec((S, dv), lambda b, h, i: (b, H + h)),
            pl.BlockSpec((None, None, S // tk, dv, tk), lambda b, h, i: (b, h, 0, 0, 0)),
            pl.BlockSpec((1, dv), lambda b, h, i: (0, 0)),
        ],
        out_specs=pl.BlockSpec((tq, dv), lambda b, h, i: (b * nq + i, h)),
        out_shape=jax.ShapeDtypeStruct((T, H * dv), BF16),
        scratch_shapes=[
            pltpu.VMEM((2, 1, tq), F32),
            pltpu.VMEM((2, 1, tq), F32),
            pltpu.VMEM((2, dv, tq), F32),
        ],
        compiler_params=_params("parallel", "parallel", "arbitrary"),
        name="diff_attention",
    )(lam, qkv, qkv, vT, subln_g)


def _mb_kernel(q_ref, k_ref, vT_ref, o_ref, kmean_ref, bias_ref, m_ref, l_ref, acc_ref, *, nb, group):
    j = pl.program_id(2)
    blk = MB_BLOCK

    @pl.when(j == 0)
    def _():
        for n in range(nb):
            kb = k_ref[n * blk:(n + 1) * blk, :].astype(F32)
            kmean_ref[n:n + 1, :] = jnp.sum(kb, axis=0, keepdims=True) * (1.0 / blk)

    q = q_ref[...]
    gate = lax.dot_general(kmean_ref[...], q.astype(F32), _NT, precision=lax.Precision.HIGHEST,
                           preferred_element_type=F32)
    bid = lax.broadcasted_iota(jnp.int32, gate.shape, 0)
    rem = jnp.where(bid < j, gate, -jnp.inf)
    sel = jnp.zeros(gate.shape, jnp.bool_)
    for _ in range(MB_TOPK):
        mx = jnp.max(rem, axis=0, keepdims=True)
        first = jnp.min(jnp.where(rem == mx, bid, nb), axis=0, keepdims=True)
        pick = (bid == first) & (mx > -jnp.inf)
        sel = sel | pick
        rem = jnp.where(bid == first, -jnp.inf, rem)
    bias = jnp.where(sel, 0.0, MASK_VALUE).astype(F32)
    for gg in range(nb // group):
        bias_ref[gg] = bias[gg * group:(gg + 1) * group, :]

    def scores(n):
        koff = pl.multiple_of(n * blk, blk)
        return lax.dot_general(k_ref[pl.ds(koff, blk), :], q, _NT, preferred_element_type=F32)

    m, l, acc = _softmax_first(_causal_mask(scores(j), 0, 0), vT_ref[j])
    m_ref[...] = m
    l_ref[...] = l
    acc_ref[...] = acc

    def body(g, carry):
        gbias = bias_ref[g]
        sTs = [scores(g * group + t) + gbias[t:t + 1, :] for t in range(group)]
        vTs = [vT_ref[g * group + t] for t in range(group)]
        _softmax_update(sTs, vTs, m_ref, l_ref, acc_ref)
        return carry

    lax.fori_loop(0, (j + group - 1) // group, body, 0)

    oT = acc_ref[...] / l_ref[...]
    o_ref[...] = oT.T.astype(o_ref.dtype)


def _moba_attention(qkv, vT, *, B, S, H, group):
    T = B * S
    blk = MB_BLOCK
    nb = S // blk
    hd = HEAD_DIM
    kern = functools.partial(_mb_kernel, nb=nb, group=group)
    return pl.pallas_call(
        kern,
        grid=(B, H, nb),
        in_specs=[
            pl.BlockSpec((blk, hd), lambda b, h, i: (b * nb + i, h)),
            pl.BlockSpec((S, hd), lambda b, h, i: (b, H + h)),
            pl.BlockSpec((None, None, nb, hd, blk), lambda b, h, i: (b, h, 0, 0, 0)),
        ],
        out_specs=pl.BlockSpec((blk, hd), lambda b, h, i: (b * nb + i, h)),
        out_shape=jax.ShapeDtypeStruct((T, H * hd), BF16),
        scratch_shapes=[
            pltpu.VMEM((nb, hd), F32),
            pltpu.VMEM((nb // group, group, blk), F32),
            pltpu.VMEM((1, blk), F32),
            pltpu.VMEM((1, blk), F32),
            pltpu.VMEM((hd, blk), F32),
        ],
        compiler_params=_params("parallel", "parallel", "arbitrary"),
        name="moba_attention",
    )(qkv, qkv, vT)


def _value_tiles(v2d, *, B, S, H, dv, tk):
    v = v2d.reshape(B, S // tk, tk, H, dv)
    return v.transpose(0, 3, 1, 4, 2)


def _router(x1, wrT_ref, rb_ref, eidx_ref, wgt_ref):
    E = wrT_ref.shape[0]
    per_group = E // N_GROUPS
    tm = x1.shape[0]
    logits = lax.dot_general(wrT_ref[...], x1, _NT, precision=lax.Precision.HIGHEST,
                             preferred_element_type=F32)
    scores = _sigmoid(logits)
    biased = scores + rb_ref[...]

    gid8 = lax.broadcasted_iota(jnp.int32, (per_group, tm), 0)
    gscores = []
    for g in range(N_GROUPS):
        bg = biased[g * per_group:(g + 1) * per_group, :]
        m1 = jnp.max(bg, axis=0, keepdims=True)
        i1 = jnp.min(jnp.where(bg == m1, gid8, per_group), axis=0, keepdims=True)
        m2 = jnp.max(jnp.where(gid8 == i1, -jnp.inf, bg), axis=0, keepdims=True)
        gscores.append(m1 + m2)
    gs = jnp.concatenate(gscores, axis=0)

    gidx = lax.broadcasted_iota(jnp.int32, (N_GROUPS, tm), 0)
    rank = jnp.zeros((N_GROUPS, tm), jnp.int32)
    for g in range(N_GROUPS):
        row = gs[g:g + 1, :]
        beats = (row > gs) | ((row == gs) & (g < gidx))
        rank = rank + beats.astype(jnp.int32)
    gsel = rank < TOPK_GROUPS

    masked = jnp.concatenate(
        [jnp.where(gsel[g:g + 1, :], biased[g * per_group:(g + 1) * per_group, :], -jnp.inf)
         for g in range(N_GROUPS)], axis=0)

    eid = lax.broadcasted_iota(jnp.int32, (E, tm), 0)
    rem = masked
    ids, vals = [], []
    for _ in range(TOP_K):
        mx = jnp.max(rem, axis=0, keepdims=True)
        first = jnp.min(jnp.where(rem == mx, eid, E), axis=0, keepdims=True)
        onehot = eid == first
        ids.append(first)
        vals.append(jnp.sum(jnp.where(onehot, scores, 0.0), axis=0, keepdims=True))
        rem = jnp.where(onehot, -jnp.inf, rem)
    w = jnp.concatenate(vals, axis=0)
    w = w / jnp.sum(w, axis=0, keepdims=True) * ROUTED_SCALE
    eidx_ref[...] = jnp.concatenate(ids, axis=0)
    wgt_ref[...] = w


def _proj_ln_router_kernel(o_ref, w_ref, x_ref, g_ref, b_ref, wrT_ref, rb_ref,
                           x1_ref, x1p_ref, eidx_ref, wgt_ref, *, alpha):
    h = jnp.dot(o_ref[...], w_ref[...], preferred_element_type=F32)
    x1 = _layer_norm(alpha * x_ref[...] + h, g_ref[...], b_ref[...])
    x1_ref[...] = x1
    _pack_rows(x1, x1p_ref)
    _router(x1, wrT_ref, rb_ref, eidx_ref, wgt_ref)


def _proj_ln_router(o2d, w_out_bf16, x2d, g, b, wrT, rb, *, alpha, tm):
    T, D = x2d.shape
    Do = o2d.shape[1]
    E = wrT.shape[0]
    kern = functools.partial(_proj_ln_router_kernel, alpha=alpha)
    const = lambda i: (0, 0)
    return pl.pallas_call(
        kern,
        grid=(T // tm,),
        in_specs=[
            pl.BlockSpec((tm, Do), lambda i: (i, 0)),
            pl.BlockSpec((Do, D), const),
            pl.BlockSpec((tm, D), lambda i: (i, 0)),
            pl.BlockSpec((1, D), const),
            pl.BlockSpec((1, D), const),
            pl.BlockSpec((E, D), const),
            pl.BlockSpec((E, 1), const),
        ],
        out_specs=[
            pl.BlockSpec((tm, D), lambda i: (i, 0)),
            pl.BlockSpec((tm * (D // (2 * LANES)), LANES), lambda i: (i, 0)),
            pl.BlockSpec((TOP_K, tm), lambda i: (0, i)),
            pl.BlockSpec((TOP_K, tm), lambda i: (0, i)),
        ],
        out_shape=[
            jax.ShapeDtypeStruct((T, D), F32),
            jax.ShapeDtypeStruct((T * (D // (2 * LANES)), LANES), jnp.uint32),
            jax.ShapeDtypeStruct((TOP_K, T), jnp.int32),
            jax.ShapeDtypeStruct((TOP_K, T), F32),
        ],
        compiler_params=_params("parallel"),
        name="outproj_ln_router",
    )(o2d, w_out_bf16, x2d, g, b, wrT, rb)


def _routing_plan(eidxT, *, n_experts, tm):
    K, T = eidxT.shape
    n_slots = T * K
    n_tiles = n_slots // tm + n_experts
    e_flat = eidxT.reshape(-1)
    order = jnp.argsort(e_flat, stable=True).astype(jnp.int32)
    counts = jnp.sum((e_flat[:, None] == jnp.arange(n_experts, dtype=jnp.int32)[None, :]).astype(jnp.int32), axis=0)
    tiles_per_e = (counts + tm - 1) // tm
    tile_end = jnp.cumsum(tiles_per_e)
    tile_start = tile_end - tiles_per_e
    run_start = jnp.cumsum(counts) - counts
    total_tiles = tile_end[-1]

    tile_ids = jnp.arange(n_tiles, dtype=jnp.int32)
    tile_valid = tile_ids < total_tiles
    last_tile = jnp.maximum(total_tiles - 1, 0)
    tile_e = jnp.sum((jnp.minimum(tile_ids, last_tile)[:, None] >= tile_end[None, :]).astype(jnp.int32), axis=1)
    tile_e = jnp.minimum(tile_e, n_experts - 1)

    tile_off = (tile_ids - tile_start[tile_e]) * tm
    n_valid = jnp.where(tile_valid, jnp.clip(counts[tile_e] - tile_off, 0, tm), 0).astype(jnp.int32)
    r = jnp.arange(tm, dtype=jnp.int32)[None, :]
    src = jnp.clip((run_start[tile_e] + tile_off)[:, None] + r, 0, n_slots - 1)
    slots = jnp.where(r < n_valid[:, None], order[src], 0).astype(jnp.int32)
    return tile_e, n_valid, lax.rem(slots, T), slots


def _experts_kernel(te_ref, nv_ref, tok_hbm, slot_hbm, x_hbm, wg_ref, wu_ref, wd_ref, y_hbm,
                    tok_a, tok_b, slot_a, slot_b, xbuf, ybuf, tok_sems, slot_sems, gather_sems, scatter_sems,
                    *, n_sub):
    i = pl.program_id(0)
    last = pl.num_programs(0) - 1
    tm = xbuf.shape[1] // n_sub
    toks = (tok_a, tok_b)
    slots = (slot_a, slot_b)
    n_valid = nv_ref[i]

    def tok_copy(tile, b):
        return pltpu.make_async_copy(tok_hbm.at[jnp.minimum(tile, last)], toks[b], tok_sems.at[b])

    def slot_copy(tile, b):
        return pltpu.make_async_copy(slot_hbm.at[jnp.minimum(tile, last)], slots[b], slot_sems.at[b])

    def start_gather(b):
        for r in range(tm):
            src = pl.multiple_of(toks[b][r], n_sub)
            pltpu.make_async_copy(x_hbm.at[pl.ds(src, n_sub), :], xbuf.at[b, pl.ds(r * n_sub, n_sub), :],
                                  gather_sems.at[b]).start()

    def wait_gather(b):
        pltpu.make_async_copy(x_hbm.at[pl.ds(0, tm * n_sub), :], xbuf.at[b], gather_sems.at[b]).wait()

    def start_scatter(b, n):
        def row(r):
            src = pl.multiple_of(r * n_sub, n_sub)
            dst = pl.multiple_of(slots[b][r], n_sub)
            pltpu.make_async_copy(ybuf.at[b, pl.ds(src, n_sub), :], y_hbm.at[pl.ds(dst, n_sub), :],
                                  scatter_sems.at[b]).start()

        def group(g, carry):
            for u in range(ISSUE_UNROLL):
                row(g * ISSUE_UNROLL + u)
            return carry

        def single(r, carry):
            row(r)
            return carry

        n_groups = n // ISSUE_UNROLL
        lax.fori_loop(0, n_groups, group, 0)
        lax.fori_loop(n_groups * ISSUE_UNROLL, n, single, 0)

    def wait_scatter(b, n):
        @pl.when(n > 0)
        def _():
            rows = pl.multiple_of(n * n_sub, n_sub)
            pltpu.make_async_copy(ybuf.at[b, pl.ds(0, rows), :], y_hbm.at[pl.ds(0, rows), :],
                                  scatter_sems.at[b]).wait()

    def step(cur):
        nxt = 1 - cur

        @pl.when(i == 0)
        def _():
            first = tok_copy(0, cur)
            first.start()
            first.wait()
            start_gather(cur)
            tok_copy(1, nxt).start()
            slot_copy(0, cur).start()

        tok_copy(i + 2, cur).start()
        slot_copy(i + 1, nxt).start()
        wait_gather(cur)
        tok_copy(i + 1, nxt).wait()
        start_gather(nxt)

        x_lo, x_hi = _unpack_rows(xbuf.at[cur], n_sub)
        xb = jnp.concatenate([x_lo, x_hi], axis=1).astype(BF16)
        g = jnp.dot(xb, wg_ref[...], preferred_element_type=F32)
        u = jnp.dot(xb, wu_ref[...], preferred_element_type=F32)
        h = (g * _sigmoid(g) * u).astype(BF16)
        _pack_rows(jnp.dot(h, wd_ref[...], preferred_element_type=F32), ybuf.at[cur])

        @pl.when(i > 0)
        def _():
            wait_scatter(nxt, nv_ref[jnp.maximum(i - 1, 0)])

        slot_copy(i, cur).wait()
        start_scatter(cur, n_valid)

        @pl.when(i == last)
        def _():
            wait_scatter(cur, n_valid)
            wait_gather(nxt)
            tok_copy(i + 2, cur).wait()
            slot_copy(i + 1, nxt).wait()

    parity = lax.rem(i, 2)
    for b in range(2):
        pl.when(parity == b)(functools.partial(step, b))


def _routed_experts(x1p, tile_e, tile_rows, toks, slots, wg, wu, wd, *, tm):
    D = wg.shape[1]
    F = wg.shape[2]
    S = D // (2 * LANES)
    T = x1p.shape[0] // S
    n_tiles = slots.shape[0]
    grid_spec = pltpu.PrefetchScalarGridSpec(
        num_scalar_prefetch=2,
        grid=(n_tiles,),
        in_specs=[
            pl.BlockSpec(memory_space=pl.ANY),
            pl.BlockSpec(memory_space=pl.ANY),
            pl.BlockSpec(memory_space=pl.ANY),
            pl.BlockSpec((None, D, F), lambda i, te, nv: (te[i], 0, 0)),
            pl.BlockSpec((None, D, F), lambda i, te, nv: (te[i], 0, 0)),
            pl.BlockSpec((None, F, D), lambda i, te, nv: (te[i], 0, 0)),
        ],
        out_specs=pl.BlockSpec(memory_space=pl.ANY),
        scratch_shapes=[
            pltpu.SMEM((tm,), jnp.int32),
            pltpu.SMEM((tm,), jnp.int32),
            pltpu.SMEM((tm,), jnp.int32),
            pltpu.SMEM((tm,), jnp.int32),
            pltpu.VMEM((2, tm * S, LANES), jnp.uint32),
            pltpu.VMEM((2, tm * S, LANES), jnp.uint32),
            pltpu.SemaphoreType.DMA((2,)),
            pltpu.SemaphoreType.DMA((2,)),
            pltpu.SemaphoreType.DMA((2,)),
            pltpu.SemaphoreType.DMA((2,)),
        ],
    )
    return pl.pallas_call(
        functools.partial(_experts_kernel, n_sub=S),
        grid_spec=grid_spec,
        out_shape=jax.ShapeDtypeStruct((T * TOP_K * S, LANES), jnp.uint32),
        compiler_params=_params("arbitrary"),
        name="routed_experts",
    )(tile_e, tile_rows, toks * S, slots * S, x1p, wg, wu, wd)


def _combine_kernel(y_ref, w_ref, x1_ref, wsg_ref, wsu_ref, wsd_ref, g_ref, b_ref, out_ref, *, alpha):
    x1 = x1_ref[...]
    xb = x1.astype(BF16)
    gg = jnp.dot(xb, wsg_ref[...], preferred_element_type=F32)
    uu = jnp.dot(xb, wsu_ref[...], preferred_element_type=F32)
    hh = (gg * _sigmoid(gg) * uu).astype(BF16)
    shared = jnp.dot(hh, wsd_ref[...], preferred_element_type=F32)
    w = w_ref[...]
    routed = None
    for k in range(TOP_K):
        y_lo, y_hi = _unpack_rows(y_ref.at[k], y_ref.shape[1] // x1.shape[0])
        term = w[:, k:k + 1] * jnp.concatenate([y_lo, y_hi], axis=1)
        routed = term if routed is None else routed + term
    out_ref[...] = _layer_norm(alpha * x1 + (routed + shared), g_ref[...], b_ref[...])


def _combine(y_kt, w_tk, x1, wsg, wsu, wsd, g, b, *, alpha, tm):
    T, D = x1.shape
    Fs = wsg.shape[1]
    kern = functools.partial(_combine_kernel, alpha=alpha)
    const = lambda i: (0, 0)
    return pl.pallas_call(
        kern,
        grid=(T // tm,),
        in_specs=[
            pl.BlockSpec((TOP_K, tm * (D // (2 * LANES)), LANES), lambda i: (0, i, 0)),
            pl.BlockSpec((tm, TOP_K), lambda i: (i, 0)),
            pl.BlockSpec((tm, D), lambda i: (i, 0)),
            pl.BlockSpec((D, Fs), const),
            pl.BlockSpec((D, Fs), const),
            pl.BlockSpec((Fs, D), const),
            pl.BlockSpec((1, D), const),
            pl.BlockSpec((1, D), const),
        ],
        out_specs=pl.BlockSpec((tm, D), lambda i: (i, 0)),
        out_shape=jax.ShapeDtypeStruct((T, D), F32),
        compiler_params=_params("parallel"),
        name="shared_combine_ln",
    )(y_kt, w_tk, x1, wsg, wsu, wsd, g, b)


def _pick(n, pref):
    t = min(n, pref)
    while n % t:
        t //= 2
    return t


def _moe_layer(x_attn_in, o2d, w_out, ln1_g, ln1_b, w_router, router_bias, w_gate, w_up, w_down,
               ws_gate, ws_up, ws_down, ln2_g, ln2_b, *, alpha):
    T, D = x_attn_in.shape
    E = w_router.shape[1]
    x1, x1p, eidxT, wgtT = _proj_ln_router(
        o2d, w_out.astype(BF16), x_attn_in, ln1_g[None, :], ln1_b[None, :],
        w_router.T, router_bias[:, None], alpha=alpha, tm=_pick(T, 256))
    tm_e = _pick(T * TOP_K, 256)
    tile_e, tile_rows, toks, slots = _routing_plan(eidxT, n_experts=E, tm=tm_e)
    y = _routed_experts(x1p, tile_e, tile_rows, toks, slots, w_gate.astype(BF16), w_up.astype(BF16),
                        w_down.astype(BF16), tm=tm_e)
    return _combine(y.reshape(TOP_K, T * (D // (2 * LANES)), LANES), wgtT.T, x1, ws_gate.astype(BF16), ws_up.astype(BF16), ws_down.astype(BF16),
                    ln2_g[None, :], ln2_b[None, :], alpha=alpha, tm=_pick(T, 128))


def kernel(x, positions, ln1_g, ln1_b, ln2_g, ln2_b, da_w_in, da_w_out, da_lq1, da_lk1, da_lq2, da_lk2,
           da_subln_g, mb_w_in, mb_w_out, w_router, router_bias, w_gate, w_up, w_down, ws_gate, ws_up, ws_down):
    B, S, D = x.shape
    T = B * S
    depth = ln1_g.shape[0]
    alpha = (2 * depth) ** 0.25
    hd = HEAD_DIM
    cosf, sinf = _rope_tables(positions)
    xt = x.reshape(T, D)
    tm_proj = _pick(T, 1024)

    for i in range(depth):
        m = i // 2
        if i % 2 == 0:
            w_in = da_w_in[m]
            H = w_in.shape[1] // (6 * hd)
            qk_cols = 4 * H * hd
            tq = _pick(S, 256)
            qkv = _qkv_proj(xt, w_in.astype(BF16), cosf, sinf, rope_cols=qk_cols, q_cols=qk_cols // 2,
                            tm=tm_proj, tn=_pick(qk_cols // 2, 512))
            tk = _pick(S, 1024)
            vT = _value_tiles(qkv[:, qk_cols:], B=B, S=S, H=H, dv=2 * hd, tk=tk)
            lambda_init = 0.8 - 0.6 * math.exp(-0.3 * i)
            lam = (jnp.exp(jnp.sum(da_lq1[m].astype(F32) * da_lk1[m].astype(F32)))
                   - jnp.exp(jnp.sum(da_lq2[m].astype(F32) * da_lk2[m].astype(F32))) + lambda_init)
            o2d = _diff_attention(qkv, vT, lam.reshape(1), da_subln_g[m][None, :], B=B, S=S, H=H, tq=tq,
                                  tk=tk, lambda_init=lambda_init)
            w_out = da_w_out[m]
        else:
            w_in = mb_w_in[m]
            H = w_in.shape[1] // (3 * hd)
            qk_cols = 2 * H * hd
            qkv = _qkv_proj(xt, w_in.astype(BF16), cosf, sinf, rope_cols=qk_cols, q_cols=qk_cols // 2,
                            tm=tm_proj, tn=_pick(qk_cols // 2, 512))
            vT = _value_tiles(qkv[:, qk_cols:], B=B, S=S, H=H, dv=hd, tk=MB_BLOCK)
            o2d = _moba_attention(qkv, vT, B=B, S=S, H=H, group=_pick(S // MB_BLOCK, 4))
            w_out = mb_w_out[m]
        xt = _moe_layer(xt, o2d, w_out, ln1_g[i], ln1_b[i], w_router[i], router_bias[i], w_gate[i], w_up[i],
                        w_down[i], ws_gate[i], ws_up[i], ws_down[i], ln2_g[i], ln2_b[i], alpha=alpha)
    return xt.reshape(B, S, D)
```

```python
import functools
import math

import jax
import jax.numpy as jnp
from jax import lax
from jax.experimental import pallas as pl
from jax.experimental.pallas import tpu as pltpu

F32 = jnp.float32
BF16 = jnp.bfloat16

HEAD_DIM = 128
ROPE_THETA = 500000.0
ROPE_DIMS = HEAD_DIM // 4
ROPE_HALF = ROPE_DIMS // 2
LN_EPS = 1e-5
MB_BLOCK = 256
MB_TOPK = 3
MB_GROUP = 4
TOP_K = 8
N_GROUPS = 8
TOPK_GROUPS = 4
ROUTED_SCALE = 2.5
MASK_VALUE = -1e30

LANES = 128
SUBLANES = 8
ISSUE_UNROLL = 8
VMEM_LIMIT_BYTES = 56 * 1024 * 1024

_NT = (((1,), (1,)), ((), ()))


def _params(*sem):
    return pltpu.CompilerParams(dimension_semantics=sem, vmem_limit_bytes=VMEM_LIMIT_BYTES)


def _sigmoid(x):
    return 1.0 / (1.0 + jnp.exp(-x))


def _pack_rows(x, o_ref):
    tm, d = x.shape
    half = d // 2
    n_sub = half // LANES
    lo = pltpu.bitcast(x[:, :half].astype(BF16).astype(F32), jnp.uint32) >> 16
    hi = pltpu.bitcast(x[:, half:].astype(BF16).astype(F32), jnp.uint32) & jnp.uint32(0xFFFF0000)
    packed = hi | lo
    for s in range(n_sub):
        o_ref[pl.ds(s, tm, stride=n_sub), :] = packed[:, s * LANES:(s + 1) * LANES]


def _unpack_rows(p_ref, n_sub):
    tm = p_ref.shape[0] // n_sub
    p = jnp.concatenate([p_ref[pl.ds(s, tm, stride=n_sub), :] for s in range(n_sub)], axis=1)
    return pltpu.bitcast(p << 16, F32), pltpu.bitcast(p & jnp.uint32(0xFFFF0000), F32)


def _layer_norm(y, g, b):
    mu = jnp.mean(y, axis=-1, keepdims=True)
    d = y - mu
    var = jnp.mean(d * d, axis=-1, keepdims=True)
    return d * lax.rsqrt(var + LN_EPS) * g + b


def _qkv_kernel(x_ref, w_ref, cos_ref, sin_ref, o_ref, xb_ref, *, rope_blocks, q_blocks, q_scale):
    j = pl.program_id(1)

    @pl.when(j == 0)
    def _():
        xb_ref[...] = x_ref[...].astype(BF16)

    acc = jnp.dot(xb_ref[...], w_ref[...], preferred_element_type=F32)
    tn = acc.shape[1]

    @pl.when(j < rope_blocks)
    def _():
        cosf = cos_ref[...]
        sinf = sin_ref[...]
        lane = lax.broadcasted_iota(jnp.int32, (1, LANES), 1)
        scale = jnp.where(j < q_blocks, q_scale, 1.0).astype(F32)
        for c in range(tn // LANES):
            xc = acc[:, c * LANES:(c + 1) * LANES]
            partner = jnp.where(lane < ROPE_HALF,
                                pltpu.roll(xc, LANES - ROPE_HALF, 1),
                                pltpu.roll(xc, ROPE_HALF, 1))
            r = (xc * cosf + partner * sinf) * scale
            o_ref[:, c * LANES:(c + 1) * LANES] = r.astype(o_ref.dtype)

    @pl.when(j >= rope_blocks)
    def _():
        o_ref[...] = acc.astype(o_ref.dtype)


def _qkv_proj(x2d, w_bf16, cosf, sinf, *, rope_cols, q_cols, tm, tn):
    T, D = x2d.shape
    N = w_bf16.shape[1]
    kern = functools.partial(_qkv_kernel, rope_blocks=rope_cols // tn, q_blocks=q_cols // tn,
                             q_scale=HEAD_DIM ** -0.5)
    return pl.pallas_call(
        kern,
        grid=(T // tm, N // tn),
        in_specs=[
            pl.BlockSpec((tm, D), lambda i, j: (i, 0)),
            pl.BlockSpec((D, tn), lambda i, j: (0, j)),
            pl.BlockSpec((tm, LANES), lambda i, j: (i, 0)),
            pl.BlockSpec((tm, LANES), lambda i, j: (i, 0)),
        ],
        out_specs=pl.BlockSpec((tm, tn), lambda i, j: (i, j)),
        out_shape=jax.ShapeDtypeStruct((T, N), BF16),
        scratch_shapes=[pltpu.VMEM((tm, D), BF16)],
        compiler_params=_params("parallel", "arbitrary"),
        name="qkv_proj",
    )(x2d, w_bf16, cosf, sinf)


def _rope_tables(positions):
    inv_freq = ROPE_THETA ** (-jnp.arange(0, ROPE_DIMS, 2, dtype=F32) / ROPE_DIMS)
    ang = positions.astype(F32).reshape(-1)[:, None] * inv_freq
    cos, sin = jnp.cos(ang), jnp.sin(ang)
    T = ang.shape[0]
    rest = LANES - ROPE_DIMS
    cosf = jnp.concatenate([cos, cos, jnp.ones((T, rest), F32)], axis=1)
    sinf = jnp.concatenate([-sin, sin, jnp.zeros((T, rest), F32)], axis=1)
    return cosf, sinf


def _softmax_first(sT, vT):
    m = jnp.max(sT, axis=0, keepdims=True)
    p = jnp.exp(sT - m)
    l = jnp.sum(p, axis=0, keepdims=True)
    acc = jnp.dot(vT, p.astype(BF16), preferred_element_type=F32)
    return m, l, acc


def _softmax_update(sTs, vTs, m_ref, l_ref, acc_ref):
    m_prev = m_ref[...]
    m_new = m_prev
    for sT in sTs:
        m_new = jnp.maximum(m_new, jnp.max(sT, axis=0, keepdims=True))
    alpha = jnp.exp(m_prev - m_new)
    l = alpha * l_ref[...]
    pv = None
    for sT, vT in zip(sTs, vTs):
        p = jnp.exp(sT - m_new)
        l = l + jnp.sum(p, axis=0, keepdims=True)
        d = jnp.dot(vT, p.astype(BF16), preferred_element_type=F32)
        pv = d if pv is None else pv + d
    l_ref[...] = l
    acc_ref[...] = alpha * acc_ref[...] + pv
    m_ref[...] = m_new


def _causal_mask(sT, key0, qry0):
    tk, tq = sT.shape
    key = key0 + lax.broadcasted_iota(jnp.int32, (tk, tq), 0)
    qry = qry0 + lax.broadcasted_iota(jnp.int32, (tk, tq), 1)
    return jnp.where(key <= qry, sT, MASK_VALUE)


def _da_kernel(lam_ref, q_ref, k_ref, vT_ref, g_ref, o_ref, m0, l0, acc0, m1, l1, acc1, s_a, s_b,
               *, tq, tk, out_scale):
    qi = pl.program_id(2)
    hd = HEAD_DIM
    gd = (qi * tq) // tk
    state = ((m0, l0, acc0), (m1, l1, acc1))

    def scores(g, c):
        koff = pl.multiple_of(g * tk, tk)
        kblk = k_ref[pl.ds(koff, tk), c * hd:(c + 1) * hd]
        return lax.dot_general(kblk, q_ref[:, c * hd:(c + 1) * hd], _NT, preferred_element_type=F32)

    vT_diag = vT_ref[gd]
    for c in range(2):
        m, l, acc = _softmax_first(_causal_mask(scores(gd, c), gd * tk, qi * tq), vT_diag)
        m_ref, l_ref, acc_ref = state[c]
        m_ref[...] = m
        l_ref[...] = l
        acc_ref[...] = acc

    last_tile = jnp.maximum(gd - 1, 0)

    def fill(s_ref, g_req):
        g = jnp.minimum(g_req, last_tile)
        for c in range(2):
            s_ref[c] = scores(g, c)

    def consume(s_ref, g):
        vT = vT_ref[g]
        for c in range(2):
            _softmax_update([s_ref[c]], [vT], *state[c])

    fill(s_a, 0)

    def pair(p, carry):
        g0 = 2 * p
        fill(s_b, g0 + 1)
        consume(s_a, g0)
        fill(s_a, g0 + 2)
        consume(s_b, g0 + 1)
        return carry

    lax.fori_loop(0, gd // 2, pair, 0)

    @pl.when(lax.rem(gd, 2) == 1)
    def _():
        consume(s_a, last_tile)

    lam = lam_ref[0]
    oT = acc0[...] / l0[...] - lam * (acc1[...] / l1[...])
    o = oT.T
    o = o * lax.rsqrt(jnp.mean(o * o, axis=-1, keepdims=True) + LN_EPS) * g_ref[...]
    o_ref[...] = (o * out_scale).astype(o_ref.dtype)


def _diff_attention(qkv, vT, lam, subln_g, *, B, S, H, tq, tk, lambda_init):
    T = B * S
    nq = S // tq
    dv = 2 * HEAD_DIM
    kern = functools.partial(_da_kernel, tq=tq, tk=tk, out_scale=1.0 - lambda_init)
    return pl.pallas_call(
        kern,
        grid=(B, H, nq),
        in_specs=[
            pl.BlockSpec(memory_space=pltpu.SMEM),
            pl.BlockSpec((tq, dv), lambda b, h, i: (b * nq + i, h)),
            pl.BlockSpec((S, dv), lambda b, h, i: (b, H + h)),
            pl.BlockSpec((None, None, S // tk, dv, tk), lambda b, h, i: (b, h, 0, 0, 0)),
            pl.BlockSpec((1, dv), lambda b, h, i: (0, 0)),
        ],
        out_specs=pl.BlockSpec((tq, dv), lambda b, h, i: (b * nq + i, h)),
        out_shape=jax.ShapeDtypeStruct((T, H * dv), BF16),
        scratch_shapes=2 * [pltpu.VMEM((1, tq), F32), pltpu.VMEM((1, tq), F32), pltpu.VMEM((dv, tq), F32)]
        + 2 * [pltpu.VMEM((2, tk, tq), F32)],
        compiler_params=_params("parallel", "parallel", "arbitrary"),
        name="diff_attention",
    )(lam, qkv, qkv, vT, subln_g)


def _mb_kernel(q_ref, k_ref, vT_ref, o_ref, kmean_ref, bias_ref, m_ref, l_ref, acc_ref, s_a, s_b, *, nb, group):
    j = pl.program_id(2)
    blk = MB_BLOCK

    @pl.when(j == 0)
    def _():
        for n in range(nb):
            kb = k_ref[n * blk:(n + 1) * blk, :].astype(F32)
            kmean_ref[n:n + 1, :] = jnp.sum(kb, axis=0, keepdims=True) * (1.0 / blk)

    q = q_ref[...]
    gate = lax.dot_general(kmean_ref[...], q.astype(F32), _NT, precision=lax.Precision.HIGHEST,
                           preferred_element_type=F32)
    bid = lax.broadcasted_iota(jnp.int32, gate.shape, 0)
    rem = jnp.where(bid < j, gate, -jnp.inf)
    sel = jnp.zeros(gate.shape, jnp.bool_)
    for _ in range(MB_TOPK):
        mx = jnp.max(rem, axis=0, keepdims=True)
        first = jnp.min(jnp.where(rem == mx, bid, nb), axis=0, keepdims=True)
        pick = (bid == first) & (mx > -jnp.inf)
        sel = sel | pick
        rem = jnp.where(bid == first, -jnp.inf, rem)
    bias = jnp.where(sel, 0.0, MASK_VALUE).astype(F32)
    for gg in range(nb // group):
        bias_ref[gg] = bias[gg * group:(gg + 1) * group, :]

    def scores(n):
        koff = pl.multiple_of(n * blk, blk)
        return lax.dot_general(k_ref[pl.ds(koff, blk), :], q, _NT, preferred_element_type=F32)

    m, l, acc = _softmax_first(_causal_mask(scores(j), 0, 0), vT_ref[j])
    m_ref[...] = m
    l_ref[...] = l
    acc_ref[...] = acc

    n_groups = (j + group - 1) // group
    last_group = jnp.maximum(n_groups - 1, 0)

    def fill(s_ref, g_req):
        g = jnp.minimum(g_req, last_group)
        gbias = bias_ref[g]
        for t in range(group):
            s_ref[t] = scores(g * group + t) + gbias[t:t + 1, :]

    def consume(s_ref, g):
        _softmax_update([s_ref[t] for t in range(group)], [vT_ref[g * group + t] for t in range(group)],
                        m_ref, l_ref, acc_ref)

    fill(s_a, 0)

    def pair(p, carry):
        g0 = 2 * p
        fill(s_b, g0 + 1)
        consume(s_a, g0)
        fill(s_a, g0 + 2)
        consume(s_b, g0 + 1)
        return carry

    lax.fori_loop(0, n_groups // 2, pair, 0)

    @pl.when(lax.rem(n_groups, 2) == 1)
    def _():
        consume(s_a, last_group)

    oT = acc_ref[...] / l_ref[...]
    o_ref[...] = oT.T.astype(o_ref.dtype)


def _moba_attention(qkv, vT, *, B, S, H, group):
    T = B * S
    blk = MB_BLOCK
    nb = S // blk
    hd = HEAD_DIM
    kern = functools.partial(_mb_kernel, nb=nb, group=group)
    return pl.pallas_call(
        kern,
        grid=(B, H, nb),
        in_specs=[
            pl.BlockSpec((blk, hd), lambda b, h, i: (b * nb + i, h)),
            pl.BlockSpec((S, hd), lambda b, h, i: (b, H + h)),
            pl.BlockSpec((None, None, nb, hd, blk), lambda b, h, i: (b, h, 0, 0, 0)),
        ],
        out_specs=pl.BlockSpec((blk, hd), lambda b, h, i: (b * nb + i, h)),
        out_shape=jax.ShapeDtypeStruct((T, H * hd), BF16),
        scratch_shapes=[
            pltpu.VMEM((nb, hd), F32),
            pltpu.VMEM((nb // group, group, blk), F32),
            pltpu.VMEM((1, blk), F32),
            pltpu.VMEM((1, blk), F32),
            pltpu.VMEM((hd, blk), F32),
            pltpu.VMEM((group, blk, blk), F32),
            pltpu.VMEM((group, blk, blk), F32),
        ],
        compiler_params=_params("parallel", "parallel", "arbitrary"),
        name="moba_attention",
    )(qkv, qkv, vT)


def _value_tiles(v2d, *, B, S, H, dv, tk):
    v = v2d.reshape(B, S // tk, tk, H, dv)
    return v.transpose(0, 3, 1, 4, 2)


def _router(x1, wrT_ref, rb_ref, eidx_ref, wgt_ref):
    E = wrT_ref.shape[0]
    per_group = E // N_GROUPS
    tm = x1.shape[0]
    logits = lax.dot_general(wrT_ref[...], x1, _NT, precision=lax.Precision.HIGHEST,
                             preferred_element_type=F32)
    scores = _sigmoid(logits)
    biased = scores + rb_ref[...]

    gid8 = lax.broadcasted_iota(jnp.int32, (per_group, tm), 0)
    gscores = []
    for g in range(N_GROUPS):
        bg = biased[g * per_group:(g + 1) * per_group, :]
        m1 = jnp.max(bg, axis=0, keepdims=True)
        i1 = jnp.min(jnp.where(bg == m1, gid8, per_group), axis=0, keepdims=True)
        m2 = jnp.max(jnp.where(gid8 == i1, -jnp.inf, bg), axis=0, keepdims=True)
        gscores.append(m1 + m2)
    gs = jnp.concatenate(gscores, axis=0)

    gidx = lax.broadcasted_iota(jnp.int32, (N_GROUPS, tm), 0)
    rank = jnp.zeros((N_GROUPS, tm), jnp.int32)
    for g in range(N_GROUPS):
        row = gs[g:g + 1, :]
        beats = (row > gs) | ((row == gs) & (g < gidx))
        rank = rank + beats.astype(jnp.int32)
    gsel = rank < TOPK_GROUPS

    masked = jnp.concatenate(
        [jnp.where(gsel[g:g + 1, :], biased[g * per_group:(g + 1) * per_group, :], -jnp.inf)
         for g in range(N_GROUPS)], axis=0)

    eid = lax.broadcasted_iota(jnp.int32, (E, tm), 0)
    rem = masked
    ids, vals = [], []
    for _ in range(TOP_K):
        mx = jnp.max(rem, axis=0, keepdims=True)
        first = jnp.min(jnp.where(rem == mx, eid, E), axis=0, keepdims=True)
        onehot = eid == first
        ids.append(first)
        vals.append(jnp.sum(jnp.where(onehot, scores, 0.0), axis=0, keepdims=True))
        rem = jnp.where(onehot, -jnp.inf, rem)
    w = jnp.concatenate(vals, axis=0)
    w = w / jnp.sum(w, axis=0, keepdims=True) * ROUTED_SCALE
    eidx_ref[...] = jnp.concatenate(ids, axis=0)
    wgt_ref[...] = w


def _proj_ln_router_kernel(o_ref, w_ref, x_ref, g_ref, b_ref, wrT_ref, rb_ref,
                           x1_ref, x1p_ref, eidx_ref, wgt_ref, *, alpha):
    h = jnp.dot(o_ref[...], w_ref[...], preferred_element_type=F32)
    x1 = _layer_norm(alpha * x_ref[...] + h, g_ref[...], b_ref[...])
    x1_ref[...] = x1
    _pack_rows(x1, x1p_ref)
    _router(x1, wrT_ref, rb_ref, eidx_ref, wgt_ref)


def _proj_ln_router(o2d, w_out_bf16, x2d, g, b, wrT, rb, *, alpha, tm):
    T, D = x2d.shape
    Do = o2d.shape[1]
    E = wrT.shape[0]
    kern = functools.partial(_proj_ln_router_kernel, alpha=alpha)
    const = lambda i: (0, 0)
    return pl.pallas_call(
        kern,
        grid=(T // tm,),
        in_specs=[
            pl.BlockSpec((tm, Do), lambda i: (i, 0)),
            pl.BlockSpec((Do, D), const),
            pl.BlockSpec((tm, D), lambda i: (i, 0)),
            pl.BlockSpec((1, D), const),
            pl.BlockSpec((1, D), const),
            pl.BlockSpec((E, D), const),
            pl.BlockSpec((E, 1), const),
        ],
        out_specs=[
            pl.BlockSpec((tm, D), lambda i: (i, 0)),
            pl.BlockSpec((tm * (D // (2 * LANES)), LANES), lambda i: (i, 0)),
            pl.BlockSpec((TOP_K, tm), lambda i: (0, i)),
            pl.BlockSpec((TOP_K, tm), lambda i: (0, i)),
        ],
        out_shape=[
            jax.ShapeDtypeStruct((T, D), F32),
            jax.ShapeDtypeStruct((T * (D // (2 * LANES)), LANES), jnp.uint32),
            jax.ShapeDtypeStruct((TOP_K, T), jnp.int32),
            jax.ShapeDtypeStruct((TOP_K, T), F32),
        ],
        compiler_params=_params("parallel"),
        name="outproj_ln_router",
    )(o2d, w_out_bf16, x2d, g, b, wrT, rb)


def _routing_plan(eidxT, *, n_experts, tm):
    K, T = eidxT.shape
    n_slots = T * K
    n_tiles = n_slots // tm + n_experts
    e_flat = eidxT.reshape(-1)
    order = jnp.argsort(e_flat, stable=True).astype(jnp.int32)
    counts = jnp.sum((e_flat[:, None] == jnp.arange(n_experts, dtype=jnp.int32)[None, :]).astype(jnp.int32), axis=0)
    tiles_per_e = (counts + tm - 1) // tm
    tile_end = jnp.cumsum(tiles_per_e)
    tile_start = tile_end - tiles_per_e
    run_start = jnp.cumsum(counts) - counts
    total_tiles = tile_end[-1]

    tile_ids = jnp.arange(n_tiles, dtype=jnp.int32)
    tile_valid = tile_ids < total_tiles
    last_tile = jnp.maximum(total_tiles - 1, 0)
    tile_e = jnp.sum((jnp.minimum(tile_ids, last_tile)[:, None] >= tile_end[None, :]).astype(jnp.int32), axis=1)
    tile_e = jnp.minimum(tile_e, n_experts - 1)

    tile_off = (tile_ids - tile_start[tile_e]) * tm
    n_valid = jnp.where(tile_valid, jnp.clip(counts[tile_e] - tile_off, 0, tm), 0).astype(jnp.int32)
    r = jnp.arange(tm, dtype=jnp.int32)[None, :]
    src = jnp.clip((run_start[tile_e] + tile_off)[:, None] + r, 0, n_slots - 1)
    slots = jnp.where(r < n_valid[:, None], order[src], 0).astype(jnp.int32)
    return tile_e, n_valid, lax.rem(slots, T), slots


def _experts_kernel(te_ref, nv_ref, tok_hbm, slot_hbm, x_hbm, wg_ref, wu_ref, wd_ref, y_hbm,
                    tok_a, tok_b, slot_a, slot_b, xbuf, ybuf, tok_sems, slot_sems, gather_sems, scatter_sems,
                    *, n_sub):
    i = pl.program_id(0)
    last = pl.num_programs(0) - 1
    tm = xbuf.shape[1] // n_sub
    toks = (tok_a, tok_b)
    slots = (slot_a, slot_b)
    n_valid = nv_ref[i]

    def tok_copy(tile, b):
        return pltpu.make_async_copy(tok_hbm.at[jnp.minimum(tile, last)], toks[b], tok_sems.at[b])

    def slot_copy(tile, b):
        return pltpu.make_async_copy(slot_hbm.at[jnp.minimum(tile, last)], slots[b], slot_sems.at[b])

    def start_gather(b):
        for r in range(tm):
            src = pl.multiple_of(toks[b][r], n_sub)
            pltpu.make_async_copy(x_hbm.at[pl.ds(src, n_sub), :], xbuf.at[b, pl.ds(r * n_sub, n_sub), :],
                                  gather_sems.at[b]).start()

    def wait_gather(b):
        pltpu.make_async_copy(x_hbm.at[pl.ds(0, tm * n_sub), :], xbuf.at[b], gather_sems.at[b]).wait()

    def start_scatter(b, n):
        def row(r):
            src = pl.multiple_of(r * n_sub, n_sub)
            dst = pl.multiple_of(slots[b][r], n_sub)
            pltpu.make_async_copy(ybuf.at[b, pl.ds(src, n_sub), :], y_hbm.at[pl.ds(dst, n_sub), :],
                                  scatter_sems.at[b]).start(priority=1)

        def group(g, carry):
            for u in range(ISSUE_UNROLL):
                row(g * ISSUE_UNROLL + u)
            return carry

        def single(r, carry):
            row(r)
            return carry

        n_groups = n // ISSUE_UNROLL
        lax.fori_loop(0, n_groups, group, 0)
        lax.fori_loop(n_groups * ISSUE_UNROLL, n, single, 0)

    def wait_scatter(b, n):
        @pl.when(n > 0)
        def _():
            rows = pl.multiple_of(n * n_sub, n_sub)
            pltpu.make_async_copy(ybuf.at[b, pl.ds(0, rows), :], y_hbm.at[pl.ds(0, rows), :],
                                  scatter_sems.at[b]).wait()

    def step(cur):
        nxt = 1 - cur

        @pl.when(i == 0)
        def _():
            first = tok_copy(0, cur)
            first.start()
            first.wait()
            start_gather(cur)
            tok_copy(1, nxt).start()
            slot_copy(0, cur).start()

        tok_copy(i + 2, cur).start()
        slot_copy(i + 1, nxt).start()
        wait_gather(cur)
        tok_copy(i + 1, nxt).wait()
        start_gather(nxt)

        x_lo, x_hi = _unpack_rows(xbuf.at[cur], n_sub)
        xb = jnp.concatenate([x_lo, x_hi], axis=1).astype(BF16)
        g = jnp.dot(xb, wg_ref[...], preferred_element_type=F32)
        u = jnp.dot(xb, wu_ref[...], preferred_element_type=F32)
        h = (g * _sigmoid(g) * u).astype(BF16)
        _pack_rows(jnp.dot(h, wd_ref[...], preferred_element_type=F32), ybuf.at[cur])

        @pl.when(i > 0)
        def _():
            wait_scatter(nxt, nv_ref[jnp.maximum(i - 1, 0)])

        slot_copy(i, cur).wait()
        start_scatter(cur, n_valid)

        @pl.when(i == last)
        def _():
            wait_scatter(cur, n_valid)
            wait_gather(nxt)
            tok_copy(i + 2, cur).wait()
            slot_copy(i + 1, nxt).wait()

    parity = lax.rem(i, 2)
    for b in range(2):
        pl.when(parity == b)(functools.partial(step, b))


def _routed_experts(x1p, tile_e, tile_rows, toks, slots, wg, wu, wd, *, tm):
    D = wg.shape[1]
    F = wg.shape[2]
    S = D // (2 * LANES)
    T = x1p.shape[0] // S
    n_tiles = slots.shape[0]
    grid_spec = pltpu.PrefetchScalarGridSpec(
        num_scalar_prefetch=2,
        grid=(n_tiles,),
        in_specs=[
            pl.BlockSpec(memory_space=pl.ANY),
            pl.BlockSpec(memory_space=pl.ANY),
            pl.BlockSpec(memory_space=pl.ANY),
            pl.BlockSpec((None, D, F), lambda i, te, nv: (te[i], 0, 0)),
            pl.BlockSpec((None, D, F), lambda i, te, nv: (te[i], 0, 0)),
            pl.BlockSpec((None, F, D), lambda i, te, nv: (te[i], 0, 0)),
        ],
        out_specs=pl.BlockSpec(memory_space=pl.ANY),
        scratch_shapes=[
            pltpu.SMEM((tm,), jnp.int32),
            pltpu.SMEM((tm,), jnp.int32),
            pltpu.SMEM((tm,), jnp.int32),
            pltpu.SMEM((tm,), jnp.int32),
            pltpu.VMEM((2, tm * S, LANES), jnp.uint32),
            pltpu.VMEM((2, tm * S, LANES), jnp.uint32),
            pltpu.SemaphoreType.DMA((2,)),
            pltpu.SemaphoreType.DMA((2,)),
            pltpu.SemaphoreType.DMA((2,)),
            pltpu.SemaphoreType.DMA((2,)),
        ],
    )
    return pl.pallas_call(
        functools.partial(_experts_kernel, n_sub=S),
        grid_spec=grid_spec,
        out_shape=jax.ShapeDtypeStruct((T * TOP_K * S, LANES), jnp.uint32),
        compiler_params=_params("arbitrary"),
        name="routed_experts",
    )(tile_e, tile_rows, toks * S, slots * S, x1p, wg, wu, wd)


def _combine_kernel(y_ref, w_ref, x1_ref, wsg_ref, wsu_ref, wsd_ref, g_ref, b_ref, out_ref, *, alpha):
    x1 = x1_ref[...]
    xb = x1.astype(BF16)
    gg = jnp.dot(xb, wsg_ref[...], preferred_element_type=F32)
    uu = jnp.dot(xb, wsu_ref[...], preferred_element_type=F32)
    hh = (gg * _sigmoid(gg) * uu).astype(BF16)
    shared = jnp.dot(hh, wsd_ref[...], preferred_element_type=F32)
    w = w_ref[...]
    routed = None
    for k in range(TOP_K):
        y_lo, y_hi = _unpack_rows(y_ref.at[k], y_ref.shape[1] // x1.shape[0])
        term = w[:, k:k + 1] * jnp.concatenate([y_lo, y_hi], axis=1)
        routed = term if routed is None else routed + term
    out_ref[...] = _layer_norm(alpha * x1 + (routed + shared), g_ref[...], b_ref[...])


def _combine(y_kt, w_tk, x1, wsg, wsu, wsd, g, b, *, alpha, tm):
    T, D = x1.shape
    Fs = wsg.shape[1]
    kern = functools.partial(_combine_kernel, alpha=alpha)
    const = lambda i: (0, 0)
    return pl.pallas_call(
        kern,
        grid=(T // tm,),
        in_specs=[
            pl.BlockSpec((TOP_K, tm * (D // (2 * LANES)), LANES), lambda i: (0, i, 0)),
            pl.BlockSpec((tm, TOP_K), lambda i: (i, 0)),
            pl.BlockSpec((tm, D), lambda i: (i, 0)),
            pl.BlockSpec((D, Fs), const),
            pl.BlockSpec((D, Fs), const),
            pl.BlockSpec((Fs, D), const),
            pl.BlockSpec((1, D), const),
            pl.BlockSpec((1, D), const),
        ],
        out_specs=pl.BlockSpec((tm, D), lambda i: (i, 0)),
        out_shape=jax.ShapeDtypeStruct((T, D), F32),
        compiler_params=_params("parallel"),
        name="shared_combine_ln",
    )(y_kt, w_tk, x1, wsg, wsu, wsd, g, b)


def _pick(n, pref):
    t = min(n, pref)
    while n % t:
        t //= 2
    return t


def _moe_layer(x_attn_in, o2d, w_out, ln1_g, ln1_b, w_router, router_bias, w_gate, w_up, w_down,
               ws_gate, ws_up, ws_down, ln2_g, ln2_b, *, alpha):
    T, D = x_attn_in.shape
    E = w_router.shape[1]
    x1, x1p, eidxT, wgtT = _proj_ln_router(
        o2d, w_out.astype(BF16), x_attn_in, ln1_g[None, :], ln1_b[None, :],
        w_router.T, router_bias[:, None], alpha=alpha, tm=_pick(T, 256))
    tm_e = _pick(T * TOP_K, 256)
    tile_e, tile_rows, toks, slots = _routing_plan(eidxT, n_experts=E, tm=tm_e)
    y = _routed_experts(x1p, tile_e, tile_rows, toks, slots, w_gate.astype(BF16), w_up.astype(BF16),
                        w_down.astype(BF16), tm=tm_e)
    return _combine(y.reshape(TOP_K, T * (D // (2 * LANES)), LANES), wgtT.T, x1, ws_gate.astype(BF16), ws_up.astype(BF16), ws_down.astype(BF16),
                    ln2_g[None, :], ln2_b[None, :], alpha=alpha, tm=_pick(T, 128))


def kernel(x, positions, ln1_g, ln1_b, ln2_g, ln2_b, da_w_in, da_w_out, da_lq1, da_lk1, da_lq2, da_lk2,
           da_subln_g, mb_w_in, mb_w_out, w_router, router_bias, w_gate, w_up, w_down, ws_gate, ws_up, ws_down):
    B, S, D = x.shape
    T = B * S
    depth = ln1_g.shape[0]
    alpha = (2 * depth) ** 0.25
    hd = HEAD_DIM
    cosf, sinf = _rope_tables(positions)
    xt = x.reshape(T, D)
    tm_proj = _pick(T, 1024)

    for i in range(depth):
        m = i // 2
        if i % 2 == 0:
            w_in = da_w_in[m]
            H = w_in.shape[1] // (6 * hd)
            qk_cols = 4 * H * hd
            tq = _pick(S, 256)
            qkv = _qkv_proj(xt, w_in.astype(BF16), cosf, sinf, rope_cols=qk_cols, q_cols=qk_cols // 2,
                            tm=tm_proj, tn=_pick(qk_cols // 2, 512))
            tk = _pick(S, 1024)
            vT = _value_tiles(qkv[:, qk_cols:], B=B, S=S, H=H, dv=2 * hd, tk=tk)
            lambda_init = 0.8 - 0.6 * math.exp(-0.3 * i)
            lam = (jnp.exp(jnp.sum(da_lq1[m].astype(F32) * da_lk1[m].astype(F32)))
                   - jnp.exp(jnp.sum(da_lq2[m].astype(F32) * da_lk2[m].astype(F32))) + lambda_init)
            o2d = _diff_attention(qkv, vT, lam.reshape(1), da_subln_g[m][None, :], B=B, S=S, H=H, tq=tq,
                                  tk=tk, lambda_init=lambda_init)
            w_out = da_w_out[m]
        else:
            w_in = mb_w_in[m]
            H = w_in.shape[1] // (3 * hd)
            qk_cols = 2 * H * hd
            qkv = _qkv_proj(xt, w_in.astype(BF16), cosf, sinf, rope_cols=qk_cols, q_cols=qk_cols // 2,
                            tm=tm_proj, tn=_pick(qk_cols // 2, 512))
            vT = _value_tiles(qkv[:, qk_cols:], B=B, S=S, H=H, dv=hd, tk=MB_BLOCK)
            o2d = _moba_attention(qkv, vT, B=B, S=S, H=H, group=_pick(S // MB_BLOCK, MB_GROUP))
            w_out = mb_w_out[m]
        xt = _moe_layer(xt, o2d, w_out, ln1_g[i], ln1_b[i], w_router[i], router_bias[i], w_gate[i], w_up[i],
                        w_down[i], ws_gate[i], ws_up[i], ws_down[i], ln2_g[i], ln2_b[i], alpha=alpha)
    return xt.reshape(B, S, D)
```

```python
import functools
import math

import jax
import jax.numpy as jnp
from jax import lax
from jax.experimental import pallas as pl
from jax.experimental.pallas import tpu as pltpu

F32 = jnp.float32
BF16 = jnp.bfloat16

HEAD_DIM = 128
ROPE_THETA = 500000.0
ROPE_DIMS = HEAD_DIM // 4
ROPE_HALF = ROPE_DIMS // 2
LN_EPS = 1e-5
MB_BLOCK = 256
MB_TOPK = 3
MB_GROUP = 4
TOP_K = 8
N_GROUPS = 8
TOPK_GROUPS = 4
ROUTED_SCALE = 2.5
MASK_VALUE = -1e30

LANES = 128
SUBLANES = 8
ISSUE_UNROLL = 8
VMEM_LIMIT_BYTES = 56 * 1024 * 1024

_NT = (((1,), (1,)), ((), ()))


def _params(*sem):
    return pltpu.CompilerParams(dimension_semantics=sem, vmem_limit_bytes=VMEM_LIMIT_BYTES)


def _sigmoid(x):
    return 1.0 / (1.0 + jnp.exp(-x))


def _pack_rows(x, o_ref):
    tm, d = x.shape
    half = d // 2
    n_sub = half // LANES
    lo = pltpu.bitcast(x[:, :half].astype(BF16).astype(F32), jnp.uint32) >> 16
    hi = pltpu.bitcast(x[:, half:].astype(BF16).astype(F32), jnp.uint32) & jnp.uint32(0xFFFF0000)
    packed = hi | lo
    for s in range(n_sub):
        o_ref[pl.ds(s, tm, stride=n_sub), :] = packed[:, s * LANES:(s + 1) * LANES]


def _unpack_rows(p_ref, n_sub):
    tm = p_ref.shape[0] // n_sub
    p = jnp.concatenate([p_ref[pl.ds(s, tm, stride=n_sub), :] for s in range(n_sub)], axis=1)
    return pltpu.bitcast(p << 16, F32), pltpu.bitcast(p & jnp.uint32(0xFFFF0000), F32)


def _layer_norm(y, g, b):
    mu = jnp.mean(y, axis=-1, keepdims=True)
    d = y - mu
    var = jnp.mean(d * d, axis=-1, keepdims=True)
    return d * lax.rsqrt(var + LN_EPS) * g + b


def _qkv_kernel(x_ref, w_ref, cos_ref, sin_ref, o_ref, xb_ref, *, rope_blocks, q_blocks, q_scale):
    j = pl.program_id(1)

    @pl.when(j == 0)
    def _():
        xb_ref[...] = x_ref[...].astype(BF16)

    acc = jnp.dot(xb_ref[...], w_ref[...], preferred_element_type=F32)
    tn = acc.shape[1]

    @pl.when(j < rope_blocks)
    def _():
        cosf = cos_ref[...]
        sinf = sin_ref[...]
        lane = lax.broadcasted_iota(jnp.int32, (1, LANES), 1)
        scale = jnp.where(j < q_blocks, q_scale, 1.0).astype(F32)
        for c in range(tn // LANES):
            xc = acc[:, c * LANES:(c + 1) * LANES]
            partner = jnp.where(lane < ROPE_HALF,
                                pltpu.roll(xc, LANES - ROPE_HALF, 1),
                                pltpu.roll(xc, ROPE_HALF, 1))
            r = (xc * cosf + partner * sinf) * scale
            o_ref[:, c * LANES:(c + 1) * LANES] = r.astype(o_ref.dtype)

    @pl.when(j >= rope_blocks)
    def _():
        o_ref[...] = acc.astype(o_ref.dtype)


def _qkv_proj(x2d, w_bf16, cosf, sinf, *, rope_cols, q_cols, tm, tn):
    T, D = x2d.shape
    N = w_bf16.shape[1]
    kern = functools.partial(_qkv_kernel, rope_blocks=rope_cols // tn, q_blocks=q_cols // tn,
                             q_scale=HEAD_DIM ** -0.5)
    return pl.pallas_call(
        kern,
        grid=(T // tm, N // tn),
        in_specs=[
            pl.BlockSpec((tm, D), lambda i, j: (i, 0)),
            pl.BlockSpec((D, tn), lambda i, j: (0, j)),
            pl.BlockSpec((tm, LANES), lambda i, j: (i, 0)),
            pl.BlockSpec((tm, LANES), lambda i, j: (i, 0)),
        ],
        out_specs=pl.BlockSpec((tm, tn), lambda i, j: (i, j)),
        out_shape=jax.ShapeDtypeStruct((T, N), BF16),
        scratch_shapes=[pltpu.VMEM((tm, D), BF16)],
        compiler_params=_params("parallel", "arbitrary"),
        name="qkv_proj",
    )(x2d, w_bf16, cosf, sinf)


def _rope_tables(positions):
    inv_freq = ROPE_THETA ** (-jnp.arange(0, ROPE_DIMS, 2, dtype=F32) / ROPE_DIMS)
    ang = positions.astype(F32).reshape(-1)[:, None] * inv_freq
    cos, sin = jnp.cos(ang), jnp.sin(ang)
    T = ang.shape[0]
    rest = LANES - ROPE_DIMS
    cosf = jnp.concatenate([cos, cos, jnp.ones((T, rest), F32)], axis=1)
    sinf = jnp.concatenate([-sin, sin, jnp.zeros((T, rest), F32)], axis=1)
    return cosf, sinf


def _softmax_first(sT, vT):
    m = jnp.max(sT, axis=0, keepdims=True)
    p = jnp.exp(sT - m)
    l = jnp.sum(p, axis=0, keepdims=True)
    acc = jnp.dot(vT, p.astype(BF16), preferred_element_type=F32)
    return m, l, acc


def _softmax_update(sTs, vTs, m_ref, l_ref, acc_ref):
    m_prev = m_ref[...]
    m_new = m_prev
    for sT in sTs:
        m_new = jnp.maximum(m_new, jnp.max(sT, axis=0, keepdims=True))
    alpha = jnp.exp(m_prev - m_new)
    l = alpha * l_ref[...]
    pv = None
    for sT, vT in zip(sTs, vTs):
        p = jnp.exp(sT - m_new)
        l = l + jnp.sum(p, axis=0, keepdims=True)
        d = jnp.dot(vT, p.astype(BF16), preferred_element_type=F32)
        pv = d if pv is None else pv + d
    l_ref[...] = l
    acc_ref[...] = alpha * acc_ref[...] + pv
    m_ref[...] = m_new


def _causal_mask(sT, key0, qry0):
    tk, tq = sT.shape
    key = key0 + lax.broadcasted_iota(jnp.int32, (tk, tq), 0)
    qry = qry0 + lax.broadcasted_iota(jnp.int32, (tk, tq), 1)
    return jnp.where(key <= qry, sT, MASK_VALUE)


def _da_kernel(lam_ref, q_ref, k_ref, vT_ref, g_ref, o_ref, m0, l0, acc0, m1, l1, acc1, s_a, s_b,
               *, tq, tk, out_scale):
    qi = pl.program_id(2)
    hd = HEAD_DIM
    gd = (qi * tq) // tk
    state = ((m0, l0, acc0), (m1, l1, acc1))

    def scores(g, c):
        koff = pl.multiple_of(g * tk, tk)
        kblk = k_ref[pl.ds(koff, tk), c * hd:(c + 1) * hd]
        return lax.dot_general(kblk, q_ref[:, c * hd:(c + 1) * hd], _NT, preferred_element_type=F32)

    vT_diag = vT_ref[gd]
    for c in range(2):
        m, l, acc = _softmax_first(_causal_mask(scores(gd, c), gd * tk, qi * tq), vT_diag)
        m_ref, l_ref, acc_ref = state[c]
        m_ref[...] = m
        l_ref[...] = l
        acc_ref[...] = acc

    last_tile = jnp.maximum(gd - 1, 0)

    def fill(s_ref, g_req):
        g = jnp.minimum(g_req, last_tile)
        for c in range(2):
            s_ref[c] = scores(g, c)

    def consume(s_ref, g):
        vT = vT_ref[g]
        for c in range(2):
            _softmax_update([s_ref[c]], [vT], *state[c])

    fill(s_a, 0)

    def pair(p, carry):
        g0 = 2 * p
        fill(s_b, g0 + 1)
        consume(s_a, g0)
        fill(s_a, g0 + 2)
        consume(s_b, g0 + 1)
        return carry

    lax.fori_loop(0, gd // 2, pair, 0)

    @pl.when(lax.rem(gd, 2) == 1)
    def _():
        consume(s_a, last_tile)

    lam = lam_ref[0]
    oT = acc0[...] / l0[...] - lam * (acc1[...] / l1[...])
    o = oT.T
    o = o * lax.rsqrt(jnp.mean(o * o, axis=-1, keepdims=True) + LN_EPS) * g_ref[...]
    o_ref[...] = (o * out_scale).astype(o_ref.dtype)


def _diff_attention(qkv, vT, lam, subln_g, *, B, S, H, tq, tk, lambda_init):
    T = B * S
    nq = S // tq
    dv = 2 * HEAD_DIM
    kern = functools.partial(_da_kernel, tq=tq, tk=tk, out_scale=1.0 - lambda_init)
    return pl.pallas_call(
        kern,
        grid=(B, H, nq),
        in_specs=[
            pl.BlockSpec(memory_space=pltpu.SMEM),
            pl.BlockSpec((tq, dv), lambda b, h, i: (b * nq + i, h)),
            pl.BlockSpec((S, dv), lambda b, h, i: (b, H + h)),
            pl.BlockSpec((None, None, S // tk, dv, tk), lambda b, h, i: (b, h, 0, 0, 0)),
            pl.BlockSpec((1, dv), lambda b, h, i: (0, 0)),
        ],
        out_specs=pl.BlockSpec((tq, dv), lambda b, h, i: (b * nq + i, h)),
        out_shape=jax.ShapeDtypeStruct((T, H * dv), BF16),
        scratch_shapes=2 * [pltpu.VMEM((1, tq), F32), pltpu.VMEM((1, tq), F32), pltpu.VMEM((dv, tq), F32)]
        + 2 * [pltpu.VMEM((2, tk, tq), F32)],
        compiler_params=_params("parallel", "parallel", "arbitrary"),
        name="diff_attention",
    )(lam, qkv, qkv, vT, subln_g)


def _mb_kernel(q_ref, k_ref, vT_ref, o_ref, kmean_ref, bias_ref, m_ref, l_ref, acc_ref, s_a, s_b, *, nb, group):
    j = pl.program_id(2)
    blk = MB_BLOCK

    @pl.when(j == 0)
    def _():
        for n in range(nb):
            kb = k_ref[n * blk:(n + 1) * blk, :].astype(F32)
            kmean_ref[n:n + 1, :] = jnp.sum(kb, axis=0, keepdims=True) * (1.0 / blk)

    q = q_ref[...]
    km = kmean_ref[...]
    km_hi = km.astype(BF16)
    km_lo = (km - km_hi.astype(F32)).astype(BF16)
    gate = (lax.dot_general(km_hi, q, _NT, preferred_element_type=F32)
            + lax.dot_general(km_lo, q, _NT, preferred_element_type=F32))
    bid = lax.broadcasted_iota(jnp.int32, gate.shape, 0)
    rem = jnp.where(bid < j, gate, -jnp.inf)
    sel = jnp.zeros(gate.shape, jnp.bool_)
    for _ in range(MB_TOPK):
        mx = jnp.max(rem, axis=0, keepdims=True)
        first = jnp.min(jnp.where(rem == mx, bid, nb), axis=0, keepdims=True)
        pick = (bid == first) & (mx > -jnp.inf)
        sel = sel | pick
        rem = jnp.where(bid == first, -jnp.inf, rem)
    bias = jnp.where(sel, 0.0, MASK_VALUE).astype(F32)
    for gg in range(nb // group):
        bias_ref[gg] = bias[gg * group:(gg + 1) * group, :]

    def scores(n):
        koff = pl.multiple_of(n * blk, blk)
        return lax.dot_general(k_ref[pl.ds(koff, blk), :], q, _NT, preferred_element_type=F32)

    m, l, acc = _softmax_first(_causal_mask(scores(j), 0, 0), vT_ref[j])
    m_ref[...] = m
    l_ref[...] = l
    acc_ref[...] = acc

    n_groups = (j + group - 1) // group
    last_group = jnp.maximum(n_groups - 1, 0)

    def fill(s_ref, g_req):
        g = jnp.minimum(g_req, last_group)
        gbias = bias_ref[g]
        for t in range(group):
            s_ref[t] = scores(g * group + t) + gbias[t:t + 1, :]

    def consume(s_ref, g):
        _softmax_update([s_ref[t] for t in range(group)], [vT_ref[g * group + t] for t in range(group)],
                        m_ref, l_ref, acc_ref)

    fill(s_a, 0)

    def pair(p, carry):
        g0 = 2 * p
        fill(s_b, g0 + 1)
        consume(s_a, g0)
        fill(s_a, g0 + 2)
        consume(s_b, g0 + 1)
        return carry

    lax.fori_loop(0, n_groups // 2, pair, 0)

    @pl.when(lax.rem(n_groups, 2) == 1)
    def _():
        consume(s_a, last_group)

    oT = acc_ref[...] / l_ref[...]
    o_ref[...] = oT.T.astype(o_ref.dtype)


def _moba_attention(qkv, vT, *, B, S, H, group):
    T = B * S
    blk = MB_BLOCK
    nb = S // blk
    hd = HEAD_DIM
    kern = functools.partial(_mb_kernel, nb=nb, group=group)
    return pl.pallas_call(
        kern,
        grid=(B, H, nb),
        in_specs=[
            pl.BlockSpec((blk, hd), lambda b, h, i: (b * nb + i, h)),
            pl.BlockSpec((S, hd), lambda b, h, i: (b, H + h)),
            pl.BlockSpec((None, None, nb, hd, blk), lambda b, h, i: (b, h, 0, 0, 0)),
        ],
        out_specs=pl.BlockSpec((blk, hd), lambda b, h, i: (b * nb + i, h)),
        out_shape=jax.ShapeDtypeStruct((T, H * hd), BF16),
        scratch_shapes=[
            pltpu.VMEM((nb, hd), F32),
            pltpu.VMEM((nb // group, group, blk), F32),
            pltpu.VMEM((1, blk), F32),
            pltpu.VMEM((1, blk), F32),
            pltpu.VMEM((hd, blk), F32),
            pltpu.VMEM((group, blk, blk), F32),
            pltpu.VMEM((group, blk, blk), F32),
        ],
        compiler_params=_params("parallel", "parallel", "arbitrary"),
        name="moba_attention",
    )(qkv, qkv, vT)


def _value_tiles(v2d, *, B, S, H, dv, tk):
    v = v2d.reshape(B, S // tk, tk, H, dv)
    return v.transpose(0, 3, 1, 4, 2)


def _router(x1, wr_ref, rb_ref, eidx_ref, wgt_ref):
    E = wr_ref.shape[1] // 2
    per_group = E // N_GROUPS
    tm = x1.shape[0]
    x_hi = x1.astype(BF16)
    x_lo = (x1 - x_hi.astype(F32)).astype(BF16)
    parts = (jnp.dot(x_hi, wr_ref[...], preferred_element_type=F32)
             + jnp.dot(x_lo, wr_ref[...], preferred_element_type=F32))
    logits = (parts[:, :E] + parts[:, E:]).T
    scores = _sigmoid(logits)
    biased = scores + rb_ref[...]

    gid8 = lax.broadcasted_iota(jnp.int32, (per_group, tm), 0)
    gscores = []
    for g in range(N_GROUPS):
        bg = biased[g * per_group:(g + 1) * per_group, :]
        m1 = jnp.max(bg, axis=0, keepdims=True)
        i1 = jnp.min(jnp.where(bg == m1, gid8, per_group), axis=0, keepdims=True)
        m2 = jnp.max(jnp.where(gid8 == i1, -jnp.inf, bg), axis=0, keepdims=True)
        gscores.append(m1 + m2)
    gs = jnp.concatenate(gscores, axis=0)

    gidx = lax.broadcasted_iota(jnp.int32, (N_GROUPS, tm), 0)
    rank = jnp.zeros((N_GROUPS, tm), jnp.int32)
    for g in range(N_GROUPS):
        row = gs[g:g + 1, :]
        beats = (row > gs) | ((row == gs) & (g < gidx))
        rank = rank + beats.astype(jnp.int32)
    gsel = rank < TOPK_GROUPS

    masked = jnp.concatenate(
        [jnp.where(gsel[g:g + 1, :], biased[g * per_group:(g + 1) * per_group, :], -jnp.inf)
         for g in range(N_GROUPS)], axis=0)

    eid = lax.broadcasted_iota(jnp.int32, (E, tm), 0)
    rem = masked
    ids, vals = [], []
    for _ in range(TOP_K):
        mx = jnp.max(rem, axis=0, keepdims=True)
        first = jnp.min(jnp.where(rem == mx, eid, E), axis=0, keepdims=True)
        onehot = eid == first
        ids.append(first)
        vals.append(jnp.sum(jnp.where(onehot, scores, 0.0), axis=0, keepdims=True))
        rem = jnp.where(onehot, -jnp.inf, rem)
    w = jnp.concatenate(vals, axis=0)
    w = w / jnp.sum(w, axis=0, keepdims=True) * ROUTED_SCALE
    eidx_ref[...] = jnp.concatenate(ids, axis=0)
    wgt_ref[...] = w


def _proj_ln_router_kernel(o_ref, w_ref, x_ref, g_ref, b_ref, wr_ref, rb_ref,
                           x1_ref, x1p_ref, eidx_ref, wgt_ref, *, alpha):
    h = jnp.dot(o_ref[...], w_ref[...], preferred_element_type=F32)
    x1 = _layer_norm(alpha * x_ref[...] + h, g_ref[...], b_ref[...])
    x1_ref[...] = x1
    _pack_rows(x1, x1p_ref)
    _router(x1, wr_ref, rb_ref, eidx_ref, wgt_ref)


def _proj_ln_router(o2d, w_out_bf16, x2d, g, b, wr, rb, *, alpha, tm):
    T, D = x2d.shape
    Do = o2d.shape[1]
    E = wr.shape[1] // 2
    kern = functools.partial(_proj_ln_router_kernel, alpha=alpha)
    const = lambda i: (0, 0)
    return pl.pallas_call(
        kern,
        grid=(T // tm,),
        in_specs=[
            pl.BlockSpec((tm, Do), lambda i: (i, 0)),
            pl.BlockSpec((Do, D), const),
            pl.BlockSpec((tm, D), lambda i: (i, 0)),
            pl.BlockSpec((1, D), const),
            pl.BlockSpec((1, D), const),
            pl.BlockSpec((D, 2 * E), const),
            pl.BlockSpec((E, 1), const),
        ],
        out_specs=[
            pl.BlockSpec((tm, D), lambda i: (i, 0)),
            pl.BlockSpec((tm * (D // (2 * LANES)), LANES), lambda i: (i, 0)),
            pl.BlockSpec((TOP_K, tm), lambda i: (0, i)),
            pl.BlockSpec((TOP_K, tm), lambda i: (0, i)),
        ],
        out_shape=[
            jax.ShapeDtypeStruct((T, D), F32),
            jax.ShapeDtypeStruct((T * (D // (2 * LANES)), LANES), jnp.uint32),
            jax.ShapeDtypeStruct((TOP_K, T), jnp.int32),
            jax.ShapeDtypeStruct((TOP_K, T), F32),
        ],
        compiler_params=_params("parallel"),
        name="outproj_ln_router",
    )(o2d, w_out_bf16, x2d, g, b, wr, rb)


def _routing_plan(eidxT, *, n_experts, tm):
    K, T = eidxT.shape
    n_slots = T * K
    n_tiles = n_slots // tm + n_experts
    e_flat = eidxT.reshape(-1)
    order = jnp.argsort(e_flat, stable=True).astype(jnp.int32)
    counts = jnp.sum((e_flat[:, None] == jnp.arange(n_experts, dtype=jnp.int32)[None, :]).astype(jnp.int32), axis=0)
    tiles_per_e = (counts + tm - 1) // tm
    tile_end = jnp.cumsum(tiles_per_e)
    tile_start = tile_end - tiles_per_e
    run_start = jnp.cumsum(counts) - counts
    total_tiles = tile_end[-1]

    tile_ids = jnp.arange(n_tiles, dtype=jnp.int32)
    tile_valid = tile_ids < total_tiles
    last_tile = jnp.maximum(total_tiles - 1, 0)
    tile_e = jnp.sum((jnp.minimum(tile_ids, last_tile)[:, None] >= tile_end[None, :]).astype(jnp.int32), axis=1)
    tile_e = jnp.minimum(tile_e, n_experts - 1)

    tile_off = (tile_ids - tile_start[tile_e]) * tm
    n_valid = jnp.where(tile_valid, jnp.clip(counts[tile_e] - tile_off, 0, tm), 0).astype(jnp.int32)
    r = jnp.arange(tm, dtype=jnp.int32)[None, :]
    src = jnp.clip((run_start[tile_e] + tile_off)[:, None] + r, 0, n_slots - 1)
    slots = jnp.where(r < n_valid[:, None], order[src], 0).astype(jnp.int32)
    return tile_e, n_valid, lax.rem(slots, T), slots


def _experts_kernel(te_ref, nv_ref, tok_hbm, slot_hbm, x_hbm, wg_ref, wu_ref, wd_ref, y_hbm,
                    tok_a, tok_b, slot_a, slot_b, xbuf, ybuf, wg_bf, wu_bf, wd_bf,
                    tok_sems, slot_sems, gather_sems, scatter_sems, *, n_sub):
    i = pl.program_id(0)
    last = pl.num_programs(0) - 1
    tm = xbuf.shape[1] // n_sub
    toks = (tok_a, tok_b)
    slots = (slot_a, slot_b)
    n_valid = nv_ref[i]

    @pl.when((i == 0) | (te_ref[i] != te_ref[jnp.maximum(i - 1, 0)]))
    def _():
        wg_bf[...] = wg_ref[...].astype(BF16)
        wu_bf[...] = wu_ref[...].astype(BF16)
        wd_bf[...] = wd_ref[...].astype(BF16)

    def tok_copy(tile, b):
        return pltpu.make_async_copy(tok_hbm.at[jnp.minimum(tile, last)], toks[b], tok_sems.at[b])

    def slot_copy(tile, b):
        return pltpu.make_async_copy(slot_hbm.at[jnp.minimum(tile, last)], slots[b], slot_sems.at[b])

    def start_gather(b):
        for r in range(tm):
            src = pl.multiple_of(toks[b][r], n_sub)
            pltpu.make_async_copy(x_hbm.at[pl.ds(src, n_sub), :], xbuf.at[b, pl.ds(r * n_sub, n_sub), :],
                                  gather_sems.at[b]).start(priority=r % 2)

    def wait_gather(b):
        pltpu.make_async_copy(x_hbm.at[pl.ds(0, tm * n_sub), :], xbuf.at[b], gather_sems.at[b]).wait()

    def start_scatter(b, n):
        def row(r):
            src = pl.multiple_of(r * n_sub, n_sub)
            dst = pl.multiple_of(slots[b][r], n_sub)
            pltpu.make_async_copy(ybuf.at[b, pl.ds(src, n_sub), :], y_hbm.at[pl.ds(dst, n_sub), :],
                                  scatter_sems.at[b]).start(priority=1)

        def group(g, carry):
            for u in range(ISSUE_UNROLL):
                row(g * ISSUE_UNROLL + u)
            return carry

        def single(r, carry):
            row(r)
            return carry

        n_groups = n // ISSUE_UNROLL
        lax.fori_loop(0, n_groups, group, 0)
        lax.fori_loop(n_groups * ISSUE_UNROLL, n, single, 0)

    def wait_scatter(b, n):
        @pl.when(n > 0)
        def _():
            rows = pl.multiple_of(n * n_sub, n_sub)
            pltpu.make_async_copy(ybuf.at[b, pl.ds(0, rows), :], y_hbm.at[pl.ds(0, rows), :],
                                  scatter_sems.at[b]).wait()

    def step(cur):
        nxt = 1 - cur

        @pl.when(i == 0)
        def _():
            first = tok_copy(0, cur)
            first.start()
            first.wait()
            start_gather(cur)
            tok_copy(1, nxt).start()
            slot_copy(0, cur).start()

        tok_copy(i + 2, cur).start()
        slot_copy(i + 1, nxt).start()
        wait_gather(cur)
        tok_copy(i + 1, nxt).wait()
        start_gather(nxt)

        x_lo, x_hi = _unpack_rows(xbuf.at[cur], n_sub)
        xb = jnp.concatenate([x_lo, x_hi], axis=1).astype(BF16)
        g = jnp.dot(xb, wg_bf[...], preferred_element_type=F32)
        u = jnp.dot(xb, wu_bf[...], preferred_element_type=F32)
        h = (g * _sigmoid(g) * u).astype(BF16)
        _pack_rows(jnp.dot(h, wd_bf[...], preferred_element_type=F32), ybuf.at[cur])

        @pl.when(i > 0)
        def _():
            wait_scatter(nxt, nv_ref[jnp.maximum(i - 1, 0)])

        slot_copy(i, cur).wait()
        start_scatter(cur, n_valid)

        @pl.when(i == last)
        def _():
            wait_scatter(cur, n_valid)
            wait_gather(nxt)
            tok_copy(i + 2, cur).wait()
            slot_copy(i + 1, nxt).wait()

    parity = lax.rem(i, 2)
    for b in range(2):
        pl.when(parity == b)(functools.partial(step, b))


def _routed_experts(x1p, tile_e, tile_rows, toks, slots, wg, wu, wd, *, tm):
    D = wg.shape[1]
    F = wg.shape[2]
    S = D // (2 * LANES)
    T = x1p.shape[0] // S
    n_tiles = slots.shape[0]
    grid_spec = pltpu.PrefetchScalarGridSpec(
        num_scalar_prefetch=2,
        grid=(n_tiles,),
        in_specs=[
            pl.BlockSpec(memory_space=pl.ANY),
            pl.BlockSpec(memory_space=pl.ANY),
            pl.BlockSpec(memory_space=pl.ANY),
            pl.BlockSpec((None, D, F), lambda i, te, nv: (te[i], 0, 0)),
            pl.BlockSpec((None, D, F), lambda i, te, nv: (te[i], 0, 0)),
            pl.BlockSpec((None, F, D), lambda i, te, nv: (te[i], 0, 0)),
        ],
        out_specs=pl.BlockSpec(memory_space=pl.ANY),
        scratch_shapes=[
            pltpu.SMEM((tm,), jnp.int32),
            pltpu.SMEM((tm,), jnp.int32),
            pltpu.SMEM((tm,), jnp.int32),
            pltpu.SMEM((tm,), jnp.int32),
            pltpu.VMEM((2, tm * S, LANES), jnp.uint32),
            pltpu.VMEM((2, tm * S, LANES), jnp.uint32),
            pltpu.VMEM((D, F), BF16),
            pltpu.VMEM((D, F), BF16),
            pltpu.VMEM((F, D), BF16),
            pltpu.SemaphoreType.DMA((2,)),
            pltpu.SemaphoreType.DMA((2,)),
            pltpu.SemaphoreType.DMA((2,)),
            pltpu.SemaphoreType.DMA((2,)),
        ],
    )
    return pl.pallas_call(
        functools.partial(_experts_kernel, n_sub=S),
        grid_spec=grid_spec,
        out_shape=jax.ShapeDtypeStruct((T * TOP_K * S, LANES), jnp.uint32),
        compiler_params=_params("arbitrary"),
        name="routed_experts",
    )(tile_e, tile_rows, toks * S, slots * S, x1p, wg, wu, wd)


def _combine_kernel(y_ref, w_ref, x1_ref, wsg_ref, wsu_ref, wsd_ref, g_ref, b_ref, out_ref, *, alpha):
    x1 = x1_ref[...]
    xb = x1.astype(BF16)
    gg = jnp.dot(xb, wsg_ref[...], preferred_element_type=F32)
    uu = jnp.dot(xb, wsu_ref[...], preferred_element_type=F32)
    hh = (gg * _sigmoid(gg) * uu).astype(BF16)
    shared = jnp.dot(hh, wsd_ref[...], preferred_element_type=F32)
    w = w_ref[...]
    routed = None
    for k in range(TOP_K):
        y_lo, y_hi = _unpack_rows(y_ref.at[k], y_ref.shape[1] // x1.shape[0])
        term = w[:, k:k + 1] * jnp.concatenate([y_lo, y_hi], axis=1)
        routed = term if routed is None else routed + term
    out_ref[...] = _layer_norm(alpha * x1 + (routed + shared), g_ref[...], b_ref[...])


def _combine(y_kt, w_tk, x1, wsg, wsu, wsd, g, b, *, alpha, tm):
    T, D = x1.shape
    Fs = wsg.shape[1]
    kern = functools.partial(_combine_kernel, alpha=alpha)
    const = lambda i: (0, 0)
    return pl.pallas_call(
        kern,
        grid=(T // tm,),
        in_specs=[
            pl.BlockSpec((TOP_K, tm * (D // (2 * LANES)), LANES), lambda i: (0, i, 0)),
            pl.BlockSpec((tm, TOP_K), lambda i: (i, 0)),
            pl.BlockSpec((tm, D), lambda i: (i, 0)),
            pl.BlockSpec((D, Fs), const),
            pl.BlockSpec((D, Fs), const),
            pl.BlockSpec((Fs, D), const),
            pl.BlockSpec((1, D), const),
            pl.BlockSpec((1, D), const),
        ],
        out_specs=pl.BlockSpec((tm, D), lambda i: (i, 0)),
        out_shape=jax.ShapeDtypeStruct((T, D), F32),
        compiler_params=_params("parallel"),
        name="shared_combine_ln",
    )(y_kt, w_tk, x1, wsg, wsu, wsd, g, b)


def _split_bf16(w):
    hi = w.astype(BF16)
    lo = (w - hi.astype(F32)).astype(BF16)
    return jnp.concatenate([hi, lo], axis=-1)


def _pick(n, pref):
    t = min(n, pref)
    while n % t:
        t //= 2
    return t


def _moe_layer(x_attn_in, o2d, w_out, ln1_g, ln1_b, w_router, router_bias, w_gate, w_up, w_down,
               ws_gate, ws_up, ws_down, ln2_g, ln2_b, *, alpha):
    T, D = x_attn_in.shape
    E = w_router.shape[1]
    x1, x1p, eidxT, wgtT = _proj_ln_router(
        o2d, w_out.astype(BF16), x_attn_in, ln1_g[None, :], ln1_b[None, :],
        _split_bf16(w_router), router_bias[:, None], alpha=alpha, tm=_pick(T, 256))
    tm_e = _pick(T * TOP_K, 256)
    tile_e, tile_rows, toks, slots = _routing_plan(eidxT, n_experts=E, tm=tm_e)
    y = _routed_experts(x1p, tile_e, tile_rows, toks, slots, w_gate, w_up, w_down, tm=tm_e)
    return _combine(y.reshape(TOP_K, T * (D // (2 * LANES)), LANES), wgtT.T, x1, ws_gate.astype(BF16), ws_up.astype(BF16), ws_down.astype(BF16),
                    ln2_g[None, :], ln2_b[None, :], alpha=alpha, tm=_pick(T, 128))


def kernel(x, positions, ln1_g, ln1_b, ln2_g, ln2_b, da_w_in, da_w_out, da_lq1, da_lk1, da_lq2, da_lk2,
           da_subln_g, mb_w_in, mb_w_out, w_router, router_bias, w_gate, w_up, w_down, ws_gate, ws_up, ws_down):
    B, S, D = x.shape
    T = B * S
    depth = ln1_g.shape[0]
    alpha = (2 * depth) ** 0.25
    hd = HEAD_DIM
    cosf, sinf = _rope_tables(positions)
    xt = x.reshape(T, D)
    tm_proj = _pick(T, 1024)

    for i in range(depth):
        m = i // 2
        if i % 2 == 0:
            w_in = da_w_in[m]
            H = w_in.shape[1] // (6 * hd)
            qk_cols = 4 * H * hd
            tq = _pick(S, 256)
            qkv = _qkv_proj(xt, w_in.astype(BF16), cosf, sinf, rope_cols=qk_cols, q_cols=qk_cols // 2,
                            tm=tm_proj, tn=_pick(qk_cols // 2, 512))
            tk = _pick(S, 1024)
            vT = _value_tiles(qkv[:, qk_cols:], B=B, S=S, H=H, dv=2 * hd, tk=tk)
            lambda_init = 0.8 - 0.6 * math.exp(-0.3 * i)
            lam = (jnp.exp(jnp.sum(da_lq1[m].astype(F32) * da_lk1[m].astype(F32)))
                   - jnp.exp(jnp.sum(da_lq2[m].astype(F32) * da_lk2[m].astype(F32))) + lambda_init)
            o2d = _diff_attention(qkv, vT, lam.reshape(1), da_subln_g[m][None, :], B=B, S=S, H=H, tq=tq,
                                  tk=tk, lambda_init=lambda_init)
            w_out = da_w_out[m]
        else:
            w_in = mb_w_in[m]
            H = w_in.shape[1] // (3 * hd)
            qk_cols = 2 * H * hd
            qkv = _qkv_proj(xt, w_in.astype(BF16), cosf, sinf, rope_cols=qk_cols, q_cols=qk_cols // 2,
                            tm=tm_proj, tn=_pick(qk_cols // 2, 512))
            vT = _value_tiles(qkv[:, qk_cols:], B=B, S=S, H=H, dv=hd, tk=MB_BLOCK)
            o2d = _moba_attention(qkv, vT, B=B, S=S, H=H, group=_pick(S // MB_BLOCK, MB_GROUP))
            w_out = mb_w_out[m]
        xt = _moe_layer(xt, o2d, w_out, ln1_g[i], ln1_b[i], w_router[i], router_bias[i], w_gate[i], w_up[i],
                        w_down[i], ws_gate[i], ws_up[i], ws_down[i], ln2_g[i], ln2_b[i], alpha=alpha)
    return xt.reshape(B, S, D)
```

```python
import functools
import math

import jax
import jax.numpy as jnp
from jax import lax
from jax.experimental import pallas as pl
from jax.experimental.pallas import tpu as pltpu

F32 = jnp.float32
BF16 = jnp.bfloat16

HEAD_DIM = 128
ROPE_THETA = 500000.0
ROPE_DIMS = HEAD_DIM // 4
ROPE_HALF = ROPE_DIMS // 2
LN_EPS = 1e-5
MB_BLOCK = 256
MB_TOPK = 3
MB_GROUP = 4
TOP_K = 8
N_GROUPS = 8
TOPK_GROUPS = 4
ROUTED_SCALE = 2.5
MASK_VALUE = -1e30

LANES = 128
SUBLANES = 8
ISSUE_UNROLL = 8
VMEM_LIMIT_BYTES = 56 * 1024 * 1024

_NT = (((1,), (1,)), ((), ()))
_TN = (((0,), (0,)), ((), ()))


def _params(*sem):
    return pltpu.CompilerParams(dimension_semantics=sem, vmem_limit_bytes=VMEM_LIMIT_BYTES)


def _sigmoid(x):
    return 1.0 / (1.0 + jnp.exp(-x))


def _pack_rows(x, o_ref):
    tm, d = x.shape
    half = d // 2
    n_sub = half // LANES
    lo = pltpu.bitcast(x[:, :half].astype(BF16).astype(F32), jnp.uint32) >> 16
    hi = pltpu.bitcast(x[:, half:].astype(BF16).astype(F32), jnp.uint32) & jnp.uint32(0xFFFF0000)
    packed = hi | lo
    for s in range(n_sub):
        o_ref[pl.ds(s, tm, stride=n_sub), :] = packed[:, s * LANES:(s + 1) * LANES]


def _unpack_rows(p_ref, n_sub):
    tm = p_ref.shape[0] // n_sub
    p = jnp.concatenate([p_ref[pl.ds(s, tm, stride=n_sub), :] for s in range(n_sub)], axis=1)
    return pltpu.bitcast(p << 16, F32), pltpu.bitcast(p & jnp.uint32(0xFFFF0000), F32)


def _layer_norm(y, g, b):
    mu = jnp.mean(y, axis=-1, keepdims=True)
    d = y - mu
    var = jnp.mean(d * d, axis=-1, keepdims=True)
    return d * lax.rsqrt(var + LN_EPS) * g + b


def _qkv_kernel(x_ref, w_ref, cos_ref, sin_ref, o_ref, xb_ref, *, rope_blocks, q_blocks, q_scale):
    j = pl.program_id(1)

    @pl.when(j == 0)
    def _():
        xb_ref[...] = x_ref[...].astype(BF16)

    acc = jnp.dot(xb_ref[...], w_ref[...], preferred_element_type=F32)
    tn = acc.shape[1]

    @pl.when(j < rope_blocks)
    def _():
        cosf = cos_ref[...]
        sinf = sin_ref[...]
        lane = lax.broadcasted_iota(jnp.int32, (1, LANES), 1)
        scale = jnp.where(j < q_blocks, q_scale, 1.0).astype(F32)
        for c in range(tn // LANES):
            xc = acc[:, c * LANES:(c + 1) * LANES]
            partner = jnp.where(lane < ROPE_HALF,
                                pltpu.roll(xc, LANES - ROPE_HALF, 1),
                                pltpu.roll(xc, ROPE_HALF, 1))
            r = (xc * cosf + partner * sinf) * scale
            o_ref[:, c * LANES:(c + 1) * LANES] = r.astype(o_ref.dtype)

    @pl.when(j >= rope_blocks)
    def _():
        o_ref[...] = acc.astype(o_ref.dtype)


def _qkv_proj(x2d, w_bf16, cosf, sinf, *, rope_cols, q_cols, tm, tn):
    T, D = x2d.shape
    N = w_bf16.shape[1]
    kern = functools.partial(_qkv_kernel, rope_blocks=rope_cols // tn, q_blocks=q_cols // tn,
                             q_scale=HEAD_DIM ** -0.5)
    return pl.pallas_call(
        kern,
        grid=(T // tm, N // tn),
        in_specs=[
            pl.BlockSpec((tm, D), lambda i, j: (i, 0)),
            pl.BlockSpec((D, tn), lambda i, j: (0, j)),
            pl.BlockSpec((tm, LANES), lambda i, j: (i, 0)),
            pl.BlockSpec((tm, LANES), lambda i, j: (i, 0)),
        ],
        out_specs=pl.BlockSpec((tm, tn), lambda i, j: (i, j)),
        out_shape=jax.ShapeDtypeStruct((T, N), BF16),
        scratch_shapes=[pltpu.VMEM((tm, D), BF16)],
        compiler_params=_params("parallel", "arbitrary"),
        name="qkv_proj",
    )(x2d, w_bf16, cosf, sinf)


def _rope_tables(positions):
    inv_freq = ROPE_THETA ** (-jnp.arange(0, ROPE_DIMS, 2, dtype=F32) / ROPE_DIMS)
    ang = positions.astype(F32).reshape(-1)[:, None] * inv_freq
    cos, sin = jnp.cos(ang), jnp.sin(ang)
    T = ang.shape[0]
    rest = LANES - ROPE_DIMS
    cosf = jnp.concatenate([cos, cos, jnp.ones((T, rest), F32)], axis=1)
    sinf = jnp.concatenate([-sin, sin, jnp.zeros((T, rest), F32)], axis=1)
    return cosf, sinf


def _softmax_first(sT, v):
    m = jnp.max(sT, axis=0, keepdims=True)
    p = jnp.exp(sT - m)
    l = jnp.sum(p, axis=0, keepdims=True)
    acc = lax.dot_general(v, p.astype(BF16), _TN, preferred_element_type=F32)
    return m, l, acc


def _softmax_update(sTs, vs, m_ref, l_ref, acc_ref):
    m_prev = m_ref[...]
    m_new = m_prev
    for sT in sTs:
        m_new = jnp.maximum(m_new, jnp.max(sT, axis=0, keepdims=True))
    alpha = jnp.exp(m_prev - m_new)
    l = alpha * l_ref[...]
    pv = None
    for sT, v in zip(sTs, vs):
        p = jnp.exp(sT - m_new)
        l = l + jnp.sum(p, axis=0, keepdims=True)
        d = lax.dot_general(v, p.astype(BF16), _TN, preferred_element_type=F32)
        pv = d if pv is None else pv + d
    l_ref[...] = l
    acc_ref[...] = alpha * acc_ref[...] + pv
    m_ref[...] = m_new


def _causal_mask(sT, key0, qry0):
    tk, tq = sT.shape
    key = key0 + lax.broadcasted_iota(jnp.int32, (tk, tq), 0)
    qry = qry0 + lax.broadcasted_iota(jnp.int32, (tk, tq), 1)
    return jnp.where(key <= qry, sT, MASK_VALUE)


def _da_kernel(lam_ref, q_ref, k_ref, v_ref, g_ref, o_ref, m0, l0, acc0, m1, l1, acc1, s_a, s_b,
               *, tq, tk, out_scale):
    qi = pl.program_id(2)
    hd = HEAD_DIM
    gd = (qi * tq) // tk
    state = ((m0, l0, acc0), (m1, l1, acc1))

    def scores(g, c):
        koff = pl.multiple_of(g * tk, tk)
        kblk = k_ref[pl.ds(koff, tk), c * hd:(c + 1) * hd]
        return lax.dot_general(kblk, q_ref[:, c * hd:(c + 1) * hd], _NT, preferred_element_type=F32)

    def values(g):
        voff = pl.multiple_of(g * tk, tk)
        return v_ref[pl.ds(voff, tk), :]

    last_tile = jnp.maximum(gd - 1, 0)

    def fill(s_ref, g_req):
        g = jnp.minimum(g_req, last_tile)
        for c in range(2):
            s_ref[c] = scores(g, c)

    def consume(s_ref, g):
        v = values(g)
        for c in range(2):
            _softmax_update([s_ref[c]], [v], *state[c])

    fill(s_a, 0)

    v_diag = values(gd)
    for c in range(2):
        m, l, acc = _softmax_first(_causal_mask(scores(gd, c), gd * tk, qi * tq), v_diag)
        m_ref, l_ref, acc_ref = state[c]
        m_ref[...] = m
        l_ref[...] = l
        acc_ref[...] = acc

    def pair(p, carry):
        g0 = 2 * p
        fill(s_b, g0 + 1)
        consume(s_a, g0)
        fill(s_a, g0 + 2)
        consume(s_b, g0 + 1)
        return carry

    lax.fori_loop(0, gd // 2, pair, 0)

    @pl.when(lax.rem(gd, 2) == 1)
    def _():
        consume(s_a, last_tile)

    lam = lam_ref[0]
    oT = acc0[...] / l0[...] - lam * (acc1[...] / l1[...])
    o = oT.T
    o = o * lax.rsqrt(jnp.mean(o * o, axis=-1, keepdims=True) + LN_EPS) * g_ref[...]
    o_ref[...] = (o * out_scale).astype(o_ref.dtype)


def _diff_attention(qkv, lam, subln_g, *, B, S, H, tq, tk, lambda_init):
    T = B * S
    nq = S // tq
    dv = 2 * HEAD_DIM
    kern = functools.partial(_da_kernel, tq=tq, tk=tk, out_scale=1.0 - lambda_init)
    return pl.pallas_call(
        kern,
        grid=(B, H, nq),
        in_specs=[
            pl.BlockSpec(memory_space=pltpu.SMEM),
            pl.BlockSpec((tq, dv), lambda b, h, i: (b * nq + i, h)),
            pl.BlockSpec((S, dv), lambda b, h, i: (b, H + h)),
            pl.BlockSpec((S, dv), lambda b, h, i: (b, 2 * H + h)),
            pl.BlockSpec((1, dv), lambda b, h, i: (0, 0)),
        ],
        out_specs=pl.BlockSpec((tq, dv), lambda b, h, i: (b * nq + i, h)),
        out_shape=jax.ShapeDtypeStruct((T, H * dv), BF16),
        scratch_shapes=2 * [pltpu.VMEM((1, tq), F32), pltpu.VMEM((1, tq), F32), pltpu.VMEM((dv, tq), F32)]
        + 2 * [pltpu.VMEM((2, tk, tq), F32)],
        compiler_params=_params("parallel", "parallel", "arbitrary"),
        name="diff_attention",
    )(lam, qkv, qkv, qkv, subln_g)


def _mb_kernel(q_ref, k_ref, v_ref, o_ref, kmean_ref, bias_ref, m_ref, l_ref, acc_ref, s_a, s_b, *, nb, group):
    j = pl.program_id(2)
    blk = MB_BLOCK

    @pl.when(j == 0)
    def _():
        for n in range(nb):
            kb = k_ref[n * blk:(n + 1) * blk, :].astype(F32)
            kmean_ref[n:n + 1, :] = jnp.sum(kb, axis=0, keepdims=True) * (1.0 / blk)

    q = q_ref[...]
    km = kmean_ref[...]
    km_hi = km.astype(BF16)
    km_lo = (km - km_hi.astype(F32)).astype(BF16)
    gate = (lax.dot_general(km_hi, q, _NT, preferred_element_type=F32)
            + lax.dot_general(km_lo, q, _NT, preferred_element_type=F32))
    bid = lax.broadcasted_iota(jnp.int32, gate.shape, 0)
    rem = jnp.where(bid < j, gate, -jnp.inf)
    sel = jnp.zeros(gate.shape, jnp.bool_)
    for _ in range(MB_TOPK):
        mx = jnp.max(rem, axis=0, keepdims=True)
        first = jnp.min(jnp.where(rem == mx, bid, nb), axis=0, keepdims=True)
        pick = (bid == first) & (mx > -jnp.inf)
        sel = sel | pick
        rem = jnp.where(bid == first, -jnp.inf, rem)
    bias = jnp.where(sel, 0.0, MASK_VALUE).astype(F32)
    for gg in range(nb // group):
        bias_ref[gg] = bias[gg * group:(gg + 1) * group, :]

    def scores(n):
        koff = pl.multiple_of(n * blk, blk)
        return lax.dot_general(k_ref[pl.ds(koff, blk), :], q, _NT, preferred_element_type=F32)

    def values(n):
        voff = pl.multiple_of(n * blk, blk)
        return v_ref[pl.ds(voff, blk), :]

    n_groups = (j + group - 1) // group
    last_group = jnp.maximum(n_groups - 1, 0)

    def fill(s_ref, g_req):
        g = jnp.minimum(g_req, last_group)
        gbias = bias_ref[g]
        for t in range(group):
            s_ref[t] = scores(g * group + t) + gbias[t:t + 1, :]

    def consume(s_ref, g):
        _softmax_update([s_ref[t] for t in range(group)], [values(g * group + t) for t in range(group)],
                        m_ref, l_ref, acc_ref)

    fill(s_a, 0)

    m, l, acc = _softmax_first(_causal_mask(scores(j), 0, 0), values(j))
    m_ref[...] = m
    l_ref[...] = l
    acc_ref[...] = acc

    def pair(p, carry):
        g0 = 2 * p
        fill(s_b, g0 + 1)
        consume(s_a, g0)
        fill(s_a, g0 + 2)
        consume(s_b, g0 + 1)
        return carry

    lax.fori_loop(0, n_groups // 2, pair, 0)

    @pl.when(lax.rem(n_groups, 2) == 1)
    def _():
        consume(s_a, last_group)

    oT = acc_ref[...] / l_ref[...]
    o_ref[...] = oT.T.astype(o_ref.dtype)


def _moba_attention(qkv, *, B, S, H, group):
    T = B * S
    blk = MB_BLOCK
    nb = S // blk
    hd = HEAD_DIM
    kern = functools.partial(_mb_kernel, nb=nb, group=group)
    return pl.pallas_call(
        kern,
        grid=(B, H, nb),
        in_specs=[
            pl.BlockSpec((blk, hd), lambda b, h, i: (b * nb + i, h)),
            pl.BlockSpec((S, hd), lambda b, h, i: (b, H + h)),
            pl.BlockSpec((S, hd), lambda b, h, i: (b, 2 * H + h)),
        ],
        out_specs=pl.BlockSpec((blk, hd), lambda b, h, i: (b * nb + i, h)),
        out_shape=jax.ShapeDtypeStruct((T, H * hd), BF16),
        scratch_shapes=[
            pltpu.VMEM((nb, hd), F32),
            pltpu.VMEM((nb // group, group, blk), F32),
            pltpu.VMEM((1, blk), F32),
            pltpu.VMEM((1, blk), F32),
            pltpu.VMEM((hd, blk), F32),
            pltpu.VMEM((group, blk, blk), F32),
            pltpu.VMEM((group, blk, blk), F32),
        ],
        compiler_params=_params("parallel", "parallel", "arbitrary"),
        name="moba_attention",
    )(qkv, qkv, qkv)


def _router(x1, wr_ref, rb_ref, eidx_ref, wgt_ref):
    E = wr_ref.shape[1] // 2
    per_group = E // N_GROUPS
    tm = x1.shape[0]
    x_hi = x1.astype(BF16)
    x_lo = (x1 - x_hi.astype(F32)).astype(BF16)
    parts = (jnp.dot(x_hi, wr_ref[...], preferred_element_type=F32)
             + jnp.dot(x_lo, wr_ref[...], preferred_element_type=F32))
    logits = (parts[:, :E] + parts[:, E:]).T
    scores = _sigmoid(logits)
    biased = scores + rb_ref[...]

    gid8 = lax.broadcasted_iota(jnp.int32, (per_group, tm), 0)
    gscores = []
    for g in range(N_GROUPS):
        bg = biased[g * per_group:(g + 1) * per_group, :]
        m1 = jnp.max(bg, axis=0, keepdims=True)
        i1 = jnp.min(jnp.where(bg == m1, gid8, per_group), axis=0, keepdims=True)
        m2 = jnp.max(jnp.where(gid8 == i1, -jnp.inf, bg), axis=0, keepdims=True)
        gscores.append(m1 + m2)
    gs = jnp.concatenate(gscores, axis=0)

    gidx = lax.broadcasted_iota(jnp.int32, (N_GROUPS, tm), 0)
    rank = jnp.zeros((N_GROUPS, tm), jnp.int32)
    for g in range(N_GROUPS):
        row = gs[g:g + 1, :]
        beats = (row > gs) | ((row == gs) & (g < gidx))
        rank = rank + beats.astype(jnp.int32)
    gsel = rank < TOPK_GROUPS

    masked = jnp.concatenate(
        [jnp.where(gsel[g:g + 1, :], biased[g * per_group:(g + 1) * per_group, :], -jnp.inf)
         for g in range(N_GROUPS)], axis=0)

    eid = lax.broadcasted_iota(jnp.int32, (E, tm), 0)
    rem = masked
    ids, vals = [], []
    for _ in range(TOP_K):
        mx = jnp.max(rem, axis=0, keepdims=True)
        first = jnp.min(jnp.where(rem == mx, eid, E), axis=0, keepdims=True)
        onehot = eid == first
        ids.append(first)
        vals.append(jnp.sum(jnp.where(onehot, scores, 0.0), axis=0, keepdims=True))
        rem = jnp.where(onehot, -jnp.inf, rem)
    w = jnp.concatenate(vals, axis=0)
    w = w / jnp.sum(w, axis=0, keepdims=True) * ROUTED_SCALE
    eidx_ref[...] = jnp.concatenate(ids, axis=0)
    wgt_ref[...] = w


def _proj_ln_router_kernel(o_ref, w_ref, x_ref, g_ref, b_ref, wr_ref, rb_ref,
                           x1_ref, x1p_ref, eidx_ref, wgt_ref, *, alpha):
    h = jnp.dot(o_ref[...], w_ref[...], preferred_element_type=F32)
    x1 = _layer_norm(alpha * x_ref[...] + h, g_ref[...], b_ref[...])
    x1_ref[...] = x1
    _pack_rows(x1, x1p_ref)
    _router(x1, wr_ref, rb_ref, eidx_ref, wgt_ref)


def _proj_ln_router(o2d, w_out_bf16, x2d, g, b, wr, rb, *, alpha, tm):
    T, D = x2d.shape
    Do = o2d.shape[1]
    E = wr.shape[1] // 2
    kern = functools.partial(_proj_ln_router_kernel, alpha=alpha)
    const = lambda i: (0, 0)
    return pl.pallas_call(
        kern,
        grid=(T // tm,),
        in_specs=[
            pl.BlockSpec((tm, Do), lambda i: (i, 0)),
            pl.BlockSpec((Do, D), const),
            pl.BlockSpec((tm, D), lambda i: (i, 0)),
            pl.BlockSpec((1, D), const),
            pl.BlockSpec((1, D), const),
            pl.BlockSpec((D, 2 * E), const),
            pl.BlockSpec((E, 1), const),
        ],
        out_specs=[
            pl.BlockSpec((tm, D), lambda i: (i, 0)),
            pl.BlockSpec((tm * (D // (2 * LANES)), LANES), lambda i: (i, 0)),
            pl.BlockSpec((TOP_K, tm), lambda i: (0, i)),
            pl.BlockSpec((TOP_K, tm), lambda i: (0, i)),
        ],
        out_shape=[
            jax.ShapeDtypeStruct((T, D), F32),
            jax.ShapeDtypeStruct((T * (D // (2 * LANES)), LANES), jnp.uint32),
            jax.ShapeDtypeStruct((TOP_K, T), jnp.int32),
            jax.ShapeDtypeStruct((TOP_K, T), F32),
        ],
        compiler_params=_params("parallel"),
        name="outproj_ln_router",
    )(o2d, w_out_bf16, x2d, g, b, wr, rb)


def _routing_plan(eidxT, *, n_experts, tm):
    K, T = eidxT.shape
    n_slots = T * K
    n_tiles = n_slots // tm + n_experts
    e_flat = eidxT.reshape(-1)
    order = jnp.argsort(e_flat, stable=True).astype(jnp.int32)
    counts = jnp.sum((e_flat[:, None] == jnp.arange(n_experts, dtype=jnp.int32)[None, :]).astype(jnp.int32), axis=0)
    tiles_per_e = (counts + tm - 1) // tm
    tile_end = jnp.cumsum(tiles_per_e)
    tile_start = tile_end - tiles_per_e
    run_start = jnp.cumsum(counts) - counts
    total_tiles = tile_end[-1]

    tile_ids = jnp.arange(n_tiles, dtype=jnp.int32)
    tile_valid = tile_ids < total_tiles
    last_tile = jnp.maximum(total_tiles - 1, 0)
    tile_e = jnp.sum((jnp.minimum(tile_ids, last_tile)[:, None] >= tile_end[None, :]).astype(jnp.int32), axis=1)
    tile_e = jnp.minimum(tile_e, n_experts - 1)

    tile_off = (tile_ids - tile_start[tile_e]) * tm
    n_valid = jnp.where(tile_valid, jnp.clip(counts[tile_e] - tile_off, 0, tm), 0).astype(jnp.int32)
    r = jnp.arange(tm, dtype=jnp.int32)[None, :]
    src = jnp.clip((run_start[tile_e] + tile_off)[:, None] + r, 0, n_slots - 1)
    slots = jnp.where(r < n_valid[:, None], order[src], 0).astype(jnp.int32)
    return tile_e, n_valid, lax.rem(slots, T), slots


def _experts_kernel(te_ref, nv_ref, tok_hbm, slot_hbm, x_hbm, wg_ref, wu_ref, wd_ref, y_hbm,
                    tok_a, tok_b, slot_a, slot_b, xbuf, ybuf, wg_bf, wu_bf, wd_bf,
                    tok_sems, slot_sems, gather_sems, scatter_sems, *, n_sub):
    i = pl.program_id(0)
    last = pl.num_programs(0) - 1
    tm = xbuf.shape[1] // n_sub
    toks = (tok_a, tok_b)
    slots = (slot_a, slot_b)
    n_valid = nv_ref[i]

    @pl.when((i == 0) | (te_ref[i] != te_ref[jnp.maximum(i - 1, 0)]))
    def _():
        wg_bf[...] = wg_ref[...].astype(BF16)
        wu_bf[...] = wu_ref[...].astype(BF16)
        wd_bf[...] = wd_ref[...].astype(BF16)

    def tok_copy(tile, b):
        return pltpu.make_async_copy(tok_hbm.at[jnp.minimum(tile, last)], toks[b], tok_sems.at[b])

    def slot_copy(tile, b):
        return pltpu.make_async_copy(slot_hbm.at[jnp.minimum(tile, last)], slots[b], slot_sems.at[b])

    def start_gather(b):
        for r in range(tm):
            src = pl.multiple_of(toks[b][r], n_sub)
            pltpu.make_async_copy(x_hbm.at[pl.ds(src, n_sub), :], xbuf.at[b, pl.ds(r * n_sub, n_sub), :],
                                  gather_sems.at[b]).start(priority=r % 2)

    def wait_gather(b):
        pltpu.make_async_copy(x_hbm.at[pl.ds(0, tm * n_sub), :], xbuf.at[b], gather_sems.at[b]).wait()

    def start_scatter(b, n):
        def row(r):
            src = pl.multiple_of(r * n_sub, n_sub)
            dst = pl.multiple_of(slots[b][r], n_sub)
            pltpu.make_async_copy(ybuf.at[b, pl.ds(src, n_sub), :], y_hbm.at[pl.ds(dst, n_sub), :],
                                  scatter_sems.at[b]).start(priority=1)

        def group(g, carry):
            for u in range(ISSUE_UNROLL):
                row(g * ISSUE_UNROLL + u)
            return carry

        def single(r, carry):
            row(r)
            return carry

        n_groups = n // ISSUE_UNROLL
        lax.fori_loop(0, n_groups, group, 0)
        lax.fori_loop(n_groups * ISSUE_UNROLL, n, single, 0)

    def wait_scatter(b, n):
        @pl.when(n > 0)
        def _():
            rows = pl.multiple_of(n * n_sub, n_sub)
            pltpu.make_async_copy(ybuf.at[b, pl.ds(0, rows), :], y_hbm.at[pl.ds(0, rows), :],
                                  scatter_sems.at[b]).wait()

    def step(cur):
        nxt = 1 - cur

        @pl.when(i == 0)
        def _():
            first = tok_copy(0, cur)
            first.start()
            first.wait()
            start_gather(cur)
            tok_copy(1, nxt).start()
            slot_copy(0, cur).start()

        tok_copy(i + 2, cur).start()
        slot_copy(i + 1, nxt).start()
        wait_gather(cur)
        tok_copy(i + 1, nxt).wait()
        start_gather(nxt)

        x_lo, x_hi = _unpack_rows(xbuf.at[cur], n_sub)
        xb = jnp.concatenate([x_lo, x_hi], axis=1).astype(BF16)
        g = jnp.dot(xb, wg_bf[...], preferred_element_type=F32)
        u = jnp.dot(xb, wu_bf[...], preferred_element_type=F32)
        h = (g * _sigmoid(g) * u).astype(BF16)
        _pack_rows(jnp.dot(h, wd_bf[...], preferred_element_type=F32), ybuf.at[cur])

        @pl.when(i > 0)
        def _():
            wait_scatter(nxt, nv_ref[jnp.maximum(i - 1, 0)])

        slot_copy(i, cur).wait()
        start_scatter(cur, n_valid)

        @pl.when(i == last)
        def _():
            wait_scatter(cur, n_valid)
            wait_gather(nxt)
            tok_copy(i + 2, cur).wait()
            slot_copy(i + 1, nxt).wait()

    parity = lax.rem(i, 2)
    for b in range(2):
        pl.when(parity == b)(functools.partial(step, b))


def _routed_experts(x1p, tile_e, tile_rows, toks, slots, wg, wu, wd, *, tm):
    D = wg.shape[1]
    F = wg.shape[2]
    S = D // (2 * LANES)
    T = x1p.shape[0] // S
    n_tiles = slots.shape[0]
    grid_spec = pltpu.PrefetchScalarGridSpec(
        num_scalar_prefetch=2,
        grid=(n_tiles,),
        in_specs=[
            pl.BlockSpec(memory_space=pl.ANY),
            pl.BlockSpec(memory_space=pl.ANY),
            pl.BlockSpec(memory_space=pl.ANY),
            pl.BlockSpec((None, D, F), lambda i, te, nv: (te[i], 0, 0)),
            pl.BlockSpec((None, D, F), lambda i, te, nv: (te[i], 0, 0)),
            pl.BlockSpec((None, F, D), lambda i, te, nv: (te[i], 0, 0)),
        ],
        out_specs=pl.BlockSpec(memory_space=pl.ANY),
        scratch_shapes=[
            pltpu.SMEM((tm,), jnp.int32),
            pltpu.SMEM((tm,), jnp.int32),
            pltpu.SMEM((tm,), jnp.int32),
            pltpu.SMEM((tm,), jnp.int32),
            pltpu.VMEM((2, tm * S, LANES), jnp.uint32),
            pltpu.VMEM((2, tm * S, LANES), jnp.uint32),
            pltpu.VMEM((D, F), BF16),
            pltpu.VMEM((D, F), BF16),
            pltpu.VMEM((F, D), BF16),
            pltpu.SemaphoreType.DMA((2,)),
            pltpu.SemaphoreType.DMA((2,)),
            pltpu.SemaphoreType.DMA((2,)),
            pltpu.SemaphoreType.DMA((2,)),
        ],
    )
    return pl.pallas_call(
        functools.partial(_experts_kernel, n_sub=S),
        grid_spec=grid_spec,
        out_shape=jax.ShapeDtypeStruct((T * TOP_K * S, LANES), jnp.uint32),
        compiler_params=_params("arbitrary"),
        name="routed_experts",
    )(tile_e, tile_rows, toks * S, slots * S, x1p, wg, wu, wd)


def _combine_kernel(y_ref, w_ref, x1_ref, wsg_ref, wsu_ref, wsd_ref, g_ref, b_ref, out_ref, *, alpha):
    x1 = x1_ref[...]
    xb = x1.astype(BF16)
    gg = jnp.dot(xb, wsg_ref[...], preferred_element_type=F32)
    uu = jnp.dot(xb, wsu_ref[...], preferred_element_type=F32)
    hh = (gg * _sigmoid(gg) * uu).astype(BF16)
    shared = jnp.dot(hh, wsd_ref[...], preferred_element_type=F32)
    w = w_ref[...]
    routed = None
    for k in range(TOP_K):
        y_lo, y_hi = _unpack_rows(y_ref.at[k], y_ref.shape[1] // x1.shape[0])
        term = w[:, k:k + 1] * jnp.concatenate([y_lo, y_hi], axis=1)
        routed = term if routed is None else routed + term
    out_ref[...] = _layer_norm(alpha * x1 + (routed + shared), g_ref[...], b_ref[...])


def _combine(y_kt, w_tk, x1, wsg, wsu, wsd, g, b, *, alpha, tm):
    T, D = x1.shape
    Fs = wsg.shape[1]
    kern = functools.partial(_combine_kernel, alpha=alpha)
    const = lambda i: (0, 0)
    return pl.pallas_call(
        kern,
        grid=(T // tm,),
        in_specs=[
            pl.BlockSpec((TOP_K, tm * (D // (2 * LANES)), LANES), lambda i: (0, i, 0)),
            pl.BlockSpec((tm, TOP_K), lambda i: (i, 0)),
            pl.BlockSpec((tm, D), lambda i: (i, 0)),
            pl.BlockSpec((D, Fs), const),
            pl.BlockSpec((D, Fs), const),
            pl.BlockSpec((Fs, D), const),
            pl.BlockSpec((1, D), const),
            pl.BlockSpec((1, D), const),
        ],
        out_specs=pl.BlockSpec((tm, D), lambda i: (i, 0)),
        out_shape=jax.ShapeDtypeStruct((T, D), F32),
        compiler_params=_params("parallel"),
        name="shared_combine_ln",
    )(y_kt, w_tk, x1, wsg, wsu, wsd, g, b)


def _split_bf16(w):
    hi = w.astype(BF16)
    lo = (w - hi.astype(F32)).astype(BF16)
    return jnp.concatenate([hi, lo], axis=-1)


def _pick(n, pref):
    t = min(n, pref)
    while n % t:
        t //= 2
    return t


def _moe_layer(x_attn_in, o2d, w_out, ln1_g, ln1_b, w_router, router_bias, w_gate, w_up, w_down,
               ws_gate, ws_up, ws_down, ln2_g, ln2_b, *, alpha):
    T, D = x_attn_in.shape
    E = w_router.shape[1]
    x1, x1p, eidxT, wgtT = _proj_ln_router(
        o2d, w_out.astype(BF16), x_attn_in, ln1_g[None, :], ln1_b[None, :],
        _split_bf16(w_router), router_bias[:, None], alpha=alpha, tm=_pick(T, 256))
    tm_e = _pick(T * TOP_K, 256)
    tile_e, tile_rows, toks, slots = _routing_plan(eidxT, n_experts=E, tm=tm_e)
    y = _routed_experts(x1p, tile_e, tile_rows, toks, slots, w_gate, w_up, w_down, tm=tm_e)
    return _combine(y.reshape(TOP_K, T * (D // (2 * LANES)), LANES), wgtT.T, x1, ws_gate.astype(BF16), ws_up.astype(BF16), ws_down.astype(BF16),
                    ln2_g[None, :], ln2_b[None, :], alpha=alpha, tm=_pick(T, 128))


def kernel(x, positions, ln1_g, ln1_b, ln2_g, ln2_b, da_w_in, da_w_out, da_lq1, da_lk1, da_lq2, da_lk2,
           da_subln_g, mb_w_in, mb_w_out, w_router, router_bias, w_gate, w_up, w_down, ws_gate, ws_up, ws_down):
    B, S, D = x.shape
    T = B * S
    depth = ln1_g.shape[0]
    alpha = (2 * depth) ** 0.25
    hd = HEAD_DIM
    cosf, sinf = _rope_tables(positions)
    xt = x.reshape(T, D)
    tm_proj = _pick(T, 1024)

    for i in range(depth):
        m = i // 2
        if i % 2 == 0:
            w_in = da_w_in[m]
            H = w_in.shape[1] // (6 * hd)
            qk_cols = 4 * H * hd
            tq = _pick(S, 256)
            qkv = _qkv_proj(xt, w_in.astype(BF16), cosf, sinf, rope_cols=qk_cols, q_cols=qk_cols // 2,
                            tm=tm_proj, tn=_pick(qk_cols // 2, 512))
            tk = _pick(S, 1024)
            lambda_init = 0.8 - 0.6 * math.exp(-0.3 * i)
            lam = (jnp.exp(jnp.sum(da_lq1[m].astype(F32) * da_lk1[m].astype(F32)))
                   - jnp.exp(jnp.sum(da_lq2[m].astype(F32) * da_lk2[m].astype(F32))) + lambda_init)
            o2d = _diff_attention(qkv, lam.reshape(1), da_subln_g[m][None, :], B=B, S=S, H=H, tq=tq,
                                  tk=tk, lambda_init=lambda_init)
            w_out = da_w_out[m]
        else:
            w_in = mb_w_in[m]
            H = w_in.shape[1] // (3 * hd)
            qk_cols = 2 * H * hd
            qkv = _qkv_proj(xt, w_in.astype(BF16), cosf, sinf, rope_cols=qk_cols, q_cols=qk_cols // 2,
                            tm=tm_proj, tn=_pick(qk_cols // 2, 512))
            o2d = _moba_attention(qkv, B=B, S=S, H=H, group=_pick(S // MB_BLOCK, MB_GROUP))
            w_out = mb_w_out[m]
        xt = _moe_layer(xt, o2d, w_out, ln1_g[i], ln1_b[i], w_router[i], router_bias[i], w_gate[i], w_up[i],
                        w_down[i], ws_gate[i], ws_up[i], ws_down[i], ln2_g[i], ln2_b[i], alpha=alpha)
    return xt.reshape(B, S, D)
```

```python
import functools
import math

import jax
import jax.numpy as jnp
from jax import lax
from jax.experimental import pallas as pl
from jax.experimental.pallas import tpu as pltpu

F32 = jnp.float32
BF16 = jnp.bfloat16

HEAD_DIM = 128
ROPE_THETA = 500000.0
ROPE_DIMS = HEAD_DIM // 4
ROPE_HALF = ROPE_DIMS // 2
LN_EPS = 1e-5
MB_BLOCK = 256
MB_TOPK = 3
MB_GROUP = 4
TOP_K = 8
N_GROUPS = 8
TOPK_GROUPS = 4
ROUTED_SCALE = 2.5
MASK_VALUE = -1e30

LANES = 128
SUBLANES = 8
ISSUE_UNROLL = 8
VMEM_LIMIT_BYTES = 56 * 1024 * 1024

_NT = (((1,), (1,)), ((), ()))
_TN = (((0,), (0,)), ((), ()))


def _params(*sem):
    return pltpu.CompilerParams(dimension_semantics=sem, vmem_limit_bytes=VMEM_LIMIT_BYTES)


def _sigmoid(x):
    return 1.0 / (1.0 + jnp.exp(-x))


def _pack_rows(x, o_ref):
    tm, d = x.shape
    half = d // 2
    n_sub = half // LANES
    lo = pltpu.bitcast(x[:, :half].astype(BF16).astype(F32), jnp.uint32) >> 16
    hi = pltpu.bitcast(x[:, half:].astype(BF16).astype(F32), jnp.uint32) & jnp.uint32(0xFFFF0000)
    packed = hi | lo
    for s in range(n_sub):
        o_ref[pl.ds(s, tm, stride=n_sub), :] = packed[:, s * LANES:(s + 1) * LANES]


def _unpack_rows(p_ref, n_sub):
    tm = p_ref.shape[0] // n_sub
    p = jnp.concatenate([p_ref[pl.ds(s, tm, stride=n_sub), :] for s in range(n_sub)], axis=1)
    return pltpu.bitcast(p << 16, F32), pltpu.bitcast(p & jnp.uint32(0xFFFF0000), F32)


def _layer_norm(y, g, b):
    mu = jnp.mean(y, axis=-1, keepdims=True)
    d = y - mu
    var = jnp.mean(d * d, axis=-1, keepdims=True)
    return d * lax.rsqrt(var + LN_EPS) * g + b


def _qkv_kernel(x_ref, w_ref, cos_ref, sin_ref, o_ref, xb_ref, *, rope_blocks, q_blocks, q_scale):
    j = pl.program_id(1)

    @pl.when(j == 0)
    def _():
        xb_ref[...] = x_ref[...].astype(BF16)

    acc = jnp.dot(xb_ref[...], w_ref[...], preferred_element_type=F32)
    tn = acc.shape[1]

    @pl.when(j < rope_blocks)
    def _():
        cosf = cos_ref[...]
        sinf = sin_ref[...]
        lane = lax.broadcasted_iota(jnp.int32, (1, LANES), 1)
        scale = jnp.where(j < q_blocks, q_scale, 1.0).astype(F32)
        for c in range(tn // LANES):
            xc = acc[:, c * LANES:(c + 1) * LANES]
            partner = jnp.where(lane < ROPE_HALF,
                                pltpu.roll(xc, LANES - ROPE_HALF, 1),
                                pltpu.roll(xc, ROPE_HALF, 1))
            r = (xc * cosf + partner * sinf) * scale
            o_ref[:, c * LANES:(c + 1) * LANES] = r.astype(o_ref.dtype)

    @pl.when(j >= rope_blocks)
    def _():
        o_ref[...] = acc.astype(o_ref.dtype)


def _qkv_proj(x2d, w_bf16, cosf, sinf, *, rope_cols, q_cols, tm, tn):
    T, D = x2d.shape
    N = w_bf16.shape[1]
    kern = functools.partial(_qkv_kernel, rope_blocks=rope_cols // tn, q_blocks=q_cols // tn,
                             q_scale=HEAD_DIM ** -0.5)
    return pl.pallas_call(
        kern,
        grid=(T // tm, N // tn),
        in_specs=[
            pl.BlockSpec((tm, D), lambda i, j: (i, 0)),
            pl.BlockSpec((D, tn), lambda i, j: (0, j)),
            pl.BlockSpec((tm, LANES), lambda i, j: (i, 0)),
            pl.BlockSpec((tm, LANES), lambda i, j: (i, 0)),
        ],
        out_specs=pl.BlockSpec((tm, tn), lambda i, j: (i, j)),
        out_shape=jax.ShapeDtypeStruct((T, N), BF16),
        scratch_shapes=[pltpu.VMEM((tm, D), BF16)],
        compiler_params=_params("parallel", "arbitrary"),
        name="qkv_proj",
    )(x2d, w_bf16, cosf, sinf)


def _rope_tables(positions):
    inv_freq = ROPE_THETA ** (-jnp.arange(0, ROPE_DIMS, 2, dtype=F32) / ROPE_DIMS)
    ang = positions.astype(F32).reshape(-1)[:, None] * inv_freq
    cos, sin = jnp.cos(ang), jnp.sin(ang)
    T = ang.shape[0]
    rest = LANES - ROPE_DIMS
    cosf = jnp.concatenate([cos, cos, jnp.ones((T, rest), F32)], axis=1)
    sinf = jnp.concatenate([-sin, sin, jnp.zeros((T, rest), F32)], axis=1)
    return cosf, sinf


def _softmax_first(sT, v):
    m = jnp.max(sT, axis=0, keepdims=True)
    p = jnp.exp(sT - m)
    l = jnp.sum(p, axis=0, keepdims=True)
    acc = lax.dot_general(v, p.astype(BF16), _TN, preferred_element_type=F32)
    return m, l, acc


def _softmax_update(sTs, vs, m_ref, l_ref, acc_ref):
    m_prev = m_ref[...]
    m_new = m_prev
    for sT in sTs:
        m_new = jnp.maximum(m_new, jnp.max(sT, axis=0, keepdims=True))
    alpha = jnp.exp(m_prev - m_new)
    l = alpha * l_ref[...]
    pv = None
    for sT, v in zip(sTs, vs):
        p = jnp.exp(sT - m_new)
        l = l + jnp.sum(p, axis=0, keepdims=True)
        d = lax.dot_general(v, p.astype(BF16), _TN, preferred_element_type=F32)
        pv = d if pv is None else pv + d
    l_ref[...] = l
    acc_ref[...] = alpha * acc_ref[...] + pv
    m_ref[...] = m_new


def _causal_mask(sT, key0, qry0):
    tk, tq = sT.shape
    key = key0 + lax.broadcasted_iota(jnp.int32, (tk, tq), 0)
    qry = qry0 + lax.broadcasted_iota(jnp.int32, (tk, tq), 1)
    return jnp.where(key <= qry, sT, MASK_VALUE)


def _da_kernel(lam_ref, q_ref, k_ref, v_ref, g_ref, o_ref, m0, l0, acc0, m1, l1, acc1, s_a, s_b,
               *, tq, tk, out_scale):
    qi = pl.program_id(2)
    hd = HEAD_DIM
    gd = (qi * tq) // tk
    state = ((m0, l0, acc0), (m1, l1, acc1))

    def scores(g, c):
        koff = pl.multiple_of(g * tk, tk)
        kblk = k_ref[pl.ds(koff, tk), c * hd:(c + 1) * hd]
        return lax.dot_general(kblk, q_ref[:, c * hd:(c + 1) * hd], _NT, preferred_element_type=F32)

    def values(g):
        voff = pl.multiple_of(g * tk, tk)
        return v_ref[pl.ds(voff, tk), :]

    last_tile = jnp.maximum(gd - 1, 0)

    def fill(s_ref, g_req):
        g = jnp.minimum(g_req, last_tile)
        for c in range(2):
            s_ref[c] = scores(g, c)

    def consume(s_ref, g):
        v = values(g)
        for c in range(2):
            _softmax_update([s_ref[c]], [v], *state[c])

    fill(s_a, 0)

    v_diag = values(gd)
    for c in range(2):
        m, l, acc = _softmax_first(_causal_mask(scores(gd, c), gd * tk, qi * tq), v_diag)
        m_ref, l_ref, acc_ref = state[c]
        m_ref[...] = m
        l_ref[...] = l
        acc_ref[...] = acc

    def pair(p, carry):
        g0 = 2 * p
        fill(s_b, g0 + 1)
        consume(s_a, g0)
        fill(s_a, g0 + 2)
        consume(s_b, g0 + 1)
        return carry

    lax.fori_loop(0, gd // 2, pair, 0)

    @pl.when(lax.rem(gd, 2) == 1)
    def _():
        consume(s_a, last_tile)

    lam = lam_ref[0]
    oT = acc0[...] / l0[...] - lam * (acc1[...] / l1[...])
    o = oT.T
    o = o * lax.rsqrt(jnp.mean(o * o, axis=-1, keepdims=True) + LN_EPS) * g_ref[...]
    o_ref[...] = (o * out_scale).astype(o_ref.dtype)


def _diff_attention(qkv, lam, subln_g, *, B, S, H, tq, tk, lambda_init):
    T = B * S
    nq = S // tq
    dv = 2 * HEAD_DIM
    kern = functools.partial(_da_kernel, tq=tq, tk=tk, out_scale=1.0 - lambda_init)
    return pl.pallas_call(
        kern,
        grid=(B, H, nq),
        in_specs=[
            pl.BlockSpec(memory_space=pltpu.SMEM),
            pl.BlockSpec((tq, dv), lambda b, h, i: (b * nq + i, h)),
            pl.BlockSpec((S, dv), lambda b, h, i: (b, H + h)),
            pl.BlockSpec((S, dv), lambda b, h, i: (b, 2 * H + h)),
            pl.BlockSpec((1, dv), lambda b, h, i: (0, 0)),
        ],
        out_specs=pl.BlockSpec((tq, dv), lambda b, h, i: (b * nq + i, h)),
        out_shape=jax.ShapeDtypeStruct((T, H * dv), BF16),
        scratch_shapes=2 * [pltpu.VMEM((1, tq), F32), pltpu.VMEM((1, tq), F32), pltpu.VMEM((dv, tq), F32)]
        + 2 * [pltpu.VMEM((2, tk, tq), F32)],
        compiler_params=_params("parallel", "parallel", "arbitrary"),
        name="diff_attention",
    )(lam, qkv, qkv, qkv, subln_g)


def _mb_kernel(q_ref, k_ref, v_ref, o_ref, kmean_ref, bias_ref, m_ref, l_ref, acc_ref, s_a, s_b, *, nb, group):
    j = pl.program_id(2)
    blk = MB_BLOCK

    @pl.when(j == 0)
    def _():
        for n in range(nb):
            kb = k_ref[n * blk:(n + 1) * blk, :].astype(F32)
            kmean_ref[n:n + 1, :] = jnp.sum(kb, axis=0, keepdims=True) * (1.0 / blk)

    q = q_ref[...]
    km = kmean_ref[...]
    km_hi = km.astype(BF16)
    km_lo = (km - km_hi.astype(F32)).astype(BF16)
    gate = (lax.dot_general(km_hi, q, _NT, preferred_element_type=F32)
            + lax.dot_general(km_lo, q, _NT, preferred_element_type=F32))
    bid = lax.broadcasted_iota(jnp.int32, gate.shape, 0)
    rem = jnp.where(bid < j, gate, -jnp.inf)
    sel = jnp.zeros(gate.shape, jnp.bool_)
    for _ in range(MB_TOPK):
        mx = jnp.max(rem, axis=0, keepdims=True)
        first = jnp.min(jnp.where(rem == mx, bid, nb), axis=0, keepdims=True)
        pick = (bid == first) & (mx > -jnp.inf)
        sel = sel | pick
        rem = jnp.where(bid == first, -jnp.inf, rem)
    bias = jnp.where(sel, 0.0, MASK_VALUE).astype(F32)
    for gg in range(nb // group):
        bias_ref[gg] = bias[gg * group:(gg + 1) * group, :]

    def scores(n):
        koff = pl.multiple_of(n * blk, blk)
        return lax.dot_general(k_ref[pl.ds(koff, blk), :], q, _NT, preferred_element_type=F32)

    def values(n):
        voff = pl.multiple_of(n * blk, blk)
        return v_ref[pl.ds(voff, blk), :]

    n_groups = (j + group - 1) // group
    last_group = jnp.maximum(n_groups - 1, 0)

    def fill(s_ref, g_req):
        g = jnp.minimum(g_req, last_group)
        gbias = bias_ref[g]
        for t in range(group):
            s_ref[t] = scores(g * group + t) + gbias[t:t + 1, :]

    def consume(s_ref, g):
        _softmax_update([s_ref[t] for t in range(group)], [values(g * group + t) for t in range(group)],
                        m_ref, l_ref, acc_ref)

    fill(s_a, 0)

    m, l, acc = _softmax_first(_causal_mask(scores(j), 0, 0), values(j))
    m_ref[...] = m
    l_ref[...] = l
    acc_ref[...] = acc

    def pair(p, carry):
        g0 = 2 * p
        fill(s_b, g0 + 1)
        consume(s_a, g0)
        fill(s_a, g0 + 2)
        consume(s_b, g0 + 1)
        return carry

    lax.fori_loop(0, n_groups // 2, pair, 0)

    @pl.when(lax.rem(n_groups, 2) == 1)
    def _():
        consume(s_a, last_group)

    oT = acc_ref[...] / l_ref[...]
    o_ref[...] = oT.T.astype(o_ref.dtype)


def _moba_attention(qkv, *, B, S, H, group):
    T = B * S
    blk = MB_BLOCK
    nb = S // blk
    hd = HEAD_DIM
    kern = functools.partial(_mb_kernel, nb=nb, group=group)
    return pl.pallas_call(
        kern,
        grid=(B, H, nb),
        in_specs=[
            pl.BlockSpec((blk, hd), lambda b, h, i: (b * nb + i, h)),
            pl.BlockSpec((S, hd), lambda b, h, i: (b, H + h)),
            pl.BlockSpec((S, hd), lambda b, h, i: (b, 2 * H + h)),
        ],
        out_specs=pl.BlockSpec((blk, hd), lambda b, h, i: (b * nb + i, h)),
        out_shape=jax.ShapeDtypeStruct((T, H * hd), BF16),
        scratch_shapes=[
            pltpu.VMEM((nb, hd), F32),
            pltpu.VMEM((nb // group, group, blk), F32),
            pltpu.VMEM((1, blk), F32),
            pltpu.VMEM((1, blk), F32),
            pltpu.VMEM((hd, blk), F32),
            pltpu.VMEM((group, blk, blk), F32),
            pltpu.VMEM((group, blk, blk), F32),
        ],
        compiler_params=_params("parallel", "parallel", "arbitrary"),
        name="moba_attention",
    )(qkv, qkv, qkv)


def _router(x1, wr_ref, rb_ref, eidx_ref, wgt_ref):
    E = wr_ref.shape[1] // 2
    per_group = E // N_GROUPS
    tm = x1.shape[0]
    x_hi = x1.astype(BF16)
    x_lo = (x1 - x_hi.astype(F32)).astype(BF16)
    parts = (jnp.dot(x_hi, wr_ref[...], preferred_element_type=F32)
             + jnp.dot(x_lo, wr_ref[...], preferred_element_type=F32))
    logits = (parts[:, :E] + parts[:, E:]).T
    scores = _sigmoid(logits)
    biased = scores + rb_ref[...]

    gid8 = lax.broadcasted_iota(jnp.int32, (per_group, tm), 0)
    gscores = []
    for g in range(N_GROUPS):
        bg = biased[g * per_group:(g + 1) * per_group, :]
        m1 = jnp.max(bg, axis=0, keepdims=True)
        i1 = jnp.min(jnp.where(bg == m1, gid8, per_group), axis=0, keepdims=True)
        m2 = jnp.max(jnp.where(gid8 == i1, -jnp.inf, bg), axis=0, keepdims=True)
        gscores.append(m1 + m2)
    gs = jnp.concatenate(gscores, axis=0)

    gidx = lax.broadcasted_iota(jnp.int32, (N_GROUPS, tm), 0)
    rank = jnp.zeros((N_GROUPS, tm), jnp.int32)
    for g in range(N_GROUPS):
        row = gs[g:g + 1, :]
        beats = (row > gs) | ((row == gs) & (g < gidx))
        rank = rank + beats.astype(jnp.int32)
    gsel = rank < TOPK_GROUPS

    masked = jnp.concatenate(
        [jnp.where(gsel[g:g + 1, :], biased[g * per_group:(g + 1) * per_group, :], -jnp.inf)
         for g in range(N_GROUPS)], axis=0)

    eid = lax.broadcasted_iota(jnp.int32, (E, tm), 0)
    rem = masked
    ids, vals = [], []
    for _ in range(TOP_K):
        mx = jnp.max(rem, axis=0, keepdims=True)
        first = jnp.min(jnp.where(rem == mx, eid, E), axis=0, keepdims=True)
        onehot = eid == first
        ids.append(first)
        vals.append(jnp.sum(jnp.where(onehot, scores, 0.0), axis=0, keepdims=True))
        rem = jnp.where(onehot, -jnp.inf, rem)
    w = jnp.concatenate(vals, axis=0)
    w = w / jnp.sum(w, axis=0, keepdims=True) * ROUTED_SCALE
    eidx_ref[...] = jnp.concatenate(ids, axis=0)
    wgt_ref[...] = w


def _proj_ln_router_kernel(o_ref, w_ref, x_ref, g_ref, b_ref, wr_ref, rb_ref,
                           x1_ref, x1p_ref, eidx_ref, wgt_ref, *, alpha):
    h = jnp.dot(o_ref[...], w_ref[...], preferred_element_type=F32)
    x1 = _layer_norm(alpha * x_ref[...] + h, g_ref[...], b_ref[...])
    x1_ref[...] = x1
    _pack_rows(x1, x1p_ref)
    _router(x1, wr_ref, rb_ref, eidx_ref, wgt_ref)


def _proj_ln_router(o2d, w_out_bf16, x2d, g, b, wr, rb, *, alpha, tm):
    T, D = x2d.shape
    Do = o2d.shape[1]
    E = wr.shape[1] // 2
    kern = functools.partial(_proj_ln_router_kernel, alpha=alpha)
    const = lambda i: (0, 0)
    return pl.pallas_call(
        kern,
        grid=(T // tm,),
        in_specs=[
            pl.BlockSpec((tm, Do), lambda i: (i, 0)),
            pl.BlockSpec((Do, D), const),
            pl.BlockSpec((tm, D), lambda i: (i, 0)),
            pl.BlockSpec((1, D), const),
            pl.BlockSpec((1, D), const),
            pl.BlockSpec((D, 2 * E), const),
            pl.BlockSpec((E, 1), const),
        ],
        out_specs=[
            pl.BlockSpec((tm, D), lambda i: (i, 0)),
            pl.BlockSpec((tm * (D // (2 * LANES)), LANES), lambda i: (i, 0)),
            pl.BlockSpec((TOP_K, tm), lambda i: (0, i)),
            pl.BlockSpec((TOP_K, tm), lambda i: (0, i)),
        ],
        out_shape=[
            jax.ShapeDtypeStruct((T, D), F32),
            jax.ShapeDtypeStruct((T * (D // (2 * LANES)), LANES), jnp.uint32),
            jax.ShapeDtypeStruct((TOP_K, T), jnp.int32),
            jax.ShapeDtypeStruct((TOP_K, T), F32),
        ],
        compiler_params=_params("parallel"),
        name="outproj_ln_router",
    )(o2d, w_out_bf16, x2d, g, b, wr, rb)


def _routing_plan(eidxT, *, n_experts, tm):
    K, T = eidxT.shape
    n_slots = T * K
    n_tiles = n_slots // tm + n_experts
    e_flat = eidxT.reshape(-1)
    order = jnp.argsort(e_flat, stable=True).astype(jnp.int32)
    counts = jnp.sum((e_flat[:, None] == jnp.arange(n_experts, dtype=jnp.int32)[None, :]).astype(jnp.int32), axis=0)
    tiles_per_e = (counts + tm - 1) // tm
    tile_end = jnp.cumsum(tiles_per_e)
    tile_start = tile_end - tiles_per_e
    run_start = jnp.cumsum(counts) - counts
    total_tiles = tile_end[-1]

    tile_ids = jnp.arange(n_tiles, dtype=jnp.int32)
    tile_valid = tile_ids < total_tiles
    last_tile = jnp.maximum(total_tiles - 1, 0)
    tile_e = jnp.sum((jnp.minimum(tile_ids, last_tile)[:, None] >= tile_end[None, :]).astype(jnp.int32), axis=1)
    tile_e = jnp.minimum(tile_e, n_experts - 1)

    tile_off = (tile_ids - tile_start[tile_e]) * tm
    n_valid = jnp.where(tile_valid, jnp.clip(counts[tile_e] - tile_off, 0, tm), 0).astype(jnp.int32)
    r = jnp.arange(tm, dtype=jnp.int32)[None, :]
    src = jnp.clip((run_start[tile_e] + tile_off)[:, None] + r, 0, n_slots - 1)
    slots = jnp.where(r < n_valid[:, None], order[src], 0).astype(jnp.int32)
    return tile_e, n_valid, lax.rem(slots, T), slots


def _experts_kernel(te_ref, nv_ref, tok_hbm, slot_hbm, x_hbm, wg_ref, wu_ref, wd_ref, y_hbm,
                    tok_a, tok_b, slot_a, slot_b, xbuf, ybuf, wg_bf, wu_bf, wd_bf,
                    tok_sems, slot_sems, gather_sems, scatter_sems, *, n_sub):
    i = pl.program_id(0)
    last = pl.num_programs(0) - 1
    tm = xbuf.shape[1] // n_sub
    toks = (tok_a, tok_b)
    slots = (slot_a, slot_b)
    n_valid = nv_ref[i]

    @pl.when((i == 0) | (te_ref[i] != te_ref[jnp.maximum(i - 1, 0)]))
    def _():
        wg_bf[...] = wg_ref[...].astype(BF16)
        wu_bf[...] = wu_ref[...].astype(BF16)
        wd_bf[...] = wd_ref[...].astype(BF16)

    def tok_copy(tile, b):
        return pltpu.make_async_copy(tok_hbm.at[jnp.minimum(tile, last)], toks[b], tok_sems.at[b])

    def slot_copy(tile, b):
        return pltpu.make_async_copy(slot_hbm.at[jnp.minimum(tile, last)], slots[b], slot_sems.at[b])

    def start_gather(b):
        for r in range(tm):
            src = pl.multiple_of(toks[b][r], n_sub)
            pltpu.make_async_copy(x_hbm.at[pl.ds(src, n_sub), :], xbuf.at[b, pl.ds(r * n_sub, n_sub), :],
                                  gather_sems.at[b]).start(priority=r % 2)

    def wait_gather(b):
        pltpu.make_async_copy(x_hbm.at[pl.ds(0, tm * n_sub), :], xbuf.at[b], gather_sems.at[b]).wait()

    def start_scatter(b, n):
        def row(r):
            src = pl.multiple_of(r * n_sub, n_sub)
            dst = pl.multiple_of(slots[b][r], n_sub)
            pltpu.make_async_copy(ybuf.at[b, pl.ds(src, n_sub), :], y_hbm.at[pl.ds(dst, n_sub), :],
                                  scatter_sems.at[b]).start(priority=1)

        def group(g, carry):
            for u in range(ISSUE_UNROLL):
                row(g * ISSUE_UNROLL + u)
            return carry

        def single(r, carry):
            row(r)
            return carry

        n_groups = n // ISSUE_UNROLL
        lax.fori_loop(0, n_groups, group, 0)
        lax.fori_loop(n_groups * ISSUE_UNROLL, n, single, 0)

    def wait_scatter(b, n):
        @pl.when(n > 0)
        def _():
            rows = pl.multiple_of(n * n_sub, n_sub)
            pltpu.make_async_copy(ybuf.at[b, pl.ds(0, rows), :], y_hbm.at[pl.ds(0, rows), :],
                                  scatter_sems.at[b]).wait()

    def step(cur):
        nxt = 1 - cur

        @pl.when(i == 0)
        def _():
            first = tok_copy(0, cur)
            first.start()
            first.wait()
            start_gather(cur)
            tok_copy(1, nxt).start()
            slot_copy(0, cur).start()

        tok_copy(i + 2, cur).start()
        slot_copy(i + 1, nxt).start()
        wait_gather(cur)
        tok_copy(i + 1, nxt).wait()
        start_gather(nxt)

        x_lo, x_hi = _unpack_rows(xbuf.at[cur], n_sub)
        xb = jnp.concatenate([x_lo, x_hi], axis=1).astype(BF16)
        g = jnp.dot(xb, wg_bf[...], preferred_element_type=F32)
        u = jnp.dot(xb, wu_bf[...], preferred_element_type=F32)
        h = (g * _sigmoid(g) * u).astype(BF16)
        _pack_rows(jnp.dot(h, wd_bf[...], preferred_element_type=F32), ybuf.at[cur])

        @pl.when(i > 0)
        def _():
            wait_scatter(nxt, nv_ref[jnp.maximum(i - 1, 0)])

        slot_copy(i, cur).wait()
        start_scatter(cur, n_valid)

        @pl.when(i == last)
        def _():
            wait_scatter(cur, n_valid)
            wait_gather(nxt)
            tok_copy(i + 2, cur).wait()
            slot_copy(i + 1, nxt).wait()

    parity = lax.rem(i, 2)
    for b in range(2):
        pl.when(parity == b)(functools.partial(step, b))


def _routed_experts(x1p, tile_e, tile_rows, toks, slots, wg, wu, wd, *, layer, tm):
    D = wg.shape[2]
    F = wg.shape[3]
    S = D // (2 * LANES)
    T = x1p.shape[0] // S
    n_tiles = slots.shape[0]
    grid_spec = pltpu.PrefetchScalarGridSpec(
        num_scalar_prefetch=2,
        grid=(n_tiles,),
        in_specs=[
            pl.BlockSpec(memory_space=pl.ANY),
            pl.BlockSpec(memory_space=pl.ANY),
            pl.BlockSpec(memory_space=pl.ANY),
            pl.BlockSpec((None, None, D, F), lambda i, te, nv: (layer, te[i], 0, 0)),
            pl.BlockSpec((None, None, D, F), lambda i, te, nv: (layer, te[i], 0, 0)),
            pl.BlockSpec((None, None, F, D), lambda i, te, nv: (layer, te[i], 0, 0)),
        ],
        out_specs=pl.BlockSpec(memory_space=pl.ANY),
        scratch_shapes=[
            pltpu.SMEM((tm,), jnp.int32),
            pltpu.SMEM((tm,), jnp.int32),
            pltpu.SMEM((tm,), jnp.int32),
            pltpu.SMEM((tm,), jnp.int32),
            pltpu.VMEM((2, tm * S, LANES), jnp.uint32),
            pltpu.VMEM((2, tm * S, LANES), jnp.uint32),
            pltpu.VMEM((D, F), BF16),
            pltpu.VMEM((D, F), BF16),
            pltpu.VMEM((F, D), BF16),
            pltpu.SemaphoreType.DMA((2,)),
            pltpu.SemaphoreType.DMA((2,)),
            pltpu.SemaphoreType.DMA((2,)),
            pltpu.SemaphoreType.DMA((2,)),
        ],
    )
    return pl.pallas_call(
        functools.partial(_experts_kernel, n_sub=S),
        grid_spec=grid_spec,
        out_shape=jax.ShapeDtypeStruct((T * TOP_K * S, LANES), jnp.uint32),
        compiler_params=_params("arbitrary"),
        name="routed_experts",
    )(tile_e, tile_rows, toks * S, slots * S, x1p, wg, wu, wd)


def _combine_kernel(y_ref, w_ref, x1_ref, wsg_ref, wsu_ref, wsd_ref, g_ref, b_ref, out_ref, *, alpha):
    x1 = x1_ref[...]
    xb = x1.astype(BF16)
    gg = jnp.dot(xb, wsg_ref[...], preferred_element_type=F32)
    uu = jnp.dot(xb, wsu_ref[...], preferred_element_type=F32)
    hh = (gg * _sigmoid(gg) * uu).astype(BF16)
    shared = jnp.dot(hh, wsd_ref[...], preferred_element_type=F32)
    w = w_ref[...]
    routed = None
    for k in range(TOP_K):
        y_lo, y_hi = _unpack_rows(y_ref.at[k], y_ref.shape[1] // x1.shape[0])
        term = w[:, k:k + 1] * jnp.concatenate([y_lo, y_hi], axis=1)
        routed = term if routed is None else routed + term
    out_ref[...] = _layer_norm(alpha * x1 + (routed + shared), g_ref[...], b_ref[...])


def _combine(y_kt, w_tk, x1, wsg, wsu, wsd, g, b, *, alpha, tm):
    T, D = x1.shape
    Fs = wsg.shape[1]
    kern = functools.partial(_combine_kernel, alpha=alpha)
    const = lambda i: (0, 0)
    return pl.pallas_call(
        kern,
        grid=(T // tm,),
        in_specs=[
            pl.BlockSpec((TOP_K, tm * (D // (2 * LANES)), LANES), lambda i: (0, i, 0)),
            pl.BlockSpec((tm, TOP_K), lambda i: (i, 0)),
            pl.BlockSpec((tm, D), lambda i: (i, 0)),
            pl.BlockSpec((D, Fs), const),
            pl.BlockSpec((D, Fs), const),
            pl.BlockSpec((Fs, D), const),
            pl.BlockSpec((1, D), const),
            pl.BlockSpec((1, D), const),
        ],
        out_specs=pl.BlockSpec((tm, D), lambda i: (i, 0)),
        out_shape=jax.ShapeDtypeStruct((T, D), F32),
        compiler_params=_params("parallel"),
        name="shared_combine_ln",
    )(y_kt, w_tk, x1, wsg, wsu, wsd, g, b)


def _split_bf16(w):
    hi = w.astype(BF16)
    lo = (w - hi.astype(F32)).astype(BF16)
    return jnp.concatenate([hi, lo], axis=-1)


def _pick(n, pref):
    t = min(n, pref)
    while n % t:
        t //= 2
    return t


def _moe_layer(x_attn_in, o2d, w_out, ln1_g, ln1_b, w_router, router_bias, w_gate, w_up, w_down,
               ws_gate, ws_up, ws_down, ln2_g, ln2_b, *, alpha, layer):
    T, D = x_attn_in.shape
    E = w_router.shape[1]
    x1, x1p, eidxT, wgtT = _proj_ln_router(
        o2d, w_out.astype(BF16), x_attn_in, ln1_g[None, :], ln1_b[None, :],
        _split_bf16(w_router), router_bias[:, None], alpha=alpha, tm=_pick(T, 256))
    tm_e = _pick(T * TOP_K, 256)
    tile_e, tile_rows, toks, slots = _routing_plan(eidxT, n_experts=E, tm=tm_e)
    y = _routed_experts(x1p, tile_e, tile_rows, toks, slots, w_gate, w_up, w_down, layer=layer, tm=tm_e)
    return _combine(y.reshape(TOP_K, T * (D // (2 * LANES)), LANES), wgtT.T, x1, ws_gate.astype(BF16), ws_up.astype(BF16), ws_down.astype(BF16),
                    ln2_g[None, :], ln2_b[None, :], alpha=alpha, tm=_pick(T, 128))


def kernel(x, positions, ln1_g, ln1_b, ln2_g, ln2_b, da_w_in, da_w_out, da_lq1, da_lk1, da_lq2, da_lk2,
           da_subln_g, mb_w_in, mb_w_out, w_router, router_bias, w_gate, w_up, w_down, ws_gate, ws_up, ws_down):
    B, S, D = x.shape
    T = B * S
    depth = ln1_g.shape[0]
    alpha = (2 * depth) ** 0.25
    hd = HEAD_DIM
    cosf, sinf = _rope_tables(positions)
    xt = x.reshape(T, D)
    tm_proj = _pick(T, 1024)

    for i in range(depth):
        m = i // 2
        if i % 2 == 0:
            w_in = da_w_in[m]
            H = w_in.shape[1] // (6 * hd)
            qk_cols = 4 * H * hd
            tq = _pick(S, 256)
            qkv = _qkv_proj(xt, w_in.astype(BF16), cosf, sinf, rope_cols=qk_cols, q_cols=qk_cols // 2,
                            tm=tm_proj, tn=_pick(qk_cols // 2, 512))
            tk = _pick(S, 1024)
            lambda_init = 0.8 - 0.6 * math.exp(-0.3 * i)
            lam = (jnp.exp(jnp.sum(da_lq1[m].astype(F32) * da_lk1[m].astype(F32)))
                   - jnp.exp(jnp.sum(da_lq2[m].astype(F32) * da_lk2[m].astype(F32))) + lambda_init)
            o2d = _diff_attention(qkv, lam.reshape(1), da_subln_g[m][None, :], B=B, S=S, H=H, tq=tq,
                                  tk=tk, lambda_init=lambda_init)
            w_out = da_w_out[m]
        else:
            w_in = mb_w_in[m]
            H = w_in.shape[1] // (3 * hd)
            qk_cols = 2 * H * hd
            qkv = _qkv_proj(xt, w_in.astype(BF16), cosf, sinf, rope_cols=qk_cols, q_cols=qk_cols // 2,
                            tm=tm_proj, tn=_pick(qk_cols // 2, 512))
            o2d = _moba_attention(qkv, B=B, S=S, H=H, group=_pick(S // MB_BLOCK, MB_GROUP))
            w_out = mb_w_out[m]
        xt = _moe_layer(xt, o2d, w_out, ln1_g[i], ln1_b[i], w_router[i], router_bias[i], w_gate, w_up,
                        w_down, ws_gate[i], ws_up[i], ws_down[i], ln2_g[i], ln2_b[i], alpha=alpha, layer=i)
    return xt.reshape(B, S, D)
```

```python
import functools
import math

import jax
import jax.numpy as jnp
from jax import lax
from jax.experimental import pallas as pl
from jax.experimental.pallas import tpu as pltpu

F32 = jnp.float32
BF16 = jnp.bfloat16

HEAD_DIM = 128
ROPE_THETA = 500000.0
ROPE_DIMS = HEAD_DIM // 4
ROPE_HALF = ROPE_DIMS // 2
LN_EPS = 1e-5
MB_BLOCK = 256
MB_TOPK = 3
MB_GROUP = 4
TOP_K = 8
N_GROUPS = 8
TOPK_GROUPS = 4
ROUTED_SCALE = 2.5
MASK_VALUE = -1e30

LANES = 128
SUBLANES = 8
ISSUE_UNROLL = 8
VMEM_LIMIT_BYTES = 56 * 1024 * 1024

_NT = (((1,), (1,)), ((), ()))
_TN = (((0,), (0,)), ((), ()))


def _params(*sem):
    return pltpu.CompilerParams(dimension_semantics=sem, vmem_limit_bytes=VMEM_LIMIT_BYTES)


def _sigmoid(x):
    return 1.0 / (1.0 + jnp.exp(-x))


def _pack_rows(x, o_ref):
    tm, d = x.shape
    half = d // 2
    n_sub = half // LANES
    lo = pltpu.bitcast(x[:, :half].astype(BF16).astype(F32), jnp.uint32) >> 16
    hi = pltpu.bitcast(x[:, half:].astype(BF16).astype(F32), jnp.uint32) & jnp.uint32(0xFFFF0000)
    packed = hi | lo
    for s in range(n_sub):
        o_ref[pl.ds(s, tm, stride=n_sub), :] = packed[:, s * LANES:(s + 1) * LANES]


def _unpack_rows(p_ref, n_sub):
    tm = p_ref.shape[0] // n_sub
    p = jnp.concatenate([p_ref[pl.ds(s, tm, stride=n_sub), :] for s in range(n_sub)], axis=1)
    return pltpu.bitcast(p << 16, F32), pltpu.bitcast(p & jnp.uint32(0xFFFF0000), F32)


def _layer_norm(y, g, b):
    mu = jnp.mean(y, axis=-1, keepdims=True)
    d = y - mu
    var = jnp.mean(d * d, axis=-1, keepdims=True)
    return d * lax.rsqrt(var + LN_EPS) * g + b


def _qkv_kernel(x_ref, w_ref, cos_ref, sin_ref, o_ref, xb_ref, *, rope_blocks, q_blocks, q_scale):
    j = pl.program_id(1)

    @pl.when(j == 0)
    def _():
        xb_ref[...] = x_ref[...].astype(BF16)

    acc = jnp.dot(xb_ref[...], w_ref[...], preferred_element_type=F32)
    tn = acc.shape[1]

    @pl.when(j < rope_blocks)
    def _():
        cosf = cos_ref[...]
        sinf = sin_ref[...]
        lane = lax.broadcasted_iota(jnp.int32, (1, LANES), 1)
        scale = jnp.where(j < q_blocks, q_scale, 1.0).astype(F32)
        for c in range(tn // LANES):
            xc = acc[:, c * LANES:(c + 1) * LANES]
            partner = jnp.where(lane < ROPE_HALF,
                                pltpu.roll(xc, LANES - ROPE_HALF, 1),
                                pltpu.roll(xc, ROPE_HALF, 1))
            r = (xc * cosf + partner * sinf) * scale
            o_ref[:, c * LANES:(c + 1) * LANES] = r.astype(o_ref.dtype)

    @pl.when(j >= rope_blocks)
    def _():
        o_ref[...] = acc.astype(o_ref.dtype)


def _qkv_proj(x2d, w_bf16, cosf, sinf, *, rope_cols, q_cols, tm, tn):
    T, D = x2d.shape
    N = w_bf16.shape[1]
    kern = functools.partial(_qkv_kernel, rope_blocks=rope_cols // tn, q_blocks=q_cols // tn,
                             q_scale=HEAD_DIM ** -0.5)
    return pl.pallas_call(
        kern,
        grid=(T // tm, N // tn),
        in_specs=[
            pl.BlockSpec((tm, D), lambda i, j: (i, 0)),
            pl.BlockSpec((D, tn), lambda i, j: (0, j)),
            pl.BlockSpec((tm, LANES), lambda i, j: (i, 0)),
            pl.BlockSpec((tm, LANES), lambda i, j: (i, 0)),
        ],
        out_specs=pl.BlockSpec((tm, tn), lambda i, j: (i, j)),
        out_shape=jax.ShapeDtypeStruct((T, N), BF16),
        scratch_shapes=[pltpu.VMEM((tm, D), BF16)],
        compiler_params=_params("parallel", "arbitrary"),
        name="qkv_proj",
    )(x2d, w_bf16, cosf, sinf)


def _rope_tables(positions):
    inv_freq = ROPE_THETA ** (-jnp.arange(0, ROPE_DIMS, 2, dtype=F32) / ROPE_DIMS)
    ang = positions.astype(F32).reshape(-1)[:, None] * inv_freq
    cos, sin = jnp.cos(ang), jnp.sin(ang)
    T = ang.shape[0]
    rest = LANES - ROPE_DIMS
    cosf = jnp.concatenate([cos, cos, jnp.ones((T, rest), F32)], axis=1)
    sinf = jnp.concatenate([-sin, sin, jnp.zeros((T, rest), F32)], axis=1)
    return cosf, sinf


def _softmax_first(sT, v):
    m = jnp.max(sT, axis=0, keepdims=True)
    p = jnp.exp(sT - m)
    l = jnp.sum(p, axis=0, keepdims=True)
    acc = lax.dot_general(v, p.astype(BF16), _TN, preferred_element_type=F32)
    return m, l, acc


def _softmax_update(sTs, vs, m_ref, l_ref, acc_ref):
    m_prev = m_ref[...]
    m_new = m_prev
    for sT in sTs:
        m_new = jnp.maximum(m_new, jnp.max(sT, axis=0, keepdims=True))
    alpha = jnp.exp(m_prev - m_new)
    l = alpha * l_ref[...]
    pv = None
    for sT, v in zip(sTs, vs):
        p = jnp.exp(sT - m_new)
        l = l + jnp.sum(p, axis=0, keepdims=True)
        d = lax.dot_general(v, p.astype(BF16), _TN, preferred_element_type=F32)
        pv = d if pv is None else pv + d
    l_ref[...] = l
    acc_ref[...] = alpha * acc_ref[...] + pv
    m_ref[...] = m_new


def _causal_mask(sT, key0, qry0):
    tk, tq = sT.shape
    key = key0 + lax.broadcasted_iota(jnp.int32, (tk, tq), 0)
    qry = qry0 + lax.broadcasted_iota(jnp.int32, (tk, tq), 1)
    return jnp.where(key <= qry, sT, MASK_VALUE)


def _da_kernel(lam_ref, q_ref, k_ref, v_ref, g_ref, o_ref, m0, l0, acc0, m1, l1, acc1, s_a, s_b,
               *, tq, tk, out_scale):
    qi = pl.program_id(2)
    hd = HEAD_DIM
    gd = (qi * tq) // tk
    state = ((m0, l0, acc0), (m1, l1, acc1))

    def scores(g, c):
        koff = pl.multiple_of(g * tk, tk)
        kblk = k_ref[pl.ds(koff, tk), c * hd:(c + 1) * hd]
        return lax.dot_general(kblk, q_ref[:, c * hd:(c + 1) * hd], _NT, preferred_element_type=F32)

    def values(g):
        voff = pl.multiple_of(g * tk, tk)
        return v_ref[pl.ds(voff, tk), :]

    last_tile = jnp.maximum(gd - 1, 0)

    def fill(s_ref, g_req):
        g = jnp.minimum(g_req, last_tile)
        for c in range(2):
            s_ref[c] = scores(g, c)

    def consume(s_ref, g):
        v = values(g)
        for c in range(2):
            _softmax_update([s_ref[c]], [v], *state[c])

    fill(s_a, 0)

    v_diag = values(gd)
    for c in range(2):
        m, l, acc = _softmax_first(_causal_mask(scores(gd, c), gd * tk, qi * tq), v_diag)
        m_ref, l_ref, acc_ref = state[c]
        m_ref[...] = m
        l_ref[...] = l
        acc_ref[...] = acc

    def pair(p, carry):
        g0 = 2 * p
        fill(s_b, g0 + 1)
        consume(s_a, g0)
        fill(s_a, g0 + 2)
        consume(s_b, g0 + 1)
        return carry

    lax.fori_loop(0, gd // 2, pair, 0)

    @pl.when(lax.rem(gd, 2) == 1)
    def _():
        consume(s_a, last_tile)

    lam = lam_ref[0]
    oT = acc0[...] / l0[...] - lam * (acc1[...] / l1[...])
    o = oT.T
    o = o * lax.rsqrt(jnp.mean(o * o, axis=-1, keepdims=True) + LN_EPS) * g_ref[...]
    o_ref[...] = (o * out_scale).astype(o_ref.dtype)


def _diff_attention(qkv, lam, subln_g, *, B, S, H, tq, tk, lambda_init):
    T = B * S
    nq = S // tq
    dv = 2 * HEAD_DIM
    kern = functools.partial(_da_kernel, tq=tq, tk=tk, out_scale=1.0 - lambda_init)
    return pl.pallas_call(
        kern,
        grid=(B, H, nq),
        in_specs=[
            pl.BlockSpec(memory_space=pltpu.SMEM),
            pl.BlockSpec((tq, dv), lambda b, h, i: (b * nq + i, h)),
            pl.BlockSpec((S, dv), lambda b, h, i: (b, H + h)),
            pl.BlockSpec((S, dv), lambda b, h, i: (b, 2 * H + h)),
            pl.BlockSpec((1, dv), lambda b, h, i: (0, 0)),
        ],
        out_specs=pl.BlockSpec((tq, dv), lambda b, h, i: (b * nq + i, h)),
        out_shape=jax.ShapeDtypeStruct((T, H * dv), BF16),
        scratch_shapes=2 * [pltpu.VMEM((1, tq), F32), pltpu.VMEM((1, tq), F32), pltpu.VMEM((dv, tq), F32)]
        + 2 * [pltpu.VMEM((2, tk, tq), F32)],
        compiler_params=_params("parallel", "parallel", "arbitrary"),
        name="diff_attention",
    )(lam, qkv, qkv, qkv, subln_g)


def _mb_kernel(q_ref, k_ref, v_ref, o_ref, kmean_ref, bias_ref, m_ref, l_ref, acc_ref, s_a, s_b, *, nb, group):
    j = pl.program_id(2)
    blk = MB_BLOCK

    @pl.when(j == 0)
    def _():
        for n in range(nb):
            kb = k_ref[n * blk:(n + 1) * blk, :].astype(F32)
            kmean_ref[n:n + 1, :] = jnp.sum(kb, axis=0, keepdims=True) * (1.0 / blk)

    q = q_ref[...]
    km = kmean_ref[...]
    km_hi = km.astype(BF16)
    km_lo = (km - km_hi.astype(F32)).astype(BF16)
    gate = (lax.dot_general(km_hi, q, _NT, preferred_element_type=F32)
            + lax.dot_general(km_lo, q, _NT, preferred_element_type=F32))
    bid = lax.broadcasted_iota(jnp.int32, gate.shape, 0)
    rem = jnp.where(bid < j, gate, -jnp.inf)
    sel = jnp.zeros(gate.shape, jnp.bool_)
    for _ in range(MB_TOPK):
        mx = jnp.max(rem, axis=0, keepdims=True)
        first = jnp.min(jnp.where(rem == mx, bid, nb), axis=0, keepdims=True)
        pick = (bid == first) & (mx > -jnp.inf)
        sel = sel | pick
        rem = jnp.where(bid == first, -jnp.inf, rem)
    bias = jnp.where(sel, 0.0, MASK_VALUE).astype(F32)
    for gg in range(nb // group):
        bias_ref[gg] = bias[gg * group:(gg + 1) * group, :]

    def scores(n):
        koff = pl.multiple_of(n * blk, blk)
        return lax.dot_general(k_ref[pl.ds(koff, blk), :], q, _NT, preferred_element_type=F32)

    def values(n):
        voff = pl.multiple_of(n * blk, blk)
        return v_ref[pl.ds(voff, blk), :]

    n_groups = (j + group - 1) // group
    last_group = jnp.maximum(n_groups - 1, 0)

    def fill(s_ref, g_req):
        g = jnp.minimum(g_req, last_group)
        gbias = bias_ref[g]
        for t in range(group):
            s_ref[t] = scores(g * group + t) + gbias[t:t + 1, :]

    def consume(s_ref, g):
        _softmax_update([s_ref[t] for t in range(group)], [values(g * group + t) for t in range(group)],
                        m_ref, l_ref, acc_ref)

    fill(s_a, 0)

    m, l, acc = _softmax_first(_causal_mask(scores(j), 0, 0), values(j))
    m_ref[...] = m
    l_ref[...] = l
    acc_ref[...] = acc

    def pair(p, carry):
        g0 = 2 * p
        fill(s_b, g0 + 1)
        consume(s_a, g0)
        fill(s_a, g0 + 2)
        consume(s_b, g0 + 1)
        return carry

    lax.fori_loop(0, n_groups // 2, pair, 0)

    @pl.when(lax.rem(n_groups, 2) == 1)
    def _():
        consume(s_a, last_group)

    oT = acc_ref[...] / l_ref[...]
    o_ref[...] = oT.T.astype(o_ref.dtype)


def _moba_attention(qkv, *, B, S, H, group):
    T = B * S
    blk = MB_BLOCK
    nb = S // blk
    hd = HEAD_DIM
    kern = functools.partial(_mb_kernel, nb=nb, group=group)
    return pl.pallas_call(
        kern,
        grid=(B, H, nb),
        in_specs=[
            pl.BlockSpec((blk, hd), lambda b, h, i: (b * nb + i, h)),
            pl.BlockSpec((S, hd), lambda b, h, i: (b, H + h)),
            pl.BlockSpec((S, hd), lambda b, h, i: (b, 2 * H + h)),
        ],
        out_specs=pl.BlockSpec((blk, hd), lambda b, h, i: (b * nb + i, h)),
        out_shape=jax.ShapeDtypeStruct((T, H * hd), BF16),
        scratch_shapes=[
            pltpu.VMEM((nb, hd), F32),
            pltpu.VMEM((nb // group, group, blk), F32),
            pltpu.VMEM((1, blk), F32),
            pltpu.VMEM((1, blk), F32),
            pltpu.VMEM((hd, blk), F32),
            pltpu.VMEM((group, blk, blk), F32),
            pltpu.VMEM((group, blk, blk), F32),
        ],
        compiler_params=_params("parallel", "parallel", "arbitrary"),
        name="moba_attention",
    )(qkv, qkv, qkv)


def _router(x1, wr_ref, rb_ref, eidx_ref, wgt_ref):
    E = wr_ref.shape[1] // 2
    per_group = E // N_GROUPS
    tm = x1.shape[0]
    x_hi = x1.astype(BF16)
    x_lo = (x1 - x_hi.astype(F32)).astype(BF16)
    parts = (jnp.dot(x_hi, wr_ref[...], preferred_element_type=F32)
             + jnp.dot(x_lo, wr_ref[...], preferred_element_type=F32))
    logits = (parts[:, :E] + parts[:, E:]).T
    scores = _sigmoid(logits)
    biased = scores + rb_ref[...]

    gid8 = lax.broadcasted_iota(jnp.int32, (per_group, tm), 0)
    gscores = []
    for g in range(N_GROUPS):
        bg = biased[g * per_group:(g + 1) * per_group, :]
        m1 = jnp.max(bg, axis=0, keepdims=True)
        i1 = jnp.min(jnp.where(bg == m1, gid8, per_group), axis=0, keepdims=True)
        m2 = jnp.max(jnp.where(gid8 == i1, -jnp.inf, bg), axis=0, keepdims=True)
        gscores.append(m1 + m2)
    gs = jnp.concatenate(gscores, axis=0)

    gidx = lax.broadcasted_iota(jnp.int32, (N_GROUPS, tm), 0)
    rank = jnp.zeros((N_GROUPS, tm), jnp.int32)
    for g in range(N_GROUPS):
        row = gs[g:g + 1, :]
        beats = (row > gs) | ((row == gs) & (g < gidx))
        rank = rank + beats.astype(jnp.int32)
    gsel = rank < TOPK_GROUPS

    masked = jnp.concatenate(
        [jnp.where(gsel[g:g + 1, :], biased[g * per_group:(g + 1) * per_group, :], -jnp.inf)
         for g in range(N_GROUPS)], axis=0)

    eid = lax.broadcasted_iota(jnp.int32, (E, tm), 0)
    rem = masked
    ids, vals = [], []
    for _ in range(TOP_K):
        mx = jnp.max(rem, axis=0, keepdims=True)
        first = jnp.min(jnp.where(rem == mx, eid, E), axis=0, keepdims=True)
        onehot = eid == first
        ids.append(first)
        vals.append(jnp.sum(jnp.where(onehot, scores, 0.0), axis=0, keepdims=True))
        rem = jnp.where(onehot, -jnp.inf, rem)
    w = jnp.concatenate(vals, axis=0)
    w = w / jnp.sum(w, axis=0, keepdims=True) * ROUTED_SCALE
    eidx_ref[...] = jnp.concatenate(ids, axis=0)
    wgt_ref[...] = w


def _proj_ln_router_kernel(o_ref, w_ref, x_ref, g_ref, b_ref, wr_ref, rb_ref,
                           x1_ref, x1p_ref, eidx_ref, wgt_ref, *, alpha):
    h = jnp.dot(o_ref[...], w_ref[...], preferred_element_type=F32)
    x1 = _layer_norm(alpha * x_ref[...] + h, g_ref[...], b_ref[...])
    x1_ref[...] = x1
    _pack_rows(x1, x1p_ref)
    _router(x1, wr_ref, rb_ref, eidx_ref, wgt_ref)


def _proj_ln_router(o2d, w_out_bf16, x2d, g, b, wr, rb, *, alpha, tm):
    T, D = x2d.shape
    Do = o2d.shape[1]
    E = wr.shape[1] // 2
    kern = functools.partial(_proj_ln_router_kernel, alpha=alpha)
    const = lambda i: (0, 0)
    return pl.pallas_call(
        kern,
        grid=(T // tm,),
        in_specs=[
            pl.BlockSpec((tm, Do), lambda i: (i, 0)),
            pl.BlockSpec((Do, D), const),
            pl.BlockSpec((tm, D), lambda i: (i, 0)),
            pl.BlockSpec((1, D), const),
            pl.BlockSpec((1, D), const),
            pl.BlockSpec((D, 2 * E), const),
            pl.BlockSpec((E, 1), const),
        ],
        out_specs=[
            pl.BlockSpec((tm, D), lambda i: (i, 0)),
            pl.BlockSpec((tm * (D // (2 * LANES)), LANES), lambda i: (i, 0)),
            pl.BlockSpec((TOP_K, tm), lambda i: (0, i)),
            pl.BlockSpec((TOP_K, tm), lambda i: (0, i)),
        ],
        out_shape=[
            jax.ShapeDtypeStruct((T, D), F32),
            jax.ShapeDtypeStruct((T * (D // (2 * LANES)), LANES), jnp.uint32),
            jax.ShapeDtypeStruct((TOP_K, T), jnp.int32),
            jax.ShapeDtypeStruct((TOP_K, T), F32),
        ],
        compiler_params=_params("parallel"),
        name="outproj_ln_router",
    )(o2d, w_out_bf16, x2d, g, b, wr, rb)


def _routing_plan(eidxT, *, n_experts, tm):
    K, T = eidxT.shape
    n_slots = T * K
    n_tiles = n_slots // tm + n_experts
    e_flat = eidxT.reshape(-1)
    order = jnp.argsort(e_flat, stable=True).astype(jnp.int32)
    counts = jnp.sum((e_flat[:, None] == jnp.arange(n_experts, dtype=jnp.int32)[None, :]).astype(jnp.int32), axis=0)
    tiles_per_e = (counts + tm - 1) // tm
    tile_end = jnp.cumsum(tiles_per_e)
    tile_start = tile_end - tiles_per_e
    run_start = jnp.cumsum(counts) - counts
    total_tiles = tile_end[-1]

    tile_ids = jnp.arange(n_tiles, dtype=jnp.int32)
    tile_valid = tile_ids < total_tiles
    last_tile = jnp.maximum(total_tiles - 1, 0)
    tile_e = jnp.sum((jnp.minimum(tile_ids, last_tile)[:, None] >= tile_end[None, :]).astype(jnp.int32), axis=1)
    tile_e = jnp.minimum(tile_e, n_experts - 1)

    tile_off = (tile_ids - tile_start[tile_e]) * tm
    n_valid = jnp.where(tile_valid, jnp.clip(counts[tile_e] - tile_off, 0, tm), 0).astype(jnp.int32)
    r = jnp.arange(tm, dtype=jnp.int32)[None, :]
    src = jnp.clip((run_start[tile_e] + tile_off)[:, None] + r, 0, n_slots - 1)
    slots = jnp.where(r < n_valid[:, None], order[src], 0).astype(jnp.int32)
    rank = jnp.argsort(order).astype(jnp.int32)
    dest = rank - run_start[e_flat] + tile_start[e_flat] * tm
    pad_per_e = tiles_per_e * tm - counts
    pad_end = jnp.cumsum(pad_per_e)
    q = jnp.arange(n_experts * tm, dtype=jnp.int32)
    seg = jnp.sum((q[:, None] >= pad_end[None, :]).astype(jnp.int32), axis=1)
    seg_e = jnp.minimum(seg, n_experts - 1)
    in_expert = tile_start[seg_e] * tm + counts[seg_e] + (q - (pad_end - pad_per_e)[seg_e])
    in_tail = total_tiles * tm + (q - pad_end[-1])
    pad_rows = jnp.where(seg < n_experts, in_expert, in_tail).astype(jnp.int32)
    return tile_e, n_valid, slots, dest.reshape(K, T).T, pad_rows


def _dispatch_kernel(dest_hbm, pad_hbm, x_ref, xs_hbm, dest_smem, pad_smem, id_sems, sem, *, n_sub):
    i = pl.program_id(0)
    tt = x_ref.shape[0] // n_sub
    n_pad = pad_smem.shape[0]
    ids = (pltpu.make_async_copy(dest_hbm.at[i], dest_smem, id_sems.at[0]),
           pltpu.make_async_copy(pad_hbm.at[i], pad_smem, id_sems.at[1]))
    for cp in ids:
        cp.start()
    for cp in ids:
        cp.wait()

    def token(t, carry):
        src = pl.multiple_of(t * n_sub, n_sub)
        for k in range(TOP_K):
            dst = pl.multiple_of(dest_smem[t * TOP_K + k], n_sub)
            pltpu.make_async_copy(x_ref.at[pl.ds(src, n_sub), :], xs_hbm.at[pl.ds(dst, n_sub), :], sem).start()
        return carry

    lax.fori_loop(0, tt, token, 0, unroll=2)

    def pad(r, carry):
        dst = pl.multiple_of(pad_smem[r], n_sub)
        pltpu.make_async_copy(x_ref.at[pl.ds(0, n_sub), :], xs_hbm.at[pl.ds(dst, n_sub), :], sem).start()
        return carry

    lax.fori_loop(0, n_pad, pad, 0, unroll=ISSUE_UNROLL)
    for _ in range(TOP_K + n_pad // tt):
        pltpu.make_async_copy(x_ref, xs_hbm.at[pl.ds(0, tt * n_sub), :], sem).wait()
    rest = (n_pad % tt) * n_sub
    if rest:
        pltpu.make_async_copy(x_ref.at[pl.ds(0, rest), :], xs_hbm.at[pl.ds(0, rest), :], sem).wait()


def _dispatch(x1p, dest_tk, pad_rows, *, tt):
    S = x1p.shape[0] // dest_tk.shape[0]
    T = dest_tk.shape[0]
    n_steps = T // tt
    n_rows = T * TOP_K + pad_rows.shape[0]
    assert pad_rows.shape[0] % n_steps == 0
    dest = (dest_tk * S).reshape(n_steps, tt * TOP_K)
    pads = (pad_rows * S).reshape(n_steps, -1)
    return pl.pallas_call(
        functools.partial(_dispatch_kernel, n_sub=S),
        grid=(n_steps,),
        in_specs=[
            pl.BlockSpec(memory_space=pl.ANY),
            pl.BlockSpec(memory_space=pl.ANY),
            pl.BlockSpec((tt * S, LANES), lambda i: (i, 0)),
        ],
        out_specs=pl.BlockSpec(memory_space=pl.ANY),
        out_shape=jax.ShapeDtypeStruct((n_rows * S, LANES), jnp.uint32),
        scratch_shapes=[
            pltpu.SMEM((tt * TOP_K,), jnp.int32),
            pltpu.SMEM((pads.shape[1],), jnp.int32),
            pltpu.SemaphoreType.DMA((2,)),
            pltpu.SemaphoreType.DMA,
        ],
        compiler_params=_params("arbitrary"),
        name="dispatch_rows",
    )(dest, pads, x1p)


def _experts_kernel(te_ref, nv_ref, slot_hbm, xs_hbm, wg_ref, wu_ref, wd_ref, y_hbm,
                    slot_a, slot_b, xbuf, ybuf, wg_bf, wu_bf, wd_bf,
                    slot_sems, load_sems, scatter_sems, *, n_sub):
    i = pl.program_id(0)
    last = pl.num_programs(0) - 1
    tm = xbuf.shape[1] // n_sub
    slots = (slot_a, slot_b)
    n_valid = nv_ref[i]

    @pl.when((i == 0) | (te_ref[i] != te_ref[jnp.maximum(i - 1, 0)]))
    def _():
        wg_bf[...] = wg_ref[...].astype(BF16)
        wu_bf[...] = wu_ref[...].astype(BF16)
        wd_bf[...] = wd_ref[...].astype(BF16)

    def slot_copy(tile, b):
        return pltpu.make_async_copy(slot_hbm.at[jnp.minimum(tile, last)], slots[b], slot_sems.at[b])

    def load_copy(tile, b):
        t = jnp.minimum(tile, last)
        rows = pl.multiple_of(nv_ref[t] * n_sub, n_sub)
        start = pl.multiple_of(t * (tm * n_sub), tm * n_sub)
        return pltpu.make_async_copy(xs_hbm.at[pl.ds(start, rows), :], xbuf.at[b, pl.ds(0, rows), :], load_sems.at[b])

    def start_load(tile, b):
        @pl.when(nv_ref[jnp.minimum(tile, last)] > 0)
        def _():
            load_copy(tile, b).start()

    def wait_load(tile, b):
        @pl.when(nv_ref[jnp.minimum(tile, last)] > 0)
        def _():
            load_copy(tile, b).wait()

    def start_scatter(b, n):
        def row(r):
            src = pl.multiple_of(r * n_sub, n_sub)
            dst = pl.multiple_of(slots[b][r], n_sub)
            pltpu.make_async_copy(ybuf.at[b, pl.ds(src, n_sub), :], y_hbm.at[pl.ds(dst, n_sub), :],
                                  scatter_sems.at[b]).start()

        def group(g, carry):
            for u in range(ISSUE_UNROLL):
                row(g * ISSUE_UNROLL + u)
            return carry

        def single(r, carry):
            row(r)
            return carry

        n_groups = n // ISSUE_UNROLL
        lax.fori_loop(0, n_groups, group, 0)
        lax.fori_loop(n_groups * ISSUE_UNROLL, n, single, 0)

    def wait_scatter(b, n):
        @pl.when(n > 0)
        def _():
            rows = pl.multiple_of(n * n_sub, n_sub)
            pltpu.make_async_copy(ybuf.at[b, pl.ds(0, rows), :], y_hbm.at[pl.ds(0, rows), :],
                                  scatter_sems.at[b]).wait()

    def step(cur):
        nxt = 1 - cur

        @pl.when(i == 0)
        def _():
            xbuf[...] = jnp.zeros(xbuf.shape, xbuf.dtype)
            start_load(0, cur)
            slot_copy(0, cur).start()

        slot_copy(i + 1, nxt).start()
        wait_load(i, cur)

        @pl.when(i < last)
        def _():
            start_load(i + 1, nxt)

        x_lo, x_hi = _unpack_rows(xbuf.at[cur], n_sub)
        xb = jnp.concatenate([x_lo, x_hi], axis=1).astype(BF16)
        g = jnp.dot(xb, wg_bf[...], preferred_element_type=F32)
        u = jnp.dot(xb, wu_bf[...], preferred_element_type=F32)
        h = (g * _sigmoid(g) * u).astype(BF16)
        _pack_rows(jnp.dot(h, wd_bf[...], preferred_element_type=F32), ybuf.at[cur])

        @pl.when(i > 0)
        def _():
            wait_scatter(nxt, nv_ref[jnp.maximum(i - 1, 0)])

        slot_copy(i, cur).wait()
        start_scatter(cur, n_valid)

        @pl.when(i == last)
        def _():
            wait_scatter(cur, n_valid)
            slot_copy(i + 1, nxt).wait()

    parity = lax.rem(i, 2)
    for b in range(2):
        pl.when(parity == b)(functools.partial(step, b))


def _routed_experts(xs, tile_e, tile_rows, slots, wg, wu, wd, *, layer, tm):
    D = wg.shape[2]
    F = wg.shape[3]
    S = D // (2 * LANES)
    n_tiles = slots.shape[0]
    n_slots = jnp.size(slots) - wg.shape[1] * tm
    grid_spec = pltpu.PrefetchScalarGridSpec(
        num_scalar_prefetch=2,
        grid=(n_tiles,),
        in_specs=[
            pl.BlockSpec(memory_space=pl.ANY),
            pl.BlockSpec(memory_space=pl.ANY),
            pl.BlockSpec((None, None, D, F), lambda i, te, nv: (layer, te[i], 0, 0)),
            pl.BlockSpec((None, None, D, F), lambda i, te, nv: (layer, te[i], 0, 0)),
            pl.BlockSpec((None, None, F, D), lambda i, te, nv: (layer, te[i], 0, 0)),
        ],
        out_specs=pl.BlockSpec(memory_space=pl.ANY),
        scratch_shapes=[
            pltpu.SMEM((tm,), jnp.int32),
            pltpu.SMEM((tm,), jnp.int32),
            pltpu.VMEM((2, tm * S, LANES), jnp.uint32),
            pltpu.VMEM((2, tm * S, LANES), jnp.uint32),
            pltpu.VMEM((D, F), BF16),
            pltpu.VMEM((D, F), BF16),
            pltpu.VMEM((F, D), BF16),
            pltpu.SemaphoreType.DMA((2,)),
            pltpu.SemaphoreType.DMA((2,)),
            pltpu.SemaphoreType.DMA((2,)),
        ],
    )
    return pl.pallas_call(
        functools.partial(_experts_kernel, n_sub=S),
        grid_spec=grid_spec,
        out_shape=jax.ShapeDtypeStruct((n_slots * S, LANES), jnp.uint32),
        compiler_params=_params("arbitrary"),
        name="routed_experts",
    )(tile_e, tile_rows, slots * S, xs, wg, wu, wd)


def _combine_kernel(y_ref, w_ref, x1_ref, wsg_ref, wsu_ref, wsd_ref, g_ref, b_ref, out_ref, *, alpha):
    x1 = x1_ref[...]
    xb = x1.astype(BF16)
    gg = jnp.dot(xb, wsg_ref[...], preferred_element_type=F32)
    uu = jnp.dot(xb, wsu_ref[...], preferred_element_type=F32)
    hh = (gg * _sigmoid(gg) * uu).astype(BF16)
    shared = jnp.dot(hh, wsd_ref[...], preferred_element_type=F32)
    w = w_ref[...]
    routed = None
    for k in range(TOP_K):
        y_lo, y_hi = _unpack_rows(y_ref.at[k], y_ref.shape[1] // x1.shape[0])
        term = w[:, k:k + 1] * jnp.concatenate([y_lo, y_hi], axis=1)
        routed = term if routed is None else routed + term
    out_ref[...] = _layer_norm(alpha * x1 + (routed + shared), g_ref[...], b_ref[...])


def _combine(y_kt, w_tk, x1, wsg, wsu, wsd, g, b, *, alpha, tm):
    T, D = x1.shape
    Fs = wsg.shape[1]
    kern = functools.partial(_combine_kernel, alpha=alpha)
    const = lambda i: (0, 0)
    return pl.pallas_call(
        kern,
        grid=(T // tm,),
        in_specs=[
            pl.BlockSpec((TOP_K, tm * (D // (2 * LANES)), LANES), lambda i: (0, i, 0)),
            pl.BlockSpec((tm, TOP_K), lambda i: (i, 0)),
            pl.BlockSpec((tm, D), lambda i: (i, 0)),
            pl.BlockSpec((D, Fs), const),
            pl.BlockSpec((D, Fs), const),
            pl.BlockSpec((Fs, D), const),
            pl.BlockSpec((1, D), const),
            pl.BlockSpec((1, D), const),
        ],
        out_specs=pl.BlockSpec((tm, D), lambda i: (i, 0)),
        out_shape=jax.ShapeDtypeStruct((T, D), F32),
        compiler_params=_params("parallel"),
        name="shared_combine_ln",
    )(y_kt, w_tk, x1, wsg, wsu, wsd, g, b)


def _split_bf16(w):
    hi = w.astype(BF16)
    lo = (w - hi.astype(F32)).astype(BF16)
    return jnp.concatenate([hi, lo], axis=-1)


def _pick(n, pref):
    t = min(n, pref)
    while n % t:
        t //= 2
    return t


def _moe_layer(x_attn_in, o2d, w_out, ln1_g, ln1_b, w_router, router_bias, w_gate, w_up, w_down,
               ws_gate, ws_up, ws_down, ln2_g, ln2_b, *, alpha, layer):
    T, D = x_attn_in.shape
    E = w_router.shape[1]
    x1, x1p, eidxT, wgtT = _proj_ln_router(
        o2d, w_out.astype(BF16), x_attn_in, ln1_g[None, :], ln1_b[None, :],
        _split_bf16(w_router), router_bias[:, None], alpha=alpha, tm=_pick(T, 256))
    tm_e = _pick(T * TOP_K, 256)
    tile_e, tile_rows, slots, dest_tk, pad_rows = _routing_plan(eidxT, n_experts=E, tm=tm_e)
    xs = _dispatch(x1p, dest_tk, pad_rows, tt=_pick(T, 256))
    y = _routed_experts(xs, tile_e, tile_rows, slots, w_gate, w_up, w_down, layer=layer, tm=tm_e)
    return _combine(y.reshape(TOP_K, T * (D // (2 * LANES)), LANES), wgtT.T, x1, ws_gate.astype(BF16), ws_up.astype(BF16), ws_down.astype(BF16),
                    ln2_g[None, :], ln2_b[None, :], alpha=alpha, tm=_pick(T, 128))


def kernel(x, positions, ln1_g, ln1_b, ln2_g, ln2_b, da_w_in, da_w_out, da_lq1, da_lk1, da_lq2, da_lk2,
           da_subln_g, mb_w_in, mb_w_out, w_router, router_bias, w_gate, w_up, w_down, ws_gate, ws_up, ws_down):
    B, S, D = x.shape
    T = B * S
    depth = ln1_g.shape[0]
    alpha = (2 * depth) ** 0.25
    hd = HEAD_DIM
    cosf, sinf = _rope_tables(positions)
    xt = x.reshape(T, D)
    tm_proj = _pick(T, 1024)

    for i in range(depth):
        m = i // 2
        if i % 2 == 0:
            w_in = da_w_in[m]
            H = w_in.shape[1] // (6 * hd)
            qk_cols = 4 * H * hd
            tq = _pick(S, 256)
            qkv = _qkv_proj(xt, w_in.astype(BF16), cosf, sinf, rope_cols=qk_cols, q_cols=qk_cols // 2,
                            tm=tm_proj, tn=_pick(qk_cols // 2, 512))
            tk = _pick(S, 1024)
            lambda_init = 0.8 - 0.6 * math.exp(-0.3 * i)
            lam = (jnp.exp(jnp.sum(da_lq1[m].astype(F32) * da_lk1[m].astype(F32)))
                   - jnp.exp(jnp.sum(da_lq2[m].astype(F32) * da_lk2[m].astype(F32))) + lambda_init)
            o2d = _diff_attention(qkv, lam.reshape(1), da_subln_g[m][None, :], B=B, S=S, H=H, tq=tq,
                                  tk=tk, lambda_init=lambda_init)
            w_out = da_w_out[m]
        else:
            w_in = mb_w_in[m]
            H = w_in.shape[1] // (3 * hd)
            qk_cols = 2 * H * hd
            qkv = _qkv_proj(xt, w_in.astype(BF16), cosf, sinf, rope_cols=qk_cols, q_cols=qk_cols // 2,
                            tm=tm_proj, tn=_pick(qk_cols // 2, 512))
            o2d = _moba_attention(qkv, B=B, S=S, H=H, group=_pick(S // MB_BLOCK, MB_GROUP))
            w_out = mb_w_out[m]
        xt = _moe_layer(xt, o2d, w_out, ln1_g[i], ln1_b[i], w_router[i], router_bias[i], w_gate, w_up,
                        w_down, ws_gate[i], ws_up[i], ws_down[i], ln2_g[i], ln2_b[i], alpha=alpha, layer=i)
    return xt.reshape(B, S, D)
```

```python
import functools
import math

import jax
import jax.numpy as jnp
from jax import lax
from jax.experimental import pallas as pl
from jax.experimental.pallas import tpu as pltpu

F32 = jnp.float32
BF16 = jnp.bfloat16

HEAD_DIM = 128
ROPE_THETA = 500000.0
ROPE_DIMS = HEAD_DIM // 4
ROPE_HALF = ROPE_DIMS // 2
LN_EPS = 1e-5
MB_BLOCK = 256
MB_TOPK = 3
MB_GROUP = 4
TOP_K = 8
N_GROUPS = 8
TOPK_GROUPS = 4
ROUTED_SCALE = 2.5
MASK_VALUE = -1e30
LOG2_E = math.log2(math.e)

LANES = 128
SUBLANES = 8
ISSUE_UNROLL = 8
VMEM_LIMIT_BYTES = 56 * 1024 * 1024

_NT = (((1,), (1,)), ((), ()))
_TN = (((0,), (0,)), ((), ()))


def _params(*sem):
    return pltpu.CompilerParams(dimension_semantics=sem, vmem_limit_bytes=VMEM_LIMIT_BYTES)


def _sigmoid(x):
    return 1.0 / (1.0 + jnp.exp(-x))


def _pack_rows(x, o_ref):
    tm, d = x.shape
    half = d // 2
    n_sub = half // LANES
    lo = pltpu.bitcast(x[:, :half].astype(BF16).astype(F32), jnp.uint32) >> 16
    hi = pltpu.bitcast(x[:, half:].astype(BF16).astype(F32), jnp.uint32) & jnp.uint32(0xFFFF0000)
    packed = hi | lo
    for s in range(n_sub):
        o_ref[pl.ds(s, tm, stride=n_sub), :] = packed[:, s * LANES:(s + 1) * LANES]


def _unpack_rows(p_ref, n_sub):
    tm = p_ref.shape[0] // n_sub
    p = jnp.concatenate([p_ref[pl.ds(s, tm, stride=n_sub), :] for s in range(n_sub)], axis=1)
    return pltpu.bitcast(p << 16, F32), pltpu.bitcast(p & jnp.uint32(0xFFFF0000), F32)


def _layer_norm(y, g, b):
    mu = jnp.mean(y, axis=-1, keepdims=True)
    d = y - mu
    var = jnp.mean(d * d, axis=-1, keepdims=True)
    return d * lax.rsqrt(var + LN_EPS) * g + b


def _qkv_kernel(x_ref, w_ref, cos_ref, sin_ref, o_ref, xb_ref, *, rope_blocks, q_blocks, q_scale):
    j = pl.program_id(1)

    @pl.when(j == 0)
    def _():
        xb_ref[...] = x_ref[...].astype(BF16)

    acc = jnp.dot(xb_ref[...], w_ref[...], preferred_element_type=F32)
    tn = acc.shape[1]

    @pl.when(j < rope_blocks)
    def _():
        cosf = cos_ref[...]
        sinf = sin_ref[...]
        lane = lax.broadcasted_iota(jnp.int32, (1, LANES), 1)
        scale = jnp.where(j < q_blocks, q_scale, 1.0).astype(F32)
        for c in range(tn // LANES):
            xc = acc[:, c * LANES:(c + 1) * LANES]
            partner = jnp.where(lane < ROPE_HALF,
                                pltpu.roll(xc, LANES - ROPE_HALF, 1),
                                pltpu.roll(xc, ROPE_HALF, 1))
            r = (xc * cosf + partner * sinf) * scale
            o_ref[:, c * LANES:(c + 1) * LANES] = r.astype(o_ref.dtype)

    @pl.when(j >= rope_blocks)
    def _():
        o_ref[...] = acc.astype(o_ref.dtype)


def _qkv_proj(x2d, w_bf16, cosf, sinf, *, rope_cols, q_cols, tm, tn):
    T, D = x2d.shape
    N = w_bf16.shape[1]
    kern = functools.partial(_qkv_kernel, rope_blocks=rope_cols // tn, q_blocks=q_cols // tn,
                             q_scale=HEAD_DIM ** -0.5 * LOG2_E)
    return pl.pallas_call(
        kern,
        grid=(T // tm, N // tn),
        in_specs=[
            pl.BlockSpec((tm, D), lambda i, j: (i, 0)),
            pl.BlockSpec((D, tn), lambda i, j: (0, j)),
            pl.BlockSpec((tm, LANES), lambda i, j: (i, 0)),
            pl.BlockSpec((tm, LANES), lambda i, j: (i, 0)),
        ],
        out_specs=pl.BlockSpec((tm, tn), lambda i, j: (i, j)),
        out_shape=jax.ShapeDtypeStruct((T, N), BF16),
        scratch_shapes=[pltpu.VMEM((tm, D), BF16)],
        compiler_params=_params("parallel", "arbitrary"),
        name="qkv_proj",
    )(x2d, w_bf16, cosf, sinf)


def _rope_tables(positions):
    inv_freq = ROPE_THETA ** (-jnp.arange(0, ROPE_DIMS, 2, dtype=F32) / ROPE_DIMS)
    ang = positions.astype(F32).reshape(-1)[:, None] * inv_freq
    cos, sin = jnp.cos(ang), jnp.sin(ang)
    T = ang.shape[0]
    rest = LANES - ROPE_DIMS
    cosf = jnp.concatenate([cos, cos, jnp.ones((T, rest), F32)], axis=1)
    sinf = jnp.concatenate([-sin, sin, jnp.zeros((T, rest), F32)], axis=1)
    return cosf, sinf


def _softmax_first(sT, v):
    m = jnp.max(sT, axis=0, keepdims=True)
    p = jnp.exp2(sT - m)
    l = jnp.sum(p, axis=0, keepdims=True)
    acc = lax.dot_general(v, p.astype(BF16), _TN, preferred_element_type=F32)
    return m, l, acc


def _softmax_update(sTs, vs, m_ref, l_ref, acc_ref):
    m_prev = m_ref[...]
    m_new = m_prev
    for sT in sTs:
        m_new = jnp.maximum(m_new, jnp.max(sT, axis=0, keepdims=True))
    alpha = jnp.exp2(m_prev - m_new)
    l = alpha * l_ref[...]
    pv = None
    for sT, v in zip(sTs, vs):
        p = jnp.exp2(sT - m_new)
        l = l + jnp.sum(p, axis=0, keepdims=True)
        d = lax.dot_general(v, p.astype(BF16), _TN, preferred_element_type=F32)
        pv = d if pv is None else pv + d
    l_ref[...] = l
    acc_ref[...] = alpha * acc_ref[...] + pv
    m_ref[...] = m_new


def _causal_mask(sT, key0, qry0):
    tk, tq = sT.shape
    key = key0 + lax.broadcasted_iota(jnp.int32, (tk, tq), 0)
    qry = qry0 + lax.broadcasted_iota(jnp.int32, (tk, tq), 1)
    return jnp.where(key <= qry, sT, MASK_VALUE)


def _da_kernel(lam_ref, q_ref, k_ref, v_ref, g_ref, o_ref, m0, l0, acc0, m1, l1, acc1, s_a, s_b,
               *, tq, tk, out_scale):
    qi = pl.program_id(2)
    hd = HEAD_DIM
    gd = (qi * tq) // tk
    state = ((m0, l0, acc0), (m1, l1, acc1))

    def scores(g, c):
        koff = pl.multiple_of(g * tk, tk)
        kblk = k_ref[pl.ds(koff, tk), c * hd:(c + 1) * hd]
        return lax.dot_general(kblk, q_ref[:, c * hd:(c + 1) * hd], _NT, preferred_element_type=F32)

    def values(g):
        voff = pl.multiple_of(g * tk, tk)
        return v_ref[pl.ds(voff, tk), :]

    last_tile = jnp.maximum(gd - 1, 0)

    def fill(s_ref, g_req):
        g = jnp.minimum(g_req, last_tile)
        for c in range(2):
            s_ref[c] = scores(g, c)

    def consume(s_ref, g):
        v = values(g)
        for c in range(2):
            _softmax_update([s_ref[c]], [v], *state[c])

    fill(s_a, 0)

    v_diag = values(gd)
    for c in range(2):
        m, l, acc = _softmax_first(_causal_mask(scores(gd, c), gd * tk, qi * tq), v_diag)
        m_ref, l_ref, acc_ref = state[c]
        m_ref[...] = m
        l_ref[...] = l
        acc_ref[...] = acc

    def pair(p, carry):
        g0 = 2 * p
        fill(s_b, g0 + 1)
        consume(s_a, g0)
        fill(s_a, g0 + 2)
        consume(s_b, g0 + 1)
        return carry

    lax.fori_loop(0, gd // 2, pair, 0)

    @pl.when(lax.rem(gd, 2) == 1)
    def _():
        consume(s_a, last_tile)

    lam = lam_ref[0]
    oT = acc0[...] / l0[...] - lam * (acc1[...] / l1[...])
    o = oT.T
    o = o * lax.rsqrt(jnp.mean(o * o, axis=-1, keepdims=True) + LN_EPS) * g_ref[...]
    o_ref[...] = (o * out_scale).astype(o_ref.dtype)


def _diff_attention(qkv, lam, subln_g, *, B, S, H, tq, tk, lambda_init):
    T = B * S
    nq = S // tq
    dv = 2 * HEAD_DIM
    kern = functools.partial(_da_kernel, tq=tq, tk=tk, out_scale=1.0 - lambda_init)
    return pl.pallas_call(
        kern,
        grid=(B, H, nq),
        in_specs=[
            pl.BlockSpec(memory_space=pltpu.SMEM),
            pl.BlockSpec((tq, dv), lambda b, h, i: (b * nq + i, h)),
            pl.BlockSpec((S, dv), lambda b, h, i: (b, H + h)),
            pl.BlockSpec((S, dv), lambda b, h, i: (b, 2 * H + h)),
            pl.BlockSpec((1, dv), lambda b, h, i: (0, 0)),
        ],
        out_specs=pl.BlockSpec((tq, dv), lambda b, h, i: (b * nq + i, h)),
        out_shape=jax.ShapeDtypeStruct((T, H * dv), BF16),
        scratch_shapes=2 * [pltpu.VMEM((1, tq), F32), pltpu.VMEM((1, tq), F32), pltpu.VMEM((dv, tq), F32)]
        + 2 * [pltpu.VMEM((2, tk, tq), F32)],
        compiler_params=_params("parallel", "parallel", "arbitrary"),
        name="diff_attention",
    )(lam, qkv, qkv, qkv, subln_g)


def _mb_kernel(q_ref, k_ref, v_ref, o_ref, kmean_ref, bias_ref, m_ref, l_ref, acc_ref, s_a, s_b, *, nb, group):
    j = pl.program_id(2)
    blk = MB_BLOCK

    @pl.when(j == 0)
    def _():
        for n in range(nb):
            kb = k_ref[n * blk:(n + 1) * blk, :].astype(F32)
            kmean_ref[n:n + 1, :] = jnp.sum(kb, axis=0, keepdims=True) * (1.0 / blk)

    q = q_ref[...]
    km = kmean_ref[...]
    km_hi = km.astype(BF16)
    km_lo = (km - km_hi.astype(F32)).astype(BF16)
    gate = (lax.dot_general(km_hi, q, _NT, preferred_element_type=F32)
            + lax.dot_general(km_lo, q, _NT, preferred_element_type=F32))
    bid = lax.broadcasted_iota(jnp.int32, gate.shape, 0)
    rem = jnp.where(bid < j, gate, -jnp.inf)
    sel = jnp.zeros(gate.shape, jnp.bool_)
    for _ in range(MB_TOPK):
        mx = jnp.max(rem, axis=0, keepdims=True)
        first = jnp.min(jnp.where(rem == mx, bid, nb), axis=0, keepdims=True)
        pick = (bid == first) & (mx > -jnp.inf)
        sel = sel | pick
        rem = jnp.where(bid == first, -jnp.inf, rem)
    bias = jnp.where(sel, 0.0, MASK_VALUE).astype(F32)
    for gg in range(nb // group):
        bias_ref[gg] = bias[gg * group:(gg + 1) * group, :]

    def scores(n):
        koff = pl.multiple_of(n * blk, blk)
        return lax.dot_general(k_ref[pl.ds(koff, blk), :], q, _NT, preferred_element_type=F32)

    def values(n):
        voff = pl.multiple_of(n * blk, blk)
        return v_ref[pl.ds(voff, blk), :]

    n_groups = (j + group - 1) // group
    last_group = jnp.maximum(n_groups - 1, 0)

    def fill(s_ref, g_req):
        g = jnp.minimum(g_req, last_group)
        gbias = bias_ref[g]
        for t in range(group):
            s_ref[t] = scores(g * group + t) + gbias[t:t + 1, :]

    def consume(s_ref, g):
        _softmax_update([s_ref[t] for t in range(group)], [values(g * group + t) for t in range(group)],
                        m_ref, l_ref, acc_ref)

    fill(s_a, 0)

    m, l, acc = _softmax_first(_causal_mask(scores(j), 0, 0), values(j))
    m_ref[...] = m
    l_ref[...] = l
    acc_ref[...] = acc

    def pair(p, carry):
        g0 = 2 * p
        fill(s_b, g0 + 1)
        consume(s_a, g0)
        fill(s_a, g0 + 2)
        consume(s_b, g0 + 1)
        return carry

    lax.fori_loop(0, n_groups // 2, pair, 0)

    @pl.when(lax.rem(n_groups, 2) == 1)
    def _():
        consume(s_a, last_group)

    oT = acc_ref[...] / l_ref[...]
    o_ref[...] = oT.T.astype(o_ref.dtype)


def _moba_attention(qkv, *, B, S, H, group):
    T = B * S
    blk = MB_BLOCK
    nb = S // blk
    hd = HEAD_DIM
    kern = functools.partial(_mb_kernel, nb=nb, group=group)
    return pl.pallas_call(
        kern,
        grid=(B, H, nb),
        in_specs=[
            pl.BlockSpec((blk, hd), lambda b, h, i: (b * nb + i, h)),
            pl.BlockSpec((S, hd), lambda b, h, i: (b, H + h)),
            pl.BlockSpec((S, hd), lambda b, h, i: (b, 2 * H + h)),
        ],
        out_specs=pl.BlockSpec((blk, hd), lambda b, h, i: (b * nb + i, h)),
        out_shape=jax.ShapeDtypeStruct((T, H * hd), BF16),
        scratch_shapes=[
            pltpu.VMEM((nb, hd), F32),
            pltpu.VMEM((nb // group, group, blk), F32),
            pltpu.VMEM((1, blk), F32),
            pltpu.VMEM((1, blk), F32),
            pltpu.VMEM((hd, blk), F32),
            pltpu.VMEM((group, blk, blk), F32),
            pltpu.VMEM((group, blk, blk), F32),
        ],
        compiler_params=_params("parallel", "parallel", "arbitrary"),
        name="moba_attention",
    )(qkv, qkv, qkv)


def _router(x1, wr_ref, rb_ref, eidx_ref, wgt_ref):
    E = wr_ref.shape[1] // 2
    per_group = E // N_GROUPS
    tm = x1.shape[0]
    x_hi = x1.astype(BF16)
    x_lo = (x1 - x_hi.astype(F32)).astype(BF16)
    parts = (jnp.dot(x_hi, wr_ref[...], preferred_element_type=F32)
             + jnp.dot(x_lo, wr_ref[...], preferred_element_type=F32))
    logits = (parts[:, :E] + parts[:, E:]).T
    scores = _sigmoid(logits)
    biased = scores + rb_ref[...]

    gid8 = lax.broadcasted_iota(jnp.int32, (per_group, tm), 0)
    gscores = []
    for g in range(N_GROUPS):
        bg = biased[g * per_group:(g + 1) * per_group, :]
        m1 = jnp.max(bg, axis=0, keepdims=True)
        i1 = jnp.min(jnp.where(bg == m1, gid8, per_group), axis=0, keepdims=True)
        m2 = jnp.max(jnp.where(gid8 == i1, -jnp.inf, bg), axis=0, keepdims=True)
        gscores.append(m1 + m2)
    gs = jnp.concatenate(gscores, axis=0)

    gidx = lax.broadcasted_iota(jnp.int32, (N_GROUPS, tm), 0)
    rank = jnp.zeros((N_GROUPS, tm), jnp.int32)
    for g in range(N_GROUPS):
        row = gs[g:g + 1, :]
        beats = (row > gs) | ((row == gs) & (g < gidx))
        rank = rank + beats.astype(jnp.int32)
    gsel = rank < TOPK_GROUPS

    masked = jnp.concatenate(
        [jnp.where(gsel[g:g + 1, :], biased[g * per_group:(g + 1) * per_group, :], -jnp.inf)
         for g in range(N_GROUPS)], axis=0)

    eid = lax.broadcasted_iota(jnp.int32, (E, tm), 0)
    rem = masked
    ids, vals = [], []
    for _ in range(TOP_K):
        mx = jnp.max(rem, axis=0, keepdims=True)
        first = jnp.min(jnp.where(rem == mx, eid, E), axis=0, keepdims=True)
        onehot = eid == first
        ids.append(first)
        vals.append(jnp.sum(jnp.where(onehot, scores, 0.0), axis=0, keepdims=True))
        rem = jnp.where(onehot, -jnp.inf, rem)
    w = jnp.concatenate(vals, axis=0)
    w = w / jnp.sum(w, axis=0, keepdims=True) * ROUTED_SCALE
    eidx_ref[...] = jnp.concatenate(ids, axis=0)
    wgt_ref[...] = w


def _proj_ln_router_kernel(o_ref, w_ref, x_ref, g_ref, b_ref, wr_ref, rb_ref,
                           x1_ref, x1p_ref, eidx_ref, wgt_ref, *, alpha):
    h = jnp.dot(o_ref[...], w_ref[...], preferred_element_type=F32)
    x1 = _layer_norm(alpha * x_ref[...] + h, g_ref[...], b_ref[...])
    x1_ref[...] = x1
    _pack_rows(x1, x1p_ref)
    _router(x1, wr_ref, rb_ref, eidx_ref, wgt_ref)


def _proj_ln_router(o2d, w_out_bf16, x2d, g, b, wr, rb, *, alpha, tm):
    T, D = x2d.shape
    Do = o2d.shape[1]
    E = wr.shape[1] // 2
    kern = functools.partial(_proj_ln_router_kernel, alpha=alpha)
    const = lambda i: (0, 0)
    return pl.pallas_call(
        kern,
        grid=(T // tm,),
        in_specs=[
            pl.BlockSpec((tm, Do), lambda i: (i, 0)),
            pl.BlockSpec((Do, D), const),
            pl.BlockSpec((tm, D), lambda i: (i, 0)),
            pl.BlockSpec((1, D), const),
            pl.BlockSpec((1, D), const),
            pl.BlockSpec((D, 2 * E), const),
            pl.BlockSpec((E, 1), const),
        ],
        out_specs=[
            pl.BlockSpec((tm, D), lambda i: (i, 0)),
            pl.BlockSpec((tm * (D // (2 * LANES)), LANES), lambda i: (i, 0)),
            pl.BlockSpec((TOP_K, tm), lambda i: (0, i)),
            pl.BlockSpec((TOP_K, tm), lambda i: (0, i)),
        ],
        out_shape=[
            jax.ShapeDtypeStruct((T, D), F32),
            jax.ShapeDtypeStruct((T * (D // (2 * LANES)), LANES), jnp.uint32),
            jax.ShapeDtypeStruct((TOP_K, T), jnp.int32),
            jax.ShapeDtypeStruct((TOP_K, T), F32),
        ],
        compiler_params=_params("parallel"),
        name="outproj_ln_router",
    )(o2d, w_out_bf16, x2d, g, b, wr, rb)


def _routing_plan(eidxT, *, n_experts, tm):
    K, T = eidxT.shape
    n_slots = T * K
    n_tiles = n_slots // tm + n_experts
    e_flat = eidxT.reshape(-1)
    order = jnp.argsort(e_flat, stable=True).astype(jnp.int32)
    counts = jnp.sum((e_flat[:, None] == jnp.arange(n_experts, dtype=jnp.int32)[None, :]).astype(jnp.int32), axis=0)
    tiles_per_e = (counts + tm - 1) // tm
    tile_end = jnp.cumsum(tiles_per_e)
    tile_start = tile_end - tiles_per_e
    run_start = jnp.cumsum(counts) - counts
    total_tiles = tile_end[-1]

    tile_ids = jnp.arange(n_tiles, dtype=jnp.int32)
    tile_valid = tile_ids < total_tiles
    last_tile = jnp.maximum(total_tiles - 1, 0)
    tile_e = jnp.sum((jnp.minimum(tile_ids, last_tile)[:, None] >= tile_end[None, :]).astype(jnp.int32), axis=1)
    tile_e = jnp.minimum(tile_e, n_experts - 1)

    tile_off = (tile_ids - tile_start[tile_e]) * tm
    n_valid = jnp.where(tile_valid, jnp.clip(counts[tile_e] - tile_off, 0, tm), 0).astype(jnp.int32)
    r = jnp.arange(tm, dtype=jnp.int32)[None, :]
    src = jnp.clip((run_start[tile_e] + tile_off)[:, None] + r, 0, n_slots - 1)
    slots = jnp.where(r < n_valid[:, None], order[src], 0).astype(jnp.int32)
    return tile_e, n_valid, lax.rem(slots, T), slots


def _experts_kernel(te_ref, nv_ref, tok_hbm, slot_hbm, x_hbm, wg_ref, wu_ref, wd_ref, y_hbm,
                    tok_a, tok_b, slot_a, slot_b, xbuf, ybuf, wg_bf, wu_bf, wd_bf,
                    tok_sems, slot_sems, gather_sems, scatter_sems, *, n_sub):
    i = pl.program_id(0)
    last = pl.num_programs(0) - 1
    tm = xbuf.shape[1] // n_sub
    toks = (tok_a, tok_b)
    slots = (slot_a, slot_b)
    n_valid = nv_ref[i]

    @pl.when((i == 0) | (te_ref[i] != te_ref[jnp.maximum(i - 1, 0)]))
    def _():
        wg_bf[...] = wg_ref[...].astype(BF16)
        wu_bf[...] = wu_ref[...].astype(BF16)
        wd_bf[...] = wd_ref[...].astype(BF16)

    def tok_copy(tile, b):
        return pltpu.make_async_copy(tok_hbm.at[jnp.minimum(tile, last)], toks[b], tok_sems.at[b])

    def slot_copy(tile, b):
        return pltpu.make_async_copy(slot_hbm.at[jnp.minimum(tile, last)], slots[b], slot_sems.at[b])

    def start_gather(b):
        for r in range(tm):
            src = pl.multiple_of(toks[b][r], n_sub)
            pltpu.make_async_copy(x_hbm.at[pl.ds(src, n_sub), :], xbuf.at[b, pl.ds(r * n_sub, n_sub), :],
                                  gather_sems.at[b]).start(priority=r % 2)

    def wait_gather(b):
        pltpu.make_async_copy(x_hbm.at[pl.ds(0, tm * n_sub), :], xbuf.at[b], gather_sems.at[b]).wait()

    def start_scatter(b, n):
        def row(r):
            src = pl.multiple_of(r * n_sub, n_sub)
            dst = pl.multiple_of(slots[b][r], n_sub)
            pltpu.make_async_copy(ybuf.at[b, pl.ds(src, n_sub), :], y_hbm.at[pl.ds(dst, n_sub), :],
                                  scatter_sems.at[b]).start(priority=1)

        def group(g, carry):
            for u in range(ISSUE_UNROLL):
                row(g * ISSUE_UNROLL + u)
            return carry

        def single(r, carry):
            row(r)
            return carry

        n_groups = n // ISSUE_UNROLL
        lax.fori_loop(0, n_groups, group, 0)
        lax.fori_loop(n_groups * ISSUE_UNROLL, n, single, 0)

    def wait_scatter(b, n):
        @pl.when(n > 0)
        def _():
            rows = pl.multiple_of(n * n_sub, n_sub)
            pltpu.make_async_copy(ybuf.at[b, pl.ds(0, rows), :], y_hbm.at[pl.ds(0, rows), :],
                                  scatter_sems.at[b]).wait()

    def step(cur):
        nxt = 1 - cur

        @pl.when(i == 0)
        def _():
            first = tok_copy(0, cur)
            first.start()
            first.wait()
            start_gather(cur)
            tok_copy(1, nxt).start()
            slot_copy(0, cur).start()

        tok_copy(i + 2, cur).start()
        slot_copy(i + 1, nxt).start()
        wait_gather(cur)
        tok_copy(i + 1, nxt).wait()
        start_gather(nxt)

        x_lo, x_hi = _unpack_rows(xbuf.at[cur], n_sub)
        xb = jnp.concatenate([x_lo, x_hi], axis=1).astype(BF16)
        g = jnp.dot(xb, wg_bf[...], preferred_element_type=F32)
        u = jnp.dot(xb, wu_bf[...], preferred_element_type=F32)
        h = (g * _sigmoid(g) * u).astype(BF16)
        _pack_rows(jnp.dot(h, wd_bf[...], preferred_element_type=F32), ybuf.at[cur])

        @pl.when(i > 0)
        def _():
            wait_scatter(nxt, nv_ref[jnp.maximum(i - 1, 0)])

        slot_copy(i, cur).wait()
        start_scatter(cur, n_valid)

        @pl.when(i == last)
        def _():
            wait_scatter(cur, n_valid)
            wait_gather(nxt)
            tok_copy(i + 2, cur).wait()
            slot_copy(i + 1, nxt).wait()

    parity = lax.rem(i, 2)
    for b in range(2):
        pl.when(parity == b)(functools.partial(step, b))


def _routed_experts(x1p, tile_e, tile_rows, toks, slots, wg, wu, wd, *, layer, tm):
    D = wg.shape[2]
    F = wg.shape[3]
    S = D // (2 * LANES)
    T = x1p.shape[0] // S
    n_tiles = slots.shape[0]
    grid_spec = pltpu.PrefetchScalarGridSpec(
        num_scalar_prefetch=2,
        grid=(n_tiles,),
        in_specs=[
            pl.BlockSpec(memory_space=pl.ANY),
            pl.BlockSpec(memory_space=pl.ANY),
            pl.BlockSpec(memory_space=pl.ANY),
            pl.BlockSpec((None, None, D, F), lambda i, te, nv: (layer, te[i], 0, 0)),
            pl.BlockSpec((None, None, D, F), lambda i, te, nv: (layer, te[i], 0, 0)),
            pl.BlockSpec((None, None, F, D), lambda i, te, nv: (layer, te[i], 0, 0)),
        ],
        out_specs=pl.BlockSpec(memory_space=pl.ANY),
        scratch_shapes=[
            pltpu.SMEM((tm,), jnp.int32),
            pltpu.SMEM((tm,), jnp.int32),
            pltpu.SMEM((tm,), jnp.int32),
            pltpu.SMEM((tm,), jnp.int32),
            pltpu.VMEM((2, tm * S, LANES), jnp.uint32),
            pltpu.VMEM((2, tm * S, LANES), jnp.uint32),
            pltpu.VMEM((D, F), BF16),
            pltpu.VMEM((D, F), BF16),
            pltpu.VMEM((F, D), BF16),
            pltpu.SemaphoreType.DMA((2,)),
            pltpu.SemaphoreType.DMA((2,)),
            pltpu.SemaphoreType.DMA((2,)),
            pltpu.SemaphoreType.DMA((2,)),
        ],
    )
    return pl.pallas_call(
        functools.partial(_experts_kernel, n_sub=S),
        grid_spec=grid_spec,
        out_shape=jax.ShapeDtypeStruct((T * TOP_K * S, LANES), jnp.uint32),
        compiler_params=_params("arbitrary"),
        name="routed_experts",
    )(tile_e, tile_rows, toks * S, slots * S, x1p, wg, wu, wd)


def _combine_kernel(y_ref, w_ref, x1_ref, wsg_ref, wsu_ref, wsd_ref, g_ref, b_ref, out_ref, *, alpha):
    x1 = x1_ref[...]
    xb = x1.astype(BF16)
    gg = jnp.dot(xb, wsg_ref[...], preferred_element_type=F32)
    uu = jnp.dot(xb, wsu_ref[...], preferred_element_type=F32)
    hh = (gg * _sigmoid(gg) * uu).astype(BF16)
    shared = jnp.dot(hh, wsd_ref[...], preferred_element_type=F32)
    w = w_ref[...]
    routed = None
    for k in range(TOP_K):
        y_lo, y_hi = _unpack_rows(y_ref.at[k], y_ref.shape[1] // x1.shape[0])
        term = w[:, k:k + 1] * jnp.concatenate([y_lo, y_hi], axis=1)
        routed = term if routed is None else routed + term
    out_ref[...] = _layer_norm(alpha * x1 + (routed + shared), g_ref[...], b_ref[...])


def _combine(y_kt, w_tk, x1, wsg, wsu, wsd, g, b, *, alpha, tm):
    T, D = x1.shape
    Fs = wsg.shape[1]
    kern = functools.partial(_combine_kernel, alpha=alpha)
    const = lambda i: (0, 0)
    return pl.pallas_call(
        kern,
        grid=(T // tm,),
        in_specs=[
            pl.BlockSpec((TOP_K, tm * (D // (2 * LANES)), LANES), lambda i: (0, i, 0)),
            pl.BlockSpec((tm, TOP_K), lambda i: (i, 0)),
            pl.BlockSpec((tm, D), lambda i: (i, 0)),
            pl.BlockSpec((D, Fs), const),
            pl.BlockSpec((D, Fs), const),
            pl.BlockSpec((Fs, D), const),
            pl.BlockSpec((1, D), const),
            pl.BlockSpec((1, D), const),
        ],
        out_specs=pl.BlockSpec((tm, D), lambda i: (i, 0)),
        out_shape=jax.ShapeDtypeStruct((T, D), F32),
        compiler_params=_params("parallel"),
        name="shared_combine_ln",
    )(y_kt, w_tk, x1, wsg, wsu, wsd, g, b)


def _split_bf16(w):
    hi = w.astype(BF16)
    lo = (w - hi.astype(F32)).astype(BF16)
    return jnp.concatenate([hi, lo], axis=-1)


def _pick(n, pref):
    t = min(n, pref)
    while n % t:
        t //= 2
    return t


def _moe_layer(x_attn_in, o2d, w_out, ln1_g, ln1_b, w_router, router_bias, w_gate, w_up, w_down,
               ws_gate, ws_up, ws_down, ln2_g, ln2_b, *, alpha, layer):
    T, D = x_attn_in.shape
    E = w_router.shape[1]
    x1, x1p, eidxT, wgtT = _proj_ln_router(
        o2d, w_out.astype(BF16), x_attn_in, ln1_g[None, :], ln1_b[None, :],
        _split_bf16(w_router), router_bias[:, None], alpha=alpha, tm=_pick(T, 256))
    tm_e = _pick(T * TOP_K, 256)
    tile_e, tile_rows, toks, slots = _routing_plan(eidxT, n_experts=E, tm=tm_e)
    y = _routed_experts(x1p, tile_e, tile_rows, toks, slots, w_gate, w_up, w_down, layer=layer, tm=tm_e)
    return _combine(y.reshape(TOP_K, T * (D // (2 * LANES)), LANES), wgtT.T, x1, ws_gate.astype(BF16), ws_up.astype(BF16), ws_down.astype(BF16),
                    ln2_g[None, :], ln2_b[None, :], alpha=alpha, tm=_pick(T, 256))


def kernel(x, positions, ln1_g, ln1_b, ln2_g, ln2_b, da_w_in, da_w_out, da_lq1, da_lk1, da_lq2, da_lk2,
           da_subln_g, mb_w_in, mb_w_out, w_router, router_bias, w_gate, w_up, w_down, ws_gate, ws_up, ws_down):
    B, S, D = x.shape
    T = B * S
    depth = ln1_g.shape[0]
    alpha = (2 * depth) ** 0.25
    hd = HEAD_DIM
    cosf, sinf = _rope_tables(positions)
    xt = x.reshape(T, D)
    tm_proj = _pick(T, 1024)

    for i in range(depth):
        m = i // 2
        if i % 2 == 0:
            w_in = da_w_in[m]
            H = w_in.shape[1] // (6 * hd)
            qk_cols = 4 * H * hd
            tq = _pick(S, 512)
            qkv = _qkv_proj(xt, w_in.astype(BF16), cosf, sinf, rope_cols=qk_cols, q_cols=qk_cols // 2,
                            tm=tm_proj, tn=_pick(qk_cols // 2, 512))
            tk = _pick(S, 1024)
            lambda_init = 0.8 - 0.6 * math.exp(-0.3 * i)
            lam = (jnp.exp(jnp.sum(da_lq1[m].astype(F32) * da_lk1[m].astype(F32)))
                   - jnp.exp(jnp.sum(da_lq2[m].astype(F32) * da_lk2[m].astype(F32))) + lambda_init)
            o2d = _diff_attention(qkv, lam.reshape(1), da_subln_g[m][None, :], B=B, S=S, H=H, tq=tq,
                                  tk=tk, lambda_init=lambda_init)
            w_out = da_w_out[m]
        else:
            w_in = mb_w_in[m]
            H = w_in.shape[1] // (3 * hd)
            qk_cols = 2 * H * hd
            qkv = _qkv_proj(xt, w_in.astype(BF16), cosf, sinf, rope_cols=qk_cols, q_cols=qk_cols // 2,
                            tm=tm_proj, tn=_pick(qk_cols // 2, 512))
            o2d = _moba_attention(qkv, B=B, S=S, H=H, group=_pick(S // MB_BLOCK, MB_GROUP))
            w_out = mb_w_out[m]
        xt = _moe_layer(xt, o2d, w_out, ln1_g[i], ln1_b[i], w_router[i], router_bias[i], w_gate, w_up,
                        w_down, ws_gate[i], ws_up[i], ws_down[i], ln2_g[i], ln2_b[i], alpha=alpha, layer=i)
    return xt.reshape(B, S, D)
```

```python
import functools
import math

import jax
import jax.numpy as jnp
from jax import lax
from jax.experimental import pallas as pl
from jax.experimental.pallas import tpu as pltpu

F32 = jnp.float32
BF16 = jnp.bfloat16

HEAD_DIM = 128
ROPE_THETA = 500000.0
ROPE_DIMS = HEAD_DIM // 4
ROPE_HALF = ROPE_DIMS // 2
LN_EPS = 1e-5
MB_BLOCK = 256
MB_TOPK = 3
MB_GROUP = 4
TOP_K = 8
N_GROUPS = 8
TOPK_GROUPS = 4
ROUTED_SCALE = 2.5
MASK_VALUE = -1e30
LOG2_E = math.log2(math.e)

LANES = 128
SUBLANES = 8
MXU_COLS = 256
ISSUE_UNROLL = 8
VMEM_LIMIT_BYTES = 56 * 1024 * 1024

_NT = (((1,), (1,)), ((), ()))
_TN = (((0,), (0,)), ((), ()))


def _params(*sem):
    return pltpu.CompilerParams(dimension_semantics=sem, vmem_limit_bytes=VMEM_LIMIT_BYTES)


def _sigmoid(x):
    return 1.0 / (1.0 + jnp.exp(-x))


def _pack_rows(x, o_ref):
    tm, d = x.shape
    half = d // 2
    n_sub = half // LANES
    lo = pltpu.bitcast(x[:, :half].astype(BF16).astype(F32), jnp.uint32) >> 16
    hi = pltpu.bitcast(x[:, half:].astype(BF16).astype(F32), jnp.uint32) & jnp.uint32(0xFFFF0000)
    packed = hi | lo
    for s in range(n_sub):
        o_ref[pl.ds(s, tm, stride=n_sub), :] = packed[:, s * LANES:(s + 1) * LANES]


def _unpack_rows(p_ref, n_sub):
    tm = p_ref.shape[0] // n_sub
    p = jnp.concatenate([p_ref[pl.ds(s, tm, stride=n_sub), :] for s in range(n_sub)], axis=1)
    return pltpu.bitcast(p << 16, F32), pltpu.bitcast(p & jnp.uint32(0xFFFF0000), F32)


def _layer_norm(y, g, b):
    mu = jnp.mean(y, axis=-1, keepdims=True)
    d = y - mu
    var = jnp.mean(d * d, axis=-1, keepdims=True)
    return d * lax.rsqrt(var + LN_EPS) * g + b


def _qkv_kernel(x_ref, w_ref, cos_ref, sin_ref, o_ref, xb_ref, *, rope_blocks, q_blocks, q_scale):
    j = pl.program_id(1)

    @pl.when(j == 0)
    def _():
        xb_ref[...] = x_ref[...].astype(BF16)

    tn = w_ref.shape[1]
    rotary = j < rope_blocks
    cosf = jnp.where(rotary, cos_ref[...], 1.0)
    sinf = jnp.where(rotary, sin_ref[...], 0.0)
    scale = jnp.where(j < q_blocks, q_scale, 1.0).astype(F32)
    lane = lax.broadcasted_iota(jnp.int32, (1, LANES), 1)
    xb = xb_ref[...]
    for n in range(tn // MXU_COLS):
        acc = jnp.dot(xb, w_ref[:, n * MXU_COLS:(n + 1) * MXU_COLS], preferred_element_type=F32)
        for c in range(MXU_COLS // LANES):
            xc = acc[:, c * LANES:(c + 1) * LANES]
            partner = jnp.where(lane < ROPE_HALF,
                                pltpu.roll(xc, LANES - ROPE_HALF, 1),
                                pltpu.roll(xc, ROPE_HALF, 1))
            r = (xc * cosf + partner * sinf) * scale
            col = n * MXU_COLS + c * LANES
            o_ref[:, col:col + LANES] = r.astype(o_ref.dtype)


def _qkv_proj(x2d, w_bf16, cosf, sinf, *, rope_cols, q_cols, tm, tn):
    T, D = x2d.shape
    N = w_bf16.shape[1]
    kern = functools.partial(_qkv_kernel, rope_blocks=rope_cols // tn, q_blocks=q_cols // tn,
                             q_scale=HEAD_DIM ** -0.5 * LOG2_E)
    return pl.pallas_call(
        kern,
        grid=(T // tm, N // tn),
        in_specs=[
            pl.BlockSpec((tm, D), lambda i, j: (i, 0)),
            pl.BlockSpec((D, tn), lambda i, j: (0, j)),
            pl.BlockSpec((tm, LANES), lambda i, j: (i, 0)),
            pl.BlockSpec((tm, LANES), lambda i, j: (i, 0)),
        ],
        out_specs=pl.BlockSpec((tm, tn), lambda i, j: (i, j)),
        out_shape=jax.ShapeDtypeStruct((T, N), BF16),
        scratch_shapes=[pltpu.VMEM((tm, D), BF16)],
        compiler_params=_params("parallel", "arbitrary"),
        name="qkv_proj",
    )(x2d, w_bf16, cosf, sinf)


def _rope_tables(positions):
    inv_freq = ROPE_THETA ** (-jnp.arange(0, ROPE_DIMS, 2, dtype=F32) / ROPE_DIMS)
    ang = positions.astype(F32).reshape(-1)[:, None] * inv_freq
    cos, sin = jnp.cos(ang), jnp.sin(ang)
    T = ang.shape[0]
    rest = LANES - ROPE_DIMS
    cosf = jnp.concatenate([cos, cos, jnp.ones((T, rest), F32)], axis=1)
    sinf = jnp.concatenate([-sin, sin, jnp.zeros((T, rest), F32)], axis=1)
    return cosf, sinf


def _softmax_first(sT, v):
    m = jnp.max(sT, axis=0, keepdims=True)
    p = jnp.exp2(sT - m)
    l = jnp.sum(p, axis=0, keepdims=True)
    acc = lax.dot_general(v, p.astype(BF16), _TN, preferred_element_type=F32)
    return m, l, acc


def _softmax_update(sTs, vs, m_ref, l_ref, acc_ref):
    m_prev = m_ref[...]
    m_new = m_prev
    for sT in sTs:
        m_new = jnp.maximum(m_new, jnp.max(sT, axis=0, keepdims=True))
    alpha = jnp.exp2(m_prev - m_new)
    l = alpha * l_ref[...]
    pv = None
    for sT, v in zip(sTs, vs):
        p = jnp.exp2(sT - m_new)
        l = l + jnp.sum(p, axis=0, keepdims=True)
        d = lax.dot_general(v, p.astype(BF16), _TN, preferred_element_type=F32)
        pv = d if pv is None else pv + d
    l_ref[...] = l
    acc_ref[...] = alpha * acc_ref[...] + pv
    m_ref[...] = m_new


def _causal_mask(sT, key0, qry0):
    tk, tq = sT.shape
    key = key0 + lax.broadcasted_iota(jnp.int32, (tk, tq), 0)
    qry = qry0 + lax.broadcasted_iota(jnp.int32, (tk, tq), 1)
    return jnp.where(key <= qry, sT, MASK_VALUE)


def _da_kernel(lam_ref, q_ref, k_ref, v_ref, g_ref, o_ref, m0, l0, acc0, m1, l1, acc1, s_a, s_b,
               *, tq, tk, out_scale):
    qi = pl.program_id(2)
    hd = HEAD_DIM
    gd = (qi * tq) // tk
    state = ((m0, l0, acc0), (m1, l1, acc1))

    def scores(g, c):
        koff = pl.multiple_of(g * tk, tk)
        kblk = k_ref[pl.ds(koff, tk), c * hd:(c + 1) * hd]
        return lax.dot_general(kblk, q_ref[:, c * hd:(c + 1) * hd], _NT, preferred_element_type=F32)

    def values(g):
        voff = pl.multiple_of(g * tk, tk)
        return v_ref[pl.ds(voff, tk), :]

    last_tile = jnp.maximum(gd - 1, 0)

    def fill(s_ref, g_req):
        g = jnp.minimum(g_req, last_tile)
        for c in range(2):
            s_ref[c] = scores(g, c)

    def consume(s_ref, g):
        v = values(g)
        for c in range(2):
            _softmax_update([s_ref[c]], [v], *state[c])

    fill(s_a, 0)

    v_diag = values(gd)
    for c in range(2):
        m, l, acc = _softmax_first(_causal_mask(scores(gd, c), gd * tk, qi * tq), v_diag)
        m_ref, l_ref, acc_ref = state[c]
        m_ref[...] = m
        l_ref[...] = l
        acc_ref[...] = acc

    def pair(p, carry):
        g0 = 2 * p
        fill(s_b, g0 + 1)
        consume(s_a, g0)
        fill(s_a, g0 + 2)
        consume(s_b, g0 + 1)
        return carry

    lax.fori_loop(0, gd // 2, pair, 0)

    @pl.when(lax.rem(gd, 2) == 1)
    def _():
        consume(s_a, last_tile)

    lam = lam_ref[0]
    oT = acc0[...] / l0[...] - lam * (acc1[...] / l1[...])
    o = oT.T
    o = o * lax.rsqrt(jnp.mean(o * o, axis=-1, keepdims=True) + LN_EPS) * g_ref[...]
    o_ref[...] = (o * out_scale).astype(o_ref.dtype)


def _diff_attention(qkv, lam, subln_g, *, B, S, H, tq, tk, lambda_init):
    T = B * S
    nq = S // tq
    dv = 2 * HEAD_DIM
    kern = functools.partial(_da_kernel, tq=tq, tk=tk, out_scale=1.0 - lambda_init)
    return pl.pallas_call(
        kern,
        grid=(B, H, nq),
        in_specs=[
            pl.BlockSpec(memory_space=pltpu.SMEM),
            pl.BlockSpec((tq, dv), lambda b, h, i: (b * nq + i, h)),
            pl.BlockSpec((S, dv), lambda b, h, i: (b, H + h)),
            pl.BlockSpec((S, dv), lambda b, h, i: (b, 2 * H + h)),
            pl.BlockSpec((1, dv), lambda b, h, i: (0, 0)),
        ],
        out_specs=pl.BlockSpec((tq, dv), lambda b, h, i: (b * nq + i, h)),
        out_shape=jax.ShapeDtypeStruct((T, H * dv), BF16),
        scratch_shapes=2 * [pltpu.VMEM((1, tq), F32), pltpu.VMEM((1, tq), F32), pltpu.VMEM((dv, tq), F32)]
        + 2 * [pltpu.VMEM((2, tk, tq), F32)],
        compiler_params=_params("parallel", "parallel", "arbitrary"),
        name="diff_attention",
    )(lam, qkv, qkv, qkv, subln_g)


def _mb_kernel(q_ref, k_ref, v_ref, o_ref, kmean_ref, bias_ref, m_ref, l_ref, acc_ref, s_a, s_b, *, nb, group):
    j = pl.program_id(2)
    blk = MB_BLOCK

    @pl.when(j == 0)
    def _():
        for n in range(nb):
            kb = k_ref[n * blk:(n + 1) * blk, :].astype(F32)
            kmean_ref[n:n + 1, :] = jnp.sum(kb, axis=0, keepdims=True) * (1.0 / blk)

    q = q_ref[...]
    km = kmean_ref[...]
    km_hi = km.astype(BF16)
    km_lo = (km - km_hi.astype(F32)).astype(BF16)
    gate = (lax.dot_general(km_hi, q, _NT, preferred_element_type=F32)
            + lax.dot_general(km_lo, q, _NT, preferred_element_type=F32))
    bid = lax.broadcasted_iota(jnp.int32, gate.shape, 0)
    rem = jnp.where(bid < j, gate, -jnp.inf)
    sel = jnp.zeros(gate.shape, jnp.bool_)
    for _ in range(MB_TOPK):
        mx = jnp.max(rem, axis=0, keepdims=True)
        first = jnp.min(jnp.where(rem == mx, bid, nb), axis=0, keepdims=True)
        pick = (bid == first) & (mx > -jnp.inf)
        sel = sel | pick
        rem = jnp.where(bid == first, -jnp.inf, rem)
    bias = jnp.where(sel, 0.0, MASK_VALUE).astype(F32)
    for gg in range(nb // group):
        bias_ref[gg] = bias[gg * group:(gg + 1) * group, :]

    def scores(n):
        koff = pl.multiple_of(n * blk, blk)
        return lax.dot_general(k_ref[pl.ds(koff, blk), :], q, _NT, preferred_element_type=F32)

    def values(n):
        voff = pl.multiple_of(n * blk, blk)
        return v_ref[pl.ds(voff, blk), :]

    n_groups = (j + group - 1) // group
    last_group = jnp.maximum(n_groups - 1, 0)

    def fill(s_ref, g_req):
        g = jnp.minimum(g_req, last_group)
        gbias = bias_ref[g]
        for t in range(group):
            s_ref[t] = scores(g * group + t) + gbias[t:t + 1, :]

    def consume(s_ref, g):
        _softmax_update([s_ref[t] for t in range(group)], [values(g * group + t) for t in range(group)],
                        m_ref, l_ref, acc_ref)

    fill(s_a, 0)

    m, l, acc = _softmax_first(_causal_mask(scores(j), 0, 0), values(j))
    m_ref[...] = m
    l_ref[...] = l
    acc_ref[...] = acc

    def pair(p, carry):
        g0 = 2 * p
        fill(s_b, g0 + 1)
        consume(s_a, g0)
        fill(s_a, g0 + 2)
        consume(s_b, g0 + 1)
        return carry

    lax.fori_loop(0, n_groups // 2, pair, 0)

    @pl.when(lax.rem(n_groups, 2) == 1)
    def _():
        consume(s_a, last_group)

    oT = acc_ref[...] / l_ref[...]
    o_ref[...] = oT.T.astype(o_ref.dtype)


def _moba_attention(qkv, *, B, S, H, group):
    T = B * S
    blk = MB_BLOCK
    nb = S // blk
    hd = HEAD_DIM
    kern = functools.partial(_mb_kernel, nb=nb, group=group)
    return pl.pallas_call(
        kern,
        grid=(B, H, nb),
        in_specs=[
            pl.BlockSpec((blk, hd), lambda b, h, i: (b * nb + i, h)),
            pl.BlockSpec((S, hd), lambda b, h, i: (b, H + h)),
            pl.BlockSpec((S, hd), lambda b, h, i: (b, 2 * H + h)),
        ],
        out_specs=pl.BlockSpec((blk, hd), lambda b, h, i: (b * nb + i, h)),
        out_shape=jax.ShapeDtypeStruct((T, H * hd), BF16),
        scratch_shapes=[
            pltpu.VMEM((nb, hd), F32),
            pltpu.VMEM((nb // group, group, blk), F32),
            pltpu.VMEM((1, blk), F32),
            pltpu.VMEM((1, blk), F32),
            pltpu.VMEM((hd, blk), F32),
            pltpu.VMEM((group, blk, blk), F32),
            pltpu.VMEM((group, blk, blk), F32),
        ],
        compiler_params=_params("parallel", "parallel", "arbitrary"),
        name="moba_attention",
    )(qkv, qkv, qkv)


def _router(x1, wr_ref, rb_ref, eidx_ref, wgt_ref):
    E = wr_ref.shape[1] // 2
    per_group = E // N_GROUPS
    tm = x1.shape[0]
    x_hi = x1.astype(BF16)
    x_lo = (x1 - x_hi.astype(F32)).astype(BF16)
    parts = (jnp.dot(x_hi, wr_ref[...], preferred_element_type=F32)
             + jnp.dot(x_lo, wr_ref[...], preferred_element_type=F32))
    logits = (parts[:, :E] + parts[:, E:]).T
    scores = _sigmoid(logits)
    biased = scores + rb_ref[...]

    gid8 = lax.broadcasted_iota(jnp.int32, (per_group, tm), 0)
    gscores = []
    for g in range(N_GROUPS):
        bg = biased[g * per_group:(g + 1) * per_group, :]
        m1 = jnp.max(bg, axis=0, keepdims=True)
        i1 = jnp.min(jnp.where(bg == m1, gid8, per_group), axis=0, keepdims=True)
        m2 = jnp.max(jnp.where(gid8 == i1, -jnp.inf, bg), axis=0, keepdims=True)
        gscores.append(m1 + m2)
    gs = jnp.concatenate(gscores, axis=0)

    gidx = lax.broadcasted_iota(jnp.int32, (N_GROUPS, tm), 0)
    rank = jnp.zeros((N_GROUPS, tm), jnp.int32)
    for g in range(N_GROUPS):
        row = gs[g:g + 1, :]
        beats = (row > gs) | ((row == gs) & (g < gidx))
        rank = rank + beats.astype(jnp.int32)
    gsel = rank < TOPK_GROUPS

    masked = jnp.concatenate(
        [jnp.where(gsel[g:g + 1, :], biased[g * per_group:(g + 1) * per_group, :], -jnp.inf)
         for g in range(N_GROUPS)], axis=0)

    eid = lax.broadcasted_iota(jnp.int32, (E, tm), 0)
    rem = masked
    ids, vals = [], []
    for _ in range(TOP_K):
        mx = jnp.max(rem, axis=0, keepdims=True)
        first = jnp.min(jnp.where(rem == mx, eid, E), axis=0, keepdims=True)
        onehot = eid == first
        ids.append(first)
        vals.append(jnp.sum(jnp.where(onehot, scores, 0.0), axis=0, keepdims=True))
        rem = jnp.where(onehot, -jnp.inf, rem)
    w = jnp.concatenate(vals, axis=0)
    w = w / jnp.sum(w, axis=0, keepdims=True) * ROUTED_SCALE
    eidx_ref[...] = jnp.concatenate(ids, axis=0)
    wgt_ref[...] = w


def _proj_ln_router_kernel(o_ref, w_ref, x_ref, g_ref, b_ref, wr_ref, rb_ref,
                           x1_ref, x1p_ref, eidx_ref, wgt_ref, *, alpha):
    h = jnp.dot(o_ref[...], w_ref[...], preferred_element_type=F32)
    x1 = _layer_norm(alpha * x_ref[...] + h, g_ref[...], b_ref[...])
    x1_ref[...] = x1
    _pack_rows(x1, x1p_ref)
    _router(x1, wr_ref, rb_ref, eidx_ref, wgt_ref)


def _proj_ln_router(o2d, w_out_bf16, x2d, g, b, wr, rb, *, alpha, tm):
    T, D = x2d.shape
    Do = o2d.shape[1]
    E = wr.shape[1] // 2
    kern = functools.partial(_proj_ln_router_kernel, alpha=alpha)
    const = lambda i: (0, 0)
    return pl.pallas_call(
        kern,
        grid=(T // tm,),
        in_specs=[
            pl.BlockSpec((tm, Do), lambda i: (i, 0)),
            pl.BlockSpec((Do, D), const),
            pl.BlockSpec((tm, D), lambda i: (i, 0)),
            pl.BlockSpec((1, D), const),
            pl.BlockSpec((1, D), const),
            pl.BlockSpec((D, 2 * E), const),
            pl.BlockSpec((E, 1), const),
        ],
        out_specs=[
            pl.BlockSpec((tm, D), lambda i: (i, 0)),
            pl.BlockSpec((tm * (D // (2 * LANES)), LANES), lambda i: (i, 0)),
            pl.BlockSpec((TOP_K, tm), lambda i: (0, i)),
            pl.BlockSpec((TOP_K, tm), lambda i: (0, i)),
        ],
        out_shape=[
            jax.ShapeDtypeStruct((T, D), F32),
            jax.ShapeDtypeStruct((T * (D // (2 * LANES)), LANES), jnp.uint32),
            jax.ShapeDtypeStruct((TOP_K, T), jnp.int32),
            jax.ShapeDtypeStruct((TOP_K, T), F32),
        ],
        compiler_params=_params("parallel"),
        name="outproj_ln_router",
    )(o2d, w_out_bf16, x2d, g, b, wr, rb)


def _routing_plan(eidxT, *, n_experts, tm):
    K, T = eidxT.shape
    n_slots = T * K
    n_tiles = n_slots // tm + n_experts
    e_flat = eidxT.reshape(-1)
    order = jnp.argsort(e_flat, stable=True).astype(jnp.int32)
    counts = jnp.sum((e_flat[:, None] == jnp.arange(n_experts, dtype=jnp.int32)[None, :]).astype(jnp.int32), axis=0)
    tiles_per_e = (counts + tm - 1) // tm
    tile_end = jnp.cumsum(tiles_per_e)
    tile_start = tile_end - tiles_per_e
    run_start = jnp.cumsum(counts) - counts
    total_tiles = tile_end[-1]

    tile_ids = jnp.arange(n_tiles, dtype=jnp.int32)
    tile_valid = tile_ids < total_tiles
    last_tile = jnp.maximum(total_tiles - 1, 0)
    tile_e = jnp.sum((jnp.minimum(tile_ids, last_tile)[:, None] >= tile_end[None, :]).astype(jnp.int32), axis=1)
    tile_e = jnp.minimum(tile_e, n_experts - 1)

    tile_off = (tile_ids - tile_start[tile_e]) * tm
    n_valid = jnp.where(tile_valid, jnp.clip(counts[tile_e] - tile_off, 0, tm), 0).astype(jnp.int32)
    r = jnp.arange(tm, dtype=jnp.int32)[None, :]
    src = jnp.clip((run_start[tile_e] + tile_off)[:, None] + r, 0, n_slots - 1)
    slots = jnp.where(r < n_valid[:, None], order[src], 0).astype(jnp.int32)
    return tile_e, n_valid, lax.rem(slots, T), slots


def _experts_kernel(te_ref, nv_ref, tok_hbm, slot_hbm, x_hbm, wg_ref, wu_ref, wd_ref, y_hbm,
                    tok_a, tok_b, slot_a, slot_b, xbuf, ybuf, wg_bf, wu_bf, wd_bf,
                    tok_sems, slot_sems, gather_sems, scatter_sems, *, n_sub):
    i = pl.program_id(0)
    last = pl.num_programs(0) - 1
    tm = xbuf.shape[1] // n_sub
    toks = (tok_a, tok_b)
    slots = (slot_a, slot_b)
    n_valid = nv_ref[i]

    @pl.when((i == 0) | (te_ref[i] != te_ref[jnp.maximum(i - 1, 0)]))
    def _():
        wg_bf[...] = wg_ref[...].astype(BF16)
        wu_bf[...] = wu_ref[...].astype(BF16)
        wd_bf[...] = wd_ref[...].astype(BF16)

    def tok_copy(tile, b):
        return pltpu.make_async_copy(tok_hbm.at[jnp.minimum(tile, last)], toks[b], tok_sems.at[b])

    def slot_copy(tile, b):
        return pltpu.make_async_copy(slot_hbm.at[jnp.minimum(tile, last)], slots[b], slot_sems.at[b])

    def start_gather(b):
        for r in range(tm):
            src = pl.multiple_of(toks[b][r], n_sub)
            pltpu.make_async_copy(x_hbm.at[pl.ds(src, n_sub), :], xbuf.at[b, pl.ds(r * n_sub, n_sub), :],
                                  gather_sems.at[b]).start(priority=r % 2)

    def wait_gather(b):
        pltpu.make_async_copy(x_hbm.at[pl.ds(0, tm * n_sub), :], xbuf.at[b], gather_sems.at[b]).wait()

    def start_scatter(b, n):
        def row(r):
            src = pl.multiple_of(r * n_sub, n_sub)
            dst = pl.multiple_of(slots[b][r], n_sub)
            pltpu.make_async_copy(ybuf.at[b, pl.ds(src, n_sub), :], y_hbm.at[pl.ds(dst, n_sub), :],
                                  scatter_sems.at[b]).start(priority=1)

        def group(g, carry):
            for u in range(ISSUE_UNROLL):
                row(g * ISSUE_UNROLL + u)
            return carry

        def single(r, carry):
            row(r)
            return carry

        n_groups = n // ISSUE_UNROLL
        lax.fori_loop(0, n_groups, group, 0)
        lax.fori_loop(n_groups * ISSUE_UNROLL, n, single, 0)

    def wait_scatter(b, n):
        @pl.when(n > 0)
        def _():
            rows = pl.multiple_of(n * n_sub, n_sub)
            pltpu.make_async_copy(ybuf.at[b, pl.ds(0, rows), :], y_hbm.at[pl.ds(0, rows), :],
                                  scatter_sems.at[b]).wait()

    def step(cur):
        nxt = 1 - cur

        @pl.when(i == 0)
        def _():
            first = tok_copy(0, cur)
            first.start()
            first.wait()
            start_gather(cur)
            tok_copy(1, nxt).start()
            slot_copy(0, cur).start()

        tok_copy(i + 2, cur).start()
        slot_copy(i + 1, nxt).start()
        wait_gather(cur)
        tok_copy(i + 1, nxt).wait()
        start_gather(nxt)

        x_lo, x_hi = _unpack_rows(xbuf.at[cur], n_sub)
        xb = jnp.concatenate([x_lo, x_hi], axis=1).astype(BF16)
        g = jnp.dot(xb, wg_bf[...], preferred_element_type=F32)
        u = jnp.dot(xb, wu_bf[...], preferred_element_type=F32)
        h = (g * _sigmoid(g) * u).astype(BF16)
        _pack_rows(jnp.dot(h, wd_bf[...], preferred_element_type=F32), ybuf.at[cur])

        @pl.when(i > 0)
        def _():
            wait_scatter(nxt, nv_ref[jnp.maximum(i - 1, 0)])

        slot_copy(i, cur).wait()
        start_scatter(cur, n_valid)

        @pl.when(i == last)
        def _():
            wait_scatter(cur, n_valid)
            wait_gather(nxt)
            tok_copy(i + 2, cur).wait()
            slot_copy(i + 1, nxt).wait()

    parity = lax.rem(i, 2)
    for b in range(2):
        pl.when(parity == b)(functools.partial(step, b))


def _routed_experts(x1p, tile_e, tile_rows, toks, slots, wg, wu, wd, *, layer, tm):
    D = wg.shape[2]
    F = wg.shape[3]
    S = D // (2 * LANES)
    T = x1p.shape[0] // S
    n_tiles = slots.shape[0]
    grid_spec = pltpu.PrefetchScalarGridSpec(
        num_scalar_prefetch=2,
        grid=(n_tiles,),
        in_specs=[
            pl.BlockSpec(memory_space=pl.ANY),
            pl.BlockSpec(memory_space=pl.ANY),
            pl.BlockSpec(memory_space=pl.ANY),
            pl.BlockSpec((None, None, D, F), lambda i, te, nv: (layer, te[i], 0, 0)),
            pl.BlockSpec((None, None, D, F), lambda i, te, nv: (layer, te[i], 0, 0)),
            pl.BlockSpec((None, None, F, D), lambda i, te, nv: (layer, te[i], 0, 0)),
        ],
        out_specs=pl.BlockSpec(memory_space=pl.ANY),
        scratch_shapes=[
            pltpu.SMEM((tm,), jnp.int32),
            pltpu.SMEM((tm,), jnp.int32),
            pltpu.SMEM((tm,), jnp.int32),
            pltpu.SMEM((tm,), jnp.int32),
            pltpu.VMEM((2, tm * S, LANES), jnp.uint32),
            pltpu.VMEM((2, tm * S, LANES), jnp.uint32),
            pltpu.VMEM((D, F), BF16),
            pltpu.VMEM((D, F), BF16),
            pltpu.VMEM((F, D), BF16),
            pltpu.SemaphoreType.DMA((2,)),
            pltpu.SemaphoreType.DMA((2,)),
            pltpu.SemaphoreType.DMA((2,)),
            pltpu.SemaphoreType.DMA((2,)),
        ],
    )
    return pl.pallas_call(
        functools.partial(_experts_kernel, n_sub=S),
        grid_spec=grid_spec,
        out_shape=jax.ShapeDtypeStruct((T * TOP_K * S, LANES), jnp.uint32),
        compiler_params=_params("arbitrary"),
        name="routed_experts",
    )(tile_e, tile_rows, toks * S, slots * S, x1p, wg, wu, wd)


def _combine_kernel(y_ref, w_ref, x1_ref, wsg_ref, wsu_ref, wsd_ref, g_ref, b_ref, out_ref, *, alpha):
    x1 = x1_ref[...]
    xb = x1.astype(BF16)
    gg = jnp.dot(xb, wsg_ref[...], preferred_element_type=F32)
    uu = jnp.dot(xb, wsu_ref[...], preferred_element_type=F32)
    hh = (gg * _sigmoid(gg) * uu).astype(BF16)
    shared = jnp.dot(hh, wsd_ref[...], preferred_element_type=F32)
    w = w_ref[...]
    routed = None
    for k in range(TOP_K):
        y_lo, y_hi = _unpack_rows(y_ref.at[k], y_ref.shape[1] // x1.shape[0])
        term = w[:, k:k + 1] * jnp.concatenate([y_lo, y_hi], axis=1)
        routed = term if routed is None else routed + term
    out_ref[...] = _layer_norm(alpha * x1 + (routed + shared), g_ref[...], b_ref[...])


def _combine(y_kt, w_tk, x1, wsg, wsu, wsd, g, b, *, alpha, tm):
    T, D = x1.shape
    Fs = wsg.shape[1]
    kern = functools.partial(_combine_kernel, alpha=alpha)
    const = lambda i: (0, 0)
    return pl.pallas_call(
        kern,
        grid=(T // tm,),
        in_specs=[
            pl.BlockSpec((TOP_K, tm * (D // (2 * LANES)), LANES), lambda i: (0, i, 0)),
            pl.BlockSpec((tm, TOP_K), lambda i: (i, 0)),
            pl.BlockSpec((tm, D), lambda i: (i, 0)),
            pl.BlockSpec((D, Fs), const),
            pl.BlockSpec((D, Fs), const),
            pl.BlockSpec((Fs, D), const),
            pl.BlockSpec((1, D), const),
            pl.BlockSpec((1, D), const),
        ],
        out_specs=pl.BlockSpec((tm, D), lambda i: (i, 0)),
        out_shape=jax.ShapeDtypeStruct((T, D), F32),
        compiler_params=_params("parallel"),
        name="shared_combine_ln",
    )(y_kt, w_tk, x1, wsg, wsu, wsd, g, b)


def _split_bf16(w):
    hi = w.astype(BF16)
    lo = (w - hi.astype(F32)).astype(BF16)
    return jnp.concatenate([hi, lo], axis=-1)


def _pick(n, pref):
    t = min(n, pref)
    while n % t:
        t //= 2
    return t


def _moe_layer(x_attn_in, o2d, w_out, ln1_g, ln1_b, w_router, router_bias, w_gate, w_up, w_down,
               ws_gate, ws_up, ws_down, ln2_g, ln2_b, *, alpha, layer):
    T, D = x_attn_in.shape
    E = w_router.shape[1]
    x1, x1p, eidxT, wgtT = _proj_ln_router(
        o2d, w_out.astype(BF16), x_attn_in, ln1_g[None, :], ln1_b[None, :],
        _split_bf16(w_router), router_bias[:, None], alpha=alpha, tm=_pick(T, 256))
    tm_e = _pick(T * TOP_K, 256)
    tile_e, tile_rows, toks, slots = _routing_plan(eidxT, n_experts=E, tm=tm_e)
    y = _routed_experts(x1p, tile_e, tile_rows, toks, slots, w_gate, w_up, w_down, layer=layer, tm=tm_e)
    return _combine(y.reshape(TOP_K, T * (D // (2 * LANES)), LANES), wgtT.T, x1, ws_gate.astype(BF16), ws_up.astype(BF16), ws_down.astype(BF16),
                    ln2_g[None, :], ln2_b[None, :], alpha=alpha, tm=_pick(T, 256))


def kernel(x, positions, ln1_g, ln1_b, ln2_g, ln2_b, da_w_in, da_w_out, da_lq1, da_lk1, da_lq2, da_lk2,
           da_subln_g, mb_w_in, mb_w_out, w_router, router_bias, w_gate, w_up, w_down, ws_gate, ws_up, ws_down):
    B, S, D = x.shape
    T = B * S
    depth = ln1_g.shape[0]
    alpha = (2 * depth) ** 0.25
    hd = HEAD_DIM
    cosf, sinf = _rope_tables(positions)
    xt = x.reshape(T, D)
    tm_proj = _pick(T, 1024)

    for i in range(depth):
        m = i // 2
        if i % 2 == 0:
            w_in = da_w_in[m]
            H = w_in.shape[1] // (6 * hd)
            qk_cols = 4 * H * hd
            tq = _pick(S, 512)
            qkv = _qkv_proj(xt, w_in.astype(BF16), cosf, sinf, rope_cols=qk_cols, q_cols=qk_cols // 2,
                            tm=tm_proj, tn=_pick(qk_cols // 2, 1024))
            tk = _pick(S, 1024)
            lambda_init = 0.8 - 0.6 * math.exp(-0.3 * i)
            lam = (jnp.exp(jnp.sum(da_lq1[m].astype(F32) * da_lk1[m].astype(F32)))
                   - jnp.exp(jnp.sum(da_lq2[m].astype(F32) * da_lk2[m].astype(F32))) + lambda_init)
            o2d = _diff_attention(qkv, lam.reshape(1), da_subln_g[m][None, :], B=B, S=S, H=H, tq=tq,
                                  tk=tk, lambda_init=lambda_init)
            w_out = da_w_out[m]
        else:
            w_in = mb_w_in[m]
            H = w_in.shape[1] // (3 * hd)
            qk_cols = 2 * H * hd
            qkv = _qkv_proj(xt, w_in.astype(BF16), cosf, sinf, rope_cols=qk_cols, q_cols=qk_cols // 2,
                            tm=tm_proj, tn=_pick(qk_cols // 2, 1024))
            o2d = _moba_attention(qkv, B=B, S=S, H=H, group=_pick(S // MB_BLOCK, MB_GROUP))
            w_out = mb_w_out[m]
        xt = _moe_layer(xt, o2d, w_out, ln1_g[i], ln1_b[i], w_router[i], router_bias[i], w_gate, w_up,
                        w_down, ws_gate[i], ws_up[i], ws_down[i], ln2_g[i], ln2_b[i], alpha=alpha, layer=i)
    return xt.reshape(B, S, D)
```

```python
import functools
import math

import jax
import jax.numpy as jnp
from jax import lax
from jax.experimental import pallas as pl
from jax.experimental.pallas import tpu as pltpu

F32 = jnp.float32
BF16 = jnp.bfloat16

HEAD_DIM = 128
ROPE_THETA = 500000.0
ROPE_DIMS = HEAD_DIM // 4
ROPE_HALF = ROPE_DIMS // 2
LN_EPS = 1e-5
MB_BLOCK = 256
MB_TOPK = 3
MB_GROUP = 4
TOP_K = 8
N_GROUPS = 8
TOPK_GROUPS = 4
ROUTED_SCALE = 2.5
MASK_VALUE = -1e30
LOG2_E = math.log2(math.e)

LANES = 128
SUBLANES = 8
MXU_COLS = 256
ISSUE_UNROLL = 8
VMEM_LIMIT_BYTES = 56 * 1024 * 1024

_NT = (((1,), (1,)), ((), ()))
_TN = (((0,), (0,)), ((), ()))


def _params(*sem):
    return pltpu.CompilerParams(dimension_semantics=sem, vmem_limit_bytes=VMEM_LIMIT_BYTES)


def _sigmoid(x):
    return 1.0 / (1.0 + jnp.exp(-x))


def _pack_rows(x, o_ref):
    tm, d = x.shape
    half = d // 2
    n_sub = half // LANES
    lo = pltpu.bitcast(x[:, :half].astype(BF16).astype(F32), jnp.uint32) >> 16
    hi = pltpu.bitcast(x[:, half:].astype(BF16).astype(F32), jnp.uint32) & jnp.uint32(0xFFFF0000)
    packed = hi | lo
    for s in range(n_sub):
        o_ref[pl.ds(s, tm, stride=n_sub), :] = packed[:, s * LANES:(s + 1) * LANES]


def _unpack_rows(p_ref, n_sub):
    tm = p_ref.shape[0] // n_sub
    p = jnp.concatenate([p_ref[pl.ds(s, tm, stride=n_sub), :] for s in range(n_sub)], axis=1)
    return pltpu.bitcast(p << 16, F32), pltpu.bitcast(p & jnp.uint32(0xFFFF0000), F32)


def _layer_norm(y, g, b):
    mu = jnp.mean(y, axis=-1, keepdims=True)
    d = y - mu
    var = jnp.mean(d * d, axis=-1, keepdims=True)
    return d * lax.rsqrt(var + LN_EPS) * g + b


def _qkv_kernel(x_ref, w_ref, cos_ref, sin_ref, o_ref, xb_ref, *, rope_blocks, q_blocks, q_scale):
    j = pl.program_id(1)

    @pl.when(j == 0)
    def _():
        xb_ref[...] = x_ref[...].astype(BF16)

    tn = w_ref.shape[1]
    rotary = j < rope_blocks
    cosf = jnp.where(rotary, cos_ref[...], 1.0)
    sinf = jnp.where(rotary, sin_ref[...], 0.0)
    scale = jnp.where(j < q_blocks, q_scale, 1.0).astype(F32)
    lane = lax.broadcasted_iota(jnp.int32, (1, LANES), 1)
    xb = xb_ref[...]
    for n in range(tn // MXU_COLS):
        acc = jnp.dot(xb, w_ref[:, n * MXU_COLS:(n + 1) * MXU_COLS], preferred_element_type=F32)
        for c in range(MXU_COLS // LANES):
            xc = acc[:, c * LANES:(c + 1) * LANES]
            partner = jnp.where(lane < ROPE_HALF,
                                pltpu.roll(xc, LANES - ROPE_HALF, 1),
                                pltpu.roll(xc, ROPE_HALF, 1))
            r = (xc * cosf + partner * sinf) * scale
            col = n * MXU_COLS + c * LANES
            o_ref[:, col:col + LANES] = r.astype(o_ref.dtype)


def _qkv_proj(x2d, w_bf16, cosf, sinf, *, rope_cols, q_cols, tm, tn):
    T, D = x2d.shape
    N = w_bf16.shape[1]
    kern = functools.partial(_qkv_kernel, rope_blocks=rope_cols // tn, q_blocks=q_cols // tn,
                             q_scale=HEAD_DIM ** -0.5 * LOG2_E)
    return pl.pallas_call(
        kern,
        grid=(T // tm, N // tn),
        in_specs=[
            pl.BlockSpec((tm, D), lambda i, j: (i, 0)),
            pl.BlockSpec((D, tn), lambda i, j: (0, j)),
            pl.BlockSpec((tm, LANES), lambda i, j: (i, 0)),
            pl.BlockSpec((tm, LANES), lambda i, j: (i, 0)),
        ],
        out_specs=pl.BlockSpec((tm, tn), lambda i, j: (i, j)),
        out_shape=jax.ShapeDtypeStruct((T, N), BF16),
        scratch_shapes=[pltpu.VMEM((tm, D), BF16)],
        compiler_params=_params("parallel", "arbitrary"),
        name="qkv_proj",
    )(x2d, w_bf16, cosf, sinf)


def _rope_tables(positions):
    inv_freq = ROPE_THETA ** (-jnp.arange(0, ROPE_DIMS, 2, dtype=F32) / ROPE_DIMS)
    ang = positions.astype(F32).reshape(-1)[:, None] * inv_freq
    cos, sin = jnp.cos(ang), jnp.sin(ang)
    T = ang.shape[0]
    rest = LANES - ROPE_DIMS
    cosf = jnp.concatenate([cos, cos, jnp.ones((T, rest), F32)], axis=1)
    sinf = jnp.concatenate([-sin, sin, jnp.zeros((T, rest), F32)], axis=1)
    return cosf, sinf


def _softmax_first(sT, v):
    m = jnp.max(sT, axis=0, keepdims=True)
    p = jnp.exp2(sT - m)
    l = jnp.sum(p, axis=0, keepdims=True)
    acc = lax.dot_general(v, p.astype(BF16), _TN, preferred_element_type=F32)
    return m, l, acc


def _softmax_update(sTs, vs, m_ref, l_ref, acc_ref):
    m_prev = m_ref[...]
    m_new = m_prev
    for sT in sTs:
        m_new = jnp.maximum(m_new, jnp.max(sT, axis=0, keepdims=True))
    alpha = jnp.exp2(m_prev - m_new)
    l = alpha * l_ref[...]
    pv = None
    for sT, v in zip(sTs, vs):
        p = jnp.exp2(sT - m_new)
        l = l + jnp.sum(p, axis=0, keepdims=True)
        d = lax.dot_general(v, p.astype(BF16), _TN, preferred_element_type=F32)
        pv = d if pv is None else pv + d
    l_ref[...] = l
    acc_ref[...] = alpha * acc_ref[...] + pv
    m_ref[...] = m_new


def _causal_mask(sT, key0, qry0):
    tk, tq = sT.shape
    key = key0 + lax.broadcasted_iota(jnp.int32, (tk, tq), 0)
    qry = qry0 + lax.broadcasted_iota(jnp.int32, (tk, tq), 1)
    return jnp.where(key <= qry, sT, MASK_VALUE)


def _da_kernel(lam_ref, q_ref, k_ref, v_ref, g_ref, o_ref, m0, l0, acc0, m1, l1, acc1, s_a, s_b,
               *, tq, tk, out_scale):
    qi = pl.program_id(2)
    hd = HEAD_DIM
    gd = (qi * tq) // tk
    state = ((m0, l0, acc0), (m1, l1, acc1))

    def scores(g, c):
        koff = pl.multiple_of(g * tk, tk)
        kblk = k_ref[pl.ds(koff, tk), c * hd:(c + 1) * hd]
        return lax.dot_general(kblk, q_ref[:, c * hd:(c + 1) * hd], _NT, preferred_element_type=F32)

    def values(g):
        voff = pl.multiple_of(g * tk, tk)
        return v_ref[pl.ds(voff, tk), :]

    last_tile = jnp.maximum(gd - 1, 0)

    def fill(s_ref, g_req):
        g = jnp.minimum(g_req, last_tile)
        for c in range(2):
            s_ref[c] = scores(g, c)

    def consume(s_ref, g):
        v = values(g)
        for c in range(2):
            _softmax_update([s_ref[c]], [v], *state[c])

    fill(s_a, 0)

    v_diag = values(gd)
    for c in range(2):
        m, l, acc = _softmax_first(_causal_mask(scores(gd, c), gd * tk, qi * tq), v_diag)
        m_ref, l_ref, acc_ref = state[c]
        m_ref[...] = m
        l_ref[...] = l
        acc_ref[...] = acc

    def pair(p, carry):
        g0 = 2 * p
        fill(s_b, g0 + 1)
        consume(s_a, g0)
        fill(s_a, g0 + 2)
        consume(s_b, g0 + 1)
        return carry

    lax.fori_loop(0, gd // 2, pair, 0)

    @pl.when(lax.rem(gd, 2) == 1)
    def _():
        consume(s_a, last_tile)

    lam = lam_ref[0]
    oT = acc0[...] / l0[...] - lam * (acc1[...] / l1[...])
    o = oT.T
    o = o * lax.rsqrt(jnp.mean(o * o, axis=-1, keepdims=True) + LN_EPS) * g_ref[...]
    o_ref[...] = (o * out_scale).astype(o_ref.dtype)


def _diff_attention(qkv, lam, subln_g, *, B, S, H, tq, tk, lambda_init):
    T = B * S
    nq = S // tq
    dv = 2 * HEAD_DIM
    kern = functools.partial(_da_kernel, tq=tq, tk=tk, out_scale=1.0 - lambda_init)
    return pl.pallas_call(
        kern,
        grid=(B, H, nq),
        in_specs=[
            pl.BlockSpec(memory_space=pltpu.SMEM),
            pl.BlockSpec((tq, dv), lambda b, h, i: (b * nq + i, h)),
            pl.BlockSpec((S, dv), lambda b, h, i: (b, H + h)),
            pl.BlockSpec((S, dv), lambda b, h, i: (b, 2 * H + h)),
            pl.BlockSpec((1, dv), lambda b, h, i: (0, 0)),
        ],
        out_specs=pl.BlockSpec((tq, dv), lambda b, h, i: (b * nq + i, h)),
        out_shape=jax.ShapeDtypeStruct((T, H * dv), BF16),
        scratch_shapes=2 * [pltpu.VMEM((1, tq), F32), pltpu.VMEM((1, tq), F32), pltpu.VMEM((dv, tq), F32)]
        + 2 * [pltpu.VMEM((2, tk, tq), F32)],
        compiler_params=_params("parallel", "parallel", "arbitrary"),
        name="diff_attention",
    )(lam, qkv, qkv, qkv, subln_g)


def _mb_kernel(q_ref, k_ref, v_ref, o_ref, kmean_ref, bias_ref, m_ref, l_ref, acc_ref, s_a, s_b, *, nb, group):
    j = pl.program_id(2)
    blk = MB_BLOCK

    @pl.when(j == 0)
    def _():
        for n in range(nb):
            kb = k_ref[n * blk:(n + 1) * blk, :].astype(F32)
            kmean_ref[n:n + 1, :] = jnp.sum(kb, axis=0, keepdims=True) * (1.0 / blk)

    q = q_ref[...]
    km = kmean_ref[...]
    km_hi = km.astype(BF16)
    km_lo = (km - km_hi.astype(F32)).astype(BF16)
    gate = (lax.dot_general(km_hi, q, _NT, preferred_element_type=F32)
            + lax.dot_general(km_lo, q, _NT, preferred_element_type=F32))
    bid = lax.broadcasted_iota(jnp.int32, gate.shape, 0)
    rem = jnp.where(bid < j, gate, -jnp.inf)
    sel = jnp.zeros(gate.shape, jnp.bool_)
    for _ in range(MB_TOPK):
        mx = jnp.max(rem, axis=0, keepdims=True)
        first = jnp.min(jnp.where(rem == mx, bid, nb), axis=0, keepdims=True)
        pick = (bid == first) & (mx > -jnp.inf)
        sel = sel | pick
        rem = jnp.where(bid == first, -jnp.inf, rem)
    bias = jnp.where(sel, 0.0, MASK_VALUE).astype(F32)
    for gg in range(nb // group):
        bias_ref[gg] = bias[gg * group:(gg + 1) * group, :]

    def scores(n):
        koff = pl.multiple_of(n * blk, blk)
        return lax.dot_general(k_ref[pl.ds(koff, blk), :], q, _NT, preferred_element_type=F32)

    def values(n):
        voff = pl.multiple_of(n * blk, blk)
        return v_ref[pl.ds(voff, blk), :]

    n_groups = (j + group - 1) // group
    last_group = jnp.maximum(n_groups - 1, 0)

    def fill(s_ref, g_req):
        g = jnp.minimum(g_req, last_group)
        gbias = bias_ref[g]
        for t in range(group):
            s_ref[t] = scores(g * group + t) + gbias[t:t + 1, :]

    def consume(s_ref, g):
        _softmax_update([s_ref[t] for t in range(group)], [values(g * group + t) for t in range(group)],
                        m_ref, l_ref, acc_ref)

    fill(s_a, 0)

    m, l, acc = _softmax_first(_causal_mask(scores(j), 0, 0), values(j))
    m_ref[...] = m
    l_ref[...] = l
    acc_ref[...] = acc

    def pair(p, carry):
        g0 = 2 * p
        fill(s_b, g0 + 1)
        consume(s_a, g0)
        fill(s_a, g0 + 2)
        consume(s_b, g0 + 1)
        return carry

    lax.fori_loop(0, n_groups // 2, pair, 0)

    @pl.when(lax.rem(n_groups, 2) == 1)
    def _():
        consume(s_a, last_group)

    oT = acc_ref[...] / l_ref[...]
    o_ref[...] = oT.T.astype(o_ref.dtype)


def _moba_attention(qkv, *, B, S, H, group):
    T = B * S
    blk = MB_BLOCK
    nb = S // blk
    hd = HEAD_DIM
    kern = functools.partial(_mb_kernel, nb=nb, group=group)
    return pl.pallas_call(
        kern,
        grid=(B, H, nb),
        in_specs=[
            pl.BlockSpec((blk, hd), lambda b, h, i: (b * nb + i, h)),
            pl.BlockSpec((S, hd), lambda b, h, i: (b, H + h)),
            pl.BlockSpec((S, hd), lambda b, h, i: (b, 2 * H + h)),
        ],
        out_specs=pl.BlockSpec((blk, hd), lambda b, h, i: (b * nb + i, h)),
        out_shape=jax.ShapeDtypeStruct((T, H * hd), BF16),
        scratch_shapes=[
            pltpu.VMEM((nb, hd), F32),
            pltpu.VMEM((nb // group, group, blk), F32),
            pltpu.VMEM((1, blk), F32),
            pltpu.VMEM((1, blk), F32),
            pltpu.VMEM((hd, blk), F32),
            pltpu.VMEM((group, blk, blk), F32),
            pltpu.VMEM((group, blk, blk), F32),
        ],
        compiler_params=_params("parallel", "parallel", "arbitrary"),
        name="moba_attention",
    )(qkv, qkv, qkv)


def _router(x1, wr_ref, rb_ref, eidx_ref, wgt_ref, cnt_ref):
    E = wr_ref.shape[1] // 2
    per_group = E // N_GROUPS
    tm = x1.shape[0]
    x_hi = x1.astype(BF16)
    x_lo = (x1 - x_hi.astype(F32)).astype(BF16)
    parts = (jnp.dot(x_hi, wr_ref[...], preferred_element_type=F32)
             + jnp.dot(x_lo, wr_ref[...], preferred_element_type=F32))
    logits = (parts[:, :E] + parts[:, E:]).T
    scores = _sigmoid(logits)
    biased = scores + rb_ref[...]

    gid8 = lax.broadcasted_iota(jnp.int32, (per_group, tm), 0)
    gscores = []
    for g in range(N_GROUPS):
        bg = biased[g * per_group:(g + 1) * per_group, :]
        m1 = jnp.max(bg, axis=0, keepdims=True)
        i1 = jnp.min(jnp.where(bg == m1, gid8, per_group), axis=0, keepdims=True)
        m2 = jnp.max(jnp.where(gid8 == i1, -jnp.inf, bg), axis=0, keepdims=True)
        gscores.append(m1 + m2)
    gs = jnp.concatenate(gscores, axis=0)

    gidx = lax.broadcasted_iota(jnp.int32, (N_GROUPS, tm), 0)
    rank = jnp.zeros((N_GROUPS, tm), jnp.int32)
    for g in range(N_GROUPS):
        row = gs[g:g + 1, :]
        beats = (row > gs) | ((row == gs) & (g < gidx))
        rank = rank + beats.astype(jnp.int32)
    gsel = rank < TOPK_GROUPS

    masked = jnp.concatenate(
        [jnp.where(gsel[g:g + 1, :], biased[g * per_group:(g + 1) * per_group, :], -jnp.inf)
         for g in range(N_GROUPS)], axis=0)

    eid = lax.broadcasted_iota(jnp.int32, (E, tm), 0)
    rem = masked
    ids, vals = [], []
    picked = jnp.zeros((E, tm), jnp.int32)
    for _ in range(TOP_K):
        mx = jnp.max(rem, axis=0, keepdims=True)
        first = jnp.min(jnp.where(rem == mx, eid, E), axis=0, keepdims=True)
        onehot = eid == first
        ids.append(first)
        vals.append(jnp.sum(jnp.where(onehot, scores, 0.0), axis=0, keepdims=True))
        rem = jnp.where(onehot, -jnp.inf, rem)
        picked = picked + onehot.astype(jnp.int32)
    cnt_ref[...] = jnp.sum(picked, axis=1, keepdims=True)
    w = jnp.concatenate(vals, axis=0)
    w = w / jnp.sum(w, axis=0, keepdims=True) * ROUTED_SCALE
    eidx_ref[...] = jnp.concatenate(ids, axis=0)
    wgt_ref[...] = w


def _proj_ln_router_kernel(o_ref, w_ref, x_ref, g_ref, b_ref, wr_ref, rb_ref,
                           x1_ref, x1p_ref, eidx_ref, wgt_ref, cnt_ref, *, alpha):
    h = jnp.dot(o_ref[...], w_ref[...], preferred_element_type=F32)
    x1 = _layer_norm(alpha * x_ref[...] + h, g_ref[...], b_ref[...])
    x1_ref[...] = x1
    _pack_rows(x1, x1p_ref)
    _router(x1, wr_ref, rb_ref, eidx_ref, wgt_ref, cnt_ref)


def _proj_ln_router(o2d, w_out_bf16, x2d, g, b, wr, rb, *, alpha, tm):
    T, D = x2d.shape
    Do = o2d.shape[1]
    E = wr.shape[1] // 2
    kern = functools.partial(_proj_ln_router_kernel, alpha=alpha)
    const = lambda i: (0, 0)
    return pl.pallas_call(
        kern,
        grid=(T // tm,),
        in_specs=[
            pl.BlockSpec((tm, Do), lambda i: (i, 0)),
            pl.BlockSpec((Do, D), const),
            pl.BlockSpec((tm, D), lambda i: (i, 0)),
            pl.BlockSpec((1, D), const),
            pl.BlockSpec((1, D), const),
            pl.BlockSpec((D, 2 * E), const),
            pl.BlockSpec((E, 1), const),
        ],
        out_specs=[
            pl.BlockSpec((tm, D), lambda i: (i, 0)),
            pl.BlockSpec((tm * (D // (2 * LANES)), LANES), lambda i: (i, 0)),
            pl.BlockSpec((TOP_K, tm), lambda i: (0, i)),
            pl.BlockSpec((TOP_K, tm), lambda i: (0, i)),
            pl.BlockSpec((None, E, 1), lambda i: (i, 0, 0)),
        ],
        out_shape=[
            jax.ShapeDtypeStruct((T, D), F32),
            jax.ShapeDtypeStruct((T * (D // (2 * LANES)), LANES), jnp.uint32),
            jax.ShapeDtypeStruct((TOP_K, T), jnp.int32),
            jax.ShapeDtypeStruct((TOP_K, T), F32),
            jax.ShapeDtypeStruct((T // tm, E, 1), jnp.int32),
        ],
        compiler_params=_params("parallel"),
        name="outproj_ln_router",
    )(o2d, w_out_bf16, x2d, g, b, wr, rb)


def _routing_plan(eidxT, counts, *, tm):
    K, T = eidxT.shape
    n_experts = counts.shape[0]
    n_slots = T * K
    n_tiles = n_slots // tm + n_experts
    e_flat = eidxT.reshape(-1)
    order = jnp.argsort(e_flat, stable=True).astype(jnp.int32)
    tiles_per_e = (counts + tm - 1) // tm
    tile_end = jnp.cumsum(tiles_per_e)
    tile_start = tile_end - tiles_per_e
    run_start = jnp.cumsum(counts) - counts
    total_tiles = tile_end[-1]

    tile_ids = jnp.arange(n_tiles, dtype=jnp.int32)
    tile_valid = tile_ids < total_tiles
    last_tile = jnp.maximum(total_tiles - 1, 0)
    tile_e = jnp.sum((jnp.minimum(tile_ids, last_tile)[:, None] >= tile_end[None, :]).astype(jnp.int32), axis=1)
    tile_e = jnp.minimum(tile_e, n_experts - 1)

    tile_off = (tile_ids - tile_start[tile_e]) * tm
    n_valid = jnp.where(tile_valid, jnp.clip(counts[tile_e] - tile_off, 0, tm), 0).astype(jnp.int32)
    r = jnp.arange(tm, dtype=jnp.int32)[None, :]
    src = jnp.clip((run_start[tile_e] + tile_off)[:, None] + r, 0, n_slots - 1)
    slots = jnp.where(r < n_valid[:, None], order[src], 0).astype(jnp.int32)
    return tile_e, n_valid, lax.rem(slots, T), slots


def _experts_kernel(te_ref, nv_ref, tok_hbm, slot_hbm, x_hbm, wg_ref, wu_ref, wd_ref, y_hbm,
                    tok_a, tok_b, slot_a, slot_b, xbuf, ybuf, wg_bf, wu_bf, wd_bf,
                    tok_sems, slot_sems, gather_sems, scatter_sems, *, n_sub):
    i = pl.program_id(0)
    last = pl.num_programs(0) - 1
    tm = xbuf.shape[1] // n_sub
    toks = (tok_a, tok_b)
    slots = (slot_a, slot_b)
    n_valid = nv_ref[i]

    @pl.when((i == 0) | (te_ref[i] != te_ref[jnp.maximum(i - 1, 0)]))
    def _():
        wg_bf[...] = wg_ref[...].astype(BF16)
        wu_bf[...] = wu_ref[...].astype(BF16)
        wd_bf[...] = wd_ref[...].astype(BF16)

    def tok_copy(tile, b):
        return pltpu.make_async_copy(tok_hbm.at[jnp.minimum(tile, last)], toks[b], tok_sems.at[b])

    def slot_copy(tile, b):
        return pltpu.make_async_copy(slot_hbm.at[jnp.minimum(tile, last)], slots[b], slot_sems.at[b])

    def start_gather(b):
        for r in range(tm):
            src = pl.multiple_of(toks[b][r], n_sub)
            pltpu.make_async_copy(x_hbm.at[pl.ds(src, n_sub), :], xbuf.at[b, pl.ds(r * n_sub, n_sub), :],
                                  gather_sems.at[b]).start()

    def wait_gather(b):
        pltpu.make_async_copy(x_hbm.at[pl.ds(0, tm * n_sub), :], xbuf.at[b], gather_sems.at[b]).wait()

    def start_scatter(b, n):
        def row(r):
            src = pl.multiple_of(r * n_sub, n_sub)
            dst = pl.multiple_of(slots[b][r], n_sub)
            pltpu.make_async_copy(ybuf.at[b, pl.ds(src, n_sub), :], y_hbm.at[pl.ds(dst, n_sub), :],
                                  scatter_sems.at[b]).start()

        def group(g, carry):
            for u in range(ISSUE_UNROLL):
                row(g * ISSUE_UNROLL + u)
            return carry

        def single(r, carry):
            row(r)
            return carry

        n_groups = n // ISSUE_UNROLL
        lax.fori_loop(0, n_groups, group, 0)
        lax.fori_loop(n_groups * ISSUE_UNROLL, n, single, 0)

    def wait_scatter(b, n):
        @pl.when(n > 0)
        def _():
            rows = pl.multiple_of(n * n_sub, n_sub)
            pltpu.make_async_copy(ybuf.at[b, pl.ds(0, rows), :], y_hbm.at[pl.ds(0, rows), :],
                                  scatter_sems.at[b]).wait()

    def step(cur):
        nxt = 1 - cur

        @pl.when(i == 0)
        def _():
            first = tok_copy(0, cur)
            first.start()
            first.wait()
            start_gather(cur)
            tok_copy(1, nxt).start()
            slot_copy(0, cur).start()

        tok_copy(i + 2, cur).start()
        slot_copy(i + 1, nxt).start()
        wait_gather(cur)
        tok_copy(i + 1, nxt).wait()
        start_gather(nxt)

        x_lo, x_hi = _unpack_rows(xbuf.at[cur], n_sub)
        xb = jnp.concatenate([x_lo, x_hi], axis=1).astype(BF16)
        g = jnp.dot(xb, wg_bf[...], preferred_element_type=F32)
        u = jnp.dot(xb, wu_bf[...], preferred_element_type=F32)
        h = (g * _sigmoid(g) * u).astype(BF16)
        _pack_rows(jnp.dot(h, wd_bf[...], preferred_element_type=F32), ybuf.at[cur])

        @pl.when(i > 0)
        def _():
            wait_scatter(nxt, nv_ref[jnp.maximum(i - 1, 0)])

        slot_copy(i, cur).wait()
        start_scatter(cur, n_valid)

        @pl.when(i == last)
        def _():
            wait_scatter(cur, n_valid)
            wait_gather(nxt)
            tok_copy(i + 2, cur).wait()
            slot_copy(i + 1, nxt).wait()

    parity = lax.rem(i, 2)
    for b in range(2):
        pl.when(parity == b)(functools.partial(step, b))


def _routed_experts(x1p, tile_e, tile_rows, toks, slots, wg, wu, wd, *, layer, tm):
    D = wg.shape[2]
    F = wg.shape[3]
    S = D // (2 * LANES)
    T = x1p.shape[0] // S
    n_tiles = slots.shape[0]
    grid_spec = pltpu.PrefetchScalarGridSpec(
        num_scalar_prefetch=2,
        grid=(n_tiles,),
        in_specs=[
            pl.BlockSpec(memory_space=pl.ANY),
            pl.BlockSpec(memory_space=pl.ANY),
            pl.BlockSpec(memory_space=pl.ANY),
            pl.BlockSpec((None, None, D, F), lambda i, te, nv: (layer, te[i], 0, 0)),
            pl.BlockSpec((None, None, D, F), lambda i, te, nv: (layer, te[i], 0, 0)),
            pl.BlockSpec((None, None, F, D), lambda i, te, nv: (layer, te[i], 0, 0)),
        ],
        out_specs=pl.BlockSpec(memory_space=pl.ANY),
        scratch_shapes=[
            pltpu.SMEM((tm,), jnp.int32),
            pltpu.SMEM((tm,), jnp.int32),
            pltpu.SMEM((tm,), jnp.int32),
            pltpu.SMEM((tm,), jnp.int32),
            pltpu.VMEM((2, tm * S, LANES), jnp.uint32),
            pltpu.VMEM((2, tm * S, LANES), jnp.uint32),
            pltpu.VMEM((D, F), BF16),
            pltpu.VMEM((D, F), BF16),
            pltpu.VMEM((F, D), BF16),
            pltpu.SemaphoreType.DMA((2,)),
            pltpu.SemaphoreType.DMA((2,)),
            pltpu.SemaphoreType.DMA((2,)),
            pltpu.SemaphoreType.DMA((2,)),
        ],
    )
    return pl.pallas_call(
        functools.partial(_experts_kernel, n_sub=S),
        grid_spec=grid_spec,
        out_shape=jax.ShapeDtypeStruct((T * TOP_K * S, LANES), jnp.uint32),
        compiler_params=_params("arbitrary"),
        name="routed_experts",
    )(tile_e, tile_rows, toks * S, slots * S, x1p, wg, wu, wd)


def _combine_kernel(y_ref, w_ref, x1_ref, wsg_ref, wsu_ref, wsd_ref, g_ref, b_ref, out_ref, *, alpha):
    x1 = x1_ref[...]
    xb = x1.astype(BF16)
    gg = jnp.dot(xb, wsg_ref[...], preferred_element_type=F32)
    uu = jnp.dot(xb, wsu_ref[...], preferred_element_type=F32)
    hh = (gg * _sigmoid(gg) * uu).astype(BF16)
    shared = jnp.dot(hh, wsd_ref[...], preferred_element_type=F32)
    w = w_ref[...]
    routed = None
    for k in range(TOP_K):
        y_lo, y_hi = _unpack_rows(y_ref.at[k], y_ref.shape[1] // x1.shape[0])
        term = w[:, k:k + 1] * jnp.concatenate([y_lo, y_hi], axis=1)
        routed = term if routed is None else routed + term
    out_ref[...] = _layer_norm(alpha * x1 + (routed + shared), g_ref[...], b_ref[...])


def _combine(y_kt, w_tk, x1, wsg, wsu, wsd, g, b, *, alpha, tm):
    T, D = x1.shape
    Fs = wsg.shape[1]
    kern = functools.partial(_combine_kernel, alpha=alpha)
    const = lambda i: (0, 0)
    return pl.pallas_call(
        kern,
        grid=(T // tm,),
        in_specs=[
            pl.BlockSpec((TOP_K, tm * (D // (2 * LANES)), LANES), lambda i: (0, i, 0)),
            pl.BlockSpec((tm, TOP_K), lambda i: (i, 0)),
            pl.BlockSpec((tm, D), lambda i: (i, 0)),
            pl.BlockSpec((D, Fs), const),
            pl.BlockSpec((D, Fs), const),
            pl.BlockSpec((Fs, D), const),
            pl.BlockSpec((1, D), const),
            pl.BlockSpec((1, D), const),
        ],
        out_specs=pl.BlockSpec((tm, D), lambda i: (i, 0)),
        out_shape=jax.ShapeDtypeStruct((T, D), F32),
        compiler_params=_params("parallel"),
        name="shared_combine_ln",
    )(y_kt, w_tk, x1, wsg, wsu, wsd, g, b)


def _split_bf16(w):
    hi = w.astype(BF16)
    lo = (w - hi.astype(F32)).astype(BF16)
    return jnp.concatenate([hi, lo], axis=-1)


def _pick(n, pref):
    t = min(n, pref)
    while n % t:
        t //= 2
    return t


def _moe_layer(x_attn_in, o2d, w_out, ln1_g, ln1_b, w_router, router_bias, w_gate, w_up, w_down,
               ws_gate, ws_up, ws_down, ln2_g, ln2_b, *, alpha, layer):
    T, D = x_attn_in.shape
    x1, x1p, eidxT, wgtT, tile_counts = _proj_ln_router(
        o2d, w_out.astype(BF16), x_attn_in, ln1_g[None, :], ln1_b[None, :],
        _split_bf16(w_router), router_bias[:, None], alpha=alpha, tm=_pick(T, 256))
    tm_e = _pick(T * TOP_K, 256)
    tile_e, tile_rows, toks, slots = _routing_plan(eidxT, jnp.sum(tile_counts, axis=0)[:, 0], tm=tm_e)
    y = _routed_experts(x1p, tile_e, tile_rows, toks, slots, w_gate, w_up, w_down, layer=layer, tm=tm_e)
    return _combine(y.reshape(TOP_K, T * (D // (2 * LANES)), LANES), wgtT.T, x1, ws_gate.astype(BF16), ws_up.astype(BF16), ws_down.astype(BF16),
                    ln2_g[None, :], ln2_b[None, :], alpha=alpha, tm=_pick(T, 256))


def kernel(x, positions, ln1_g, ln1_b, ln2_g, ln2_b, da_w_in, da_w_out, da_lq1, da_lk1, da_lq2, da_lk2,
           da_subln_g, mb_w_in, mb_w_out, w_router, router_bias, w_gate, w_up, w_down, ws_gate, ws_up, ws_down):
    B, S, D = x.shape
    T = B * S
    depth = ln1_g.shape[0]
    alpha = (2 * depth) ** 0.25
    hd = HEAD_DIM
    cosf, sinf = _rope_tables(positions)
    xt = x.reshape(T, D)
    tm_proj = _pick(T, 1024)

    for i in range(depth):
        m = i // 2
        if i % 2 == 0:
            w_in = da_w_in[m]
            H = w_in.shape[1] // (6 * hd)
            qk_cols = 4 * H * hd
            tq = _pick(S, 512)
            qkv = _qkv_proj(xt, w_in.astype(BF16), cosf, sinf, rope_cols=qk_cols, q_cols=qk_cols // 2,
                            tm=tm_proj, tn=_pick(qk_cols // 2, 1024))
            tk = _pick(S, 1024)
            lambda_init = 0.8 - 0.6 * math.exp(-0.3 * i)
            lam = (jnp.exp(jnp.sum(da_lq1[m].astype(F32) * da_lk1[m].astype(F32)))
                   - jnp.exp(jnp.sum(da_lq2[m].astype(F32) * da_lk2[m].astype(F32))) + lambda_init)
            o2d = _diff_attention(qkv, lam.reshape(1), da_subln_g[m][None, :], B=B, S=S, H=H, tq=tq,
                                  tk=tk, lambda_init=lambda_init)
            w_out = da_w_out[m]
        else:
            w_in = mb_w_in[m]
            H = w_in.shape[1] // (3 * hd)
            qk_cols = 2 * H * hd
            qkv = _qkv_proj(xt, w_in.astype(BF16), cosf, sinf, rope_cols=qk_cols, q_cols=qk_cols // 2,
                            tm=tm_proj, tn=_pick(qk_cols // 2, 1024))
            o2d = _moba_attention(qkv, B=B, S=S, H=H, group=_pick(S // MB_BLOCK, MB_GROUP))
            w_out = mb_w_out[m]
        xt = _moe_layer(xt, o2d, w_out, ln1_g[i], ln1_b[i], w_router[i], router_bias[i], w_gate, w_up,
                        w_down, ws_gate[i], ws_up[i], ws_down[i], ln2_g[i], ln2_b[i], alpha=alpha, layer=i)
    return xt.reshape(B, S, D)
```

```python
import functools
import math

import jax
import jax.numpy as jnp
from jax import lax
from jax.experimental import pallas as pl
from jax.experimental.pallas import tpu as pltpu

F32 = jnp.float32
BF16 = jnp.bfloat16

HEAD_DIM = 128
ROPE_THETA = 500000.0
ROPE_DIMS = HEAD_DIM // 4
ROPE_HALF = ROPE_DIMS // 2
LN_EPS = 1e-5
MB_BLOCK = 256
MB_TOPK = 3
MB_GROUP = 4
TOP_K = 8
N_GROUPS = 8
TOPK_GROUPS = 4
ROUTED_SCALE = 2.5
MASK_VALUE = -1e30
LOG2_E = math.log2(math.e)

LANES = 128
SUBLANES = 8
MXU_COLS = 256
ISSUE_UNROLL = 8
VMEM_LIMIT_BYTES = 56 * 1024 * 1024

_NT = (((1,), (1,)), ((), ()))
_TN = (((0,), (0,)), ((), ()))


def _params(*sem):
    return pltpu.CompilerParams(dimension_semantics=sem, vmem_limit_bytes=VMEM_LIMIT_BYTES)


def _sigmoid(x):
    return 1.0 / (1.0 + jnp.exp(-x))


def _pack_rows(x, o_ref):
    tm, d = x.shape
    half = d // 2
    n_sub = half // LANES
    lo = pltpu.bitcast(x[:, :half].astype(BF16).astype(F32), jnp.uint32) >> 16
    hi = pltpu.bitcast(x[:, half:].astype(BF16).astype(F32), jnp.uint32) & jnp.uint32(0xFFFF0000)
    packed = hi | lo
    for s in range(n_sub):
        o_ref[pl.ds(s, tm, stride=n_sub), :] = packed[:, s * LANES:(s + 1) * LANES]


def _unpack_rows(p_ref, n_sub):
    tm = p_ref.shape[0] // n_sub
    p = jnp.concatenate([p_ref[pl.ds(s, tm, stride=n_sub), :] for s in range(n_sub)], axis=1)
    return pltpu.bitcast(p << 16, F32), pltpu.bitcast(p & jnp.uint32(0xFFFF0000), F32)


def _layer_norm(y, g, b):
    mu = jnp.mean(y, axis=-1, keepdims=True)
    d = y - mu
    var = jnp.mean(d * d, axis=-1, keepdims=True)
    return d * lax.rsqrt(var + LN_EPS) * g + b


def _qkv_kernel(x_ref, w_ref, cos_ref, sin_ref, o_ref, xb_ref, *, rope_blocks, q_blocks, q_scale):
    j = pl.program_id(1)

    @pl.when(j == 0)
    def _():
        xb_ref[...] = x_ref[...].astype(BF16)

    tn = w_ref.shape[1]
    rotary = j < rope_blocks
    cosf = jnp.where(rotary, cos_ref[...], 1.0)
    sinf = jnp.where(rotary, sin_ref[...], 0.0)
    scale = jnp.where(j < q_blocks, q_scale, 1.0).astype(F32)
    lane = lax.broadcasted_iota(jnp.int32, (1, LANES), 1)
    xb = xb_ref[...]
    for n in range(tn // MXU_COLS):
        acc = jnp.dot(xb, w_ref[:, n * MXU_COLS:(n + 1) * MXU_COLS], preferred_element_type=F32)
        for c in range(MXU_COLS // LANES):
            xc = acc[:, c * LANES:(c + 1) * LANES]
            partner = jnp.where(lane < ROPE_HALF,
                                pltpu.roll(xc, LANES - ROPE_HALF, 1),
                                pltpu.roll(xc, ROPE_HALF, 1))
            r = (xc * cosf + partner * sinf) * scale
            col = n * MXU_COLS + c * LANES
            o_ref[:, col:col + LANES] = r.astype(o_ref.dtype)


def _qkv_proj(x2d, w_bf16, cosf, sinf, *, rope_cols, q_cols, tm, tn):
    T, D = x2d.shape
    N = w_bf16.shape[1]
    kern = functools.partial(_qkv_kernel, rope_blocks=rope_cols // tn, q_blocks=q_cols // tn,
                             q_scale=HEAD_DIM ** -0.5 * LOG2_E)
    return pl.pallas_call(
        kern,
        grid=(T // tm, N // tn),
        in_specs=[
            pl.BlockSpec((tm, D), lambda i, j: (i, 0)),
            pl.BlockSpec((D, tn), lambda i, j: (0, j)),
            pl.BlockSpec((tm, LANES), lambda i, j: (i, 0)),
            pl.BlockSpec((tm, LANES), lambda i, j: (i, 0)),
        ],
        out_specs=pl.BlockSpec((tm, tn), lambda i, j: (i, j)),
        out_shape=jax.ShapeDtypeStruct((T, N), BF16),
        scratch_shapes=[pltpu.VMEM((tm, D), BF16)],
        compiler_params=_params("parallel", "arbitrary"),
        name="qkv_proj",
    )(x2d, w_bf16, cosf, sinf)


def _rope_tables(positions):
    inv_freq = ROPE_THETA ** (-jnp.arange(0, ROPE_DIMS, 2, dtype=F32) / ROPE_DIMS)
    ang = positions.astype(F32).reshape(-1)[:, None] * inv_freq
    cos, sin = jnp.cos(ang), jnp.sin(ang)
    T = ang.shape[0]
    rest = LANES - ROPE_DIMS
    cosf = jnp.concatenate([cos, cos, jnp.ones((T, rest), F32)], axis=1)
    sinf = jnp.concatenate([-sin, sin, jnp.zeros((T, rest), F32)], axis=1)
    return cosf, sinf


def _softmax_first(sT, v):
    m = jnp.max(sT, axis=0, keepdims=True)
    p = jnp.exp2(sT - m)
    l = jnp.sum(p, axis=0, keepdims=True)
    acc = lax.dot_general(v, p.astype(BF16), _TN, preferred_element_type=F32)
    return m, l, acc


def _softmax_update(sTs, vs, m_ref, l_ref, acc_ref):
    m_prev = m_ref[...]
    m_new = m_prev
    for sT in sTs:
        m_new = jnp.maximum(m_new, jnp.max(sT, axis=0, keepdims=True))
    alpha = jnp.exp2(m_prev - m_new)
    l = alpha * l_ref[...]
    pv = None
    for sT, v in zip(sTs, vs):
        p = jnp.exp2(sT - m_new)
        l = l + jnp.sum(p, axis=0, keepdims=True)
        d = lax.dot_general(v, p.astype(BF16), _TN, preferred_element_type=F32)
        pv = d if pv is None else pv + d
    l_ref[...] = l
    acc_ref[...] = alpha * acc_ref[...] + pv
    m_ref[...] = m_new


def _causal_mask(sT, key0, qry0):
    tk, tq = sT.shape
    key = key0 + lax.broadcasted_iota(jnp.int32, (tk, tq), 0)
    qry = qry0 + lax.broadcasted_iota(jnp.int32, (tk, tq), 1)
    return jnp.where(key <= qry, sT, MASK_VALUE)


def _da_kernel(lam_ref, q_ref, k_ref, v_ref, g_ref, o_ref, m0, l0, acc0, m1, l1, acc1, s_a, s_b,
               *, tq, tk, out_scale):
    qi = pl.program_id(2)
    hd = HEAD_DIM
    gd = (qi * tq) // tk
    state = ((m0, l0, acc0), (m1, l1, acc1))

    def scores(g, c):
        koff = pl.multiple_of(g * tk, tk)
        kblk = k_ref[pl.ds(koff, tk), c * hd:(c + 1) * hd]
        return lax.dot_general(kblk, q_ref[:, c * hd:(c + 1) * hd], _NT, preferred_element_type=F32)

    def values(g):
        voff = pl.multiple_of(g * tk, tk)
        return v_ref[pl.ds(voff, tk), :]

    last_tile = jnp.maximum(gd - 1, 0)

    def fill(s_ref, g_req):
        g = jnp.minimum(g_req, last_tile)
        for c in range(2):
            s_ref[c] = scores(g, c)

    def consume(s_ref, g):
        v = values(g)
        for c in range(2):
            _softmax_update([s_ref[c]], [v], *state[c])

    fill(s_a, 0)

    v_diag = values(gd)
    for c in range(2):
        m, l, acc = _softmax_first(_causal_mask(scores(gd, c), gd * tk, qi * tq), v_diag)
        m_ref, l_ref, acc_ref = state[c]
        m_ref[...] = m
        l_ref[...] = l
        acc_ref[...] = acc

    def pair(p, carry):
        g0 = 2 * p
        fill(s_b, g0 + 1)
        consume(s_a, g0)
        fill(s_a, g0 + 2)
        consume(s_b, g0 + 1)
        return carry

    lax.fori_loop(0, gd // 2, pair, 0)

    @pl.when(lax.rem(gd, 2) == 1)
    def _():
        consume(s_a, last_tile)

    lam = lam_ref[0]
    oT = acc0[...] / l0[...] - lam * (acc1[...] / l1[...])
    o = oT.T
    o = o * lax.rsqrt(jnp.mean(o * o, axis=-1, keepdims=True) + LN_EPS) * g_ref[...]
    o_ref[...] = (o * out_scale).astype(o_ref.dtype)


def _diff_attention(qkv, lam, subln_g, *, B, S, H, tq, tk, lambda_init):
    T = B * S
    nq = S // tq
    dv = 2 * HEAD_DIM
    kern = functools.partial(_da_kernel, tq=tq, tk=tk, out_scale=1.0 - lambda_init)
    return pl.pallas_call(
        kern,
        grid=(B, H, nq),
        in_specs=[
            pl.BlockSpec(memory_space=pltpu.SMEM),
            pl.BlockSpec((tq, dv), lambda b, h, i: (b * nq + i, h)),
            pl.BlockSpec((S, dv), lambda b, h, i: (b, H + h)),
            pl.BlockSpec((S, dv), lambda b, h, i: (b, 2 * H + h)),
            pl.BlockSpec((1, dv), lambda b, h, i: (0, 0)),
        ],
        out_specs=pl.BlockSpec((tq, dv), lambda b, h, i: (b * nq + i, h)),
        out_shape=jax.ShapeDtypeStruct((T, H * dv), BF16),
        scratch_shapes=2 * [pltpu.VMEM((1, tq), F32), pltpu.VMEM((1, tq), F32), pltpu.VMEM((dv, tq), F32)]
        + 2 * [pltpu.VMEM((2, tk, tq), F32)],
        compiler_params=_params("parallel", "parallel", "arbitrary"),
        name="diff_attention",
    )(lam, qkv, qkv, qkv, subln_g)


def _mb_kernel(q_ref, k_ref, v_ref, o_ref, kmean_ref, bias_ref, m_ref, l_ref, acc_ref, s_a, s_b, *, nb, group):
    j = pl.program_id(2)
    blk = MB_BLOCK

    @pl.when(j == 0)
    def _():
        for n in range(nb):
            kb = k_ref[n * blk:(n + 1) * blk, :].astype(F32)
            kmean_ref[n:n + 1, :] = jnp.sum(kb, axis=0, keepdims=True) * (1.0 / blk)

    q = q_ref[...]
    km = kmean_ref[...]
    km_hi = km.astype(BF16)
    km_lo = (km - km_hi.astype(F32)).astype(BF16)
    gate = (lax.dot_general(km_hi, q, _NT, preferred_element_type=F32)
            + lax.dot_general(km_lo, q, _NT, preferred_element_type=F32))
    bid = lax.broadcasted_iota(jnp.int32, gate.shape, 0)
    rem = jnp.where(bid < j, gate, -jnp.inf)
    sel = jnp.zeros(gate.shape, jnp.bool_)
    for _ in range(MB_TOPK):
        mx = jnp.max(rem, axis=0, keepdims=True)
        first = jnp.min(jnp.where(rem == mx, bid, nb), axis=0, keepdims=True)
        pick = (bid == first) & (mx > -jnp.inf)
        sel = sel | pick
        rem = jnp.where(bid == first, -jnp.inf, rem)
    bias = jnp.where(sel, 0.0, MASK_VALUE).astype(F32)
    for gg in range(nb // group):
        bias_ref[gg] = bias[gg * group:(gg + 1) * group, :]

    def scores(n):
        koff = pl.multiple_of(n * blk, blk)
        return lax.dot_general(k_ref[pl.ds(koff, blk), :], q, _NT, preferred_element_type=F32)

    def values(n):
        voff = pl.multiple_of(n * blk, blk)
        return v_ref[pl.ds(voff, blk), :]

    n_groups = (j + group - 1) // group
    last_group = jnp.maximum(n_groups - 1, 0)

    def fill(s_ref, g_req):
        g = jnp.minimum(g_req, last_group)
        gbias = bias_ref[g]
        for t in range(group):
            s_ref[t] = scores(g * group + t) + gbias[t:t + 1, :]

    def consume(s_ref, g):
        _softmax_update([s_ref[t] for t in range(group)], [values(g * group + t) for t in range(group)],
                        m_ref, l_ref, acc_ref)

    fill(s_a, 0)

    m, l, acc = _softmax_first(_causal_mask(scores(j), 0, 0), values(j))
    m_ref[...] = m
    l_ref[...] = l
    acc_ref[...] = acc

    def pair(p, carry):
        g0 = 2 * p
        fill(s_b, g0 + 1)
        consume(s_a, g0)
        fill(s_a, g0 + 2)
        consume(s_b, g0 + 1)
        return carry

    lax.fori_loop(0, n_groups // 2, pair, 0)

    @pl.when(lax.rem(n_groups, 2) == 1)
    def _():
        consume(s_a, last_group)

    oT = acc_ref[...] / l_ref[...]
    o_ref[...] = oT.T.astype(o_ref.dtype)


def _moba_attention(qkv, *, B, S, H, group):
    T = B * S
    blk = MB_BLOCK
    nb = S // blk
    hd = HEAD_DIM
    kern = functools.partial(_mb_kernel, nb=nb, group=group)
    return pl.pallas_call(
        kern,
        grid=(B, H, nb),
        in_specs=[
            pl.BlockSpec((blk, hd), lambda b, h, i: (b * nb + i, h)),
            pl.BlockSpec((S, hd), lambda b, h, i: (b, H + h)),
            pl.BlockSpec((S, hd), lambda b, h, i: (b, 2 * H + h)),
        ],
        out_specs=pl.BlockSpec((blk, hd), lambda b, h, i: (b * nb + i, h)),
        out_shape=jax.ShapeDtypeStruct((T, H * hd), BF16),
        scratch_shapes=[
            pltpu.VMEM((nb, hd), F32),
            pltpu.VMEM((nb // group, group, blk), F32),
            pltpu.VMEM((1, blk), F32),
            pltpu.VMEM((1, blk), F32),
            pltpu.VMEM((hd, blk), F32),
            pltpu.VMEM((group, blk, blk), F32),
            pltpu.VMEM((group, blk, blk), F32),
        ],
        compiler_params=_params("parallel", "parallel", "arbitrary"),
        name="moba_attention",
    )(qkv, qkv, qkv)


def _router(x1, wr_ref, rb_ref, eidx_ref, wgt_ref, cnt_ref):
    E = wr_ref.shape[1] // 2
    per_group = E // N_GROUPS
    tm = x1.shape[0]
    x_hi = x1.astype(BF16)
    x_lo = (x1 - x_hi.astype(F32)).astype(BF16)
    parts = (jnp.dot(x_hi, wr_ref[...], preferred_element_type=F32)
             + jnp.dot(x_lo, wr_ref[...], preferred_element_type=F32))
    logits = (parts[:, :E] + parts[:, E:]).T
    scores = _sigmoid(logits)
    biased = scores + rb_ref[...]

    gid8 = lax.broadcasted_iota(jnp.int32, (per_group, tm), 0)
    gscores = []
    for g in range(N_GROUPS):
        bg = biased[g * per_group:(g + 1) * per_group, :]
        m1 = jnp.max(bg, axis=0, keepdims=True)
        i1 = jnp.min(jnp.where(bg == m1, gid8, per_group), axis=0, keepdims=True)
        m2 = jnp.max(jnp.where(gid8 == i1, -jnp.inf, bg), axis=0, keepdims=True)
        gscores.append(m1 + m2)
    gs = jnp.concatenate(gscores, axis=0)

    gidx = lax.broadcasted_iota(jnp.int32, (N_GROUPS, tm), 0)
    rank = jnp.zeros((N_GROUPS, tm), jnp.int32)
    for g in range(N_GROUPS):
        row = gs[g:g + 1, :]
        beats = (row > gs) | ((row == gs) & (g < gidx))
        rank = rank + beats.astype(jnp.int32)
    gsel = rank < TOPK_GROUPS

    masked = jnp.concatenate(
        [jnp.where(gsel[g:g + 1, :], biased[g * per_group:(g + 1) * per_group, :], -jnp.inf)
         for g in range(N_GROUPS)], axis=0)

    eid = lax.broadcasted_iota(jnp.int32, (E, tm), 0)
    rem = masked
    ids, vals = [], []
    picked = jnp.zeros((E, tm), jnp.int32)
    for _ in range(TOP_K):
        mx = jnp.max(rem, axis=0, keepdims=True)
        first = jnp.min(jnp.where(rem == mx, eid, E), axis=0, keepdims=True)
        onehot = eid == first
        ids.append(first)
        vals.append(jnp.sum(jnp.where(onehot, scores, 0.0), axis=0, keepdims=True))
        rem = jnp.where(onehot, -jnp.inf, rem)
        picked = picked + onehot.astype(jnp.int32)
    cnt_ref[...] = jnp.sum(picked, axis=1, keepdims=True)
    w = jnp.concatenate(vals, axis=0)
    w = w / jnp.sum(w, axis=0, keepdims=True) * ROUTED_SCALE
    eidx_ref[...] = jnp.concatenate(ids, axis=0)
    wgt_ref[...] = w


def _proj_ln_router_kernel(o_ref, w_ref, x_ref, g_ref, b_ref, wr_ref, rb_ref,
                           x1_ref, x1p_ref, eidx_ref, wgt_ref, cnt_ref, *, alpha):
    h = jnp.dot(o_ref[...], w_ref[...], preferred_element_type=F32)
    x1 = _layer_norm(alpha * x_ref[...] + h, g_ref[...], b_ref[...])
    x1_ref[...] = x1
    _pack_rows(x1, x1p_ref)
    _router(x1, wr_ref, rb_ref, eidx_ref, wgt_ref, cnt_ref)


def _proj_ln_router(o2d, w_out_bf16, x2d, g, b, wr, rb, *, alpha, tm):
    T, D = x2d.shape
    Do = o2d.shape[1]
    E = wr.shape[1] // 2
    kern = functools.partial(_proj_ln_router_kernel, alpha=alpha)
    const = lambda i: (0, 0)
    return pl.pallas_call(
        kern,
        grid=(T // tm,),
        in_specs=[
            pl.BlockSpec((tm, Do), lambda i: (i, 0)),
            pl.BlockSpec((Do, D), const),
            pl.BlockSpec((tm, D), lambda i: (i, 0)),
            pl.BlockSpec((1, D), const),
            pl.BlockSpec((1, D), const),
            pl.BlockSpec((D, 2 * E), const),
            pl.BlockSpec((E, 1), const),
        ],
        out_specs=[
            pl.BlockSpec((tm, D), lambda i: (i, 0)),
            pl.BlockSpec((tm * (D // (2 * LANES)), LANES), lambda i: (i, 0)),
            pl.BlockSpec((TOP_K, tm), lambda i: (0, i)),
            pl.BlockSpec((TOP_K, tm), lambda i: (0, i)),
            pl.BlockSpec((None, E, 1), lambda i: (i, 0, 0)),
        ],
        out_shape=[
            jax.ShapeDtypeStruct((T, D), F32),
            jax.ShapeDtypeStruct((T * (D // (2 * LANES)), LANES), jnp.uint32),
            jax.ShapeDtypeStruct((TOP_K, T), jnp.int32),
            jax.ShapeDtypeStruct((TOP_K, T), F32),
            jax.ShapeDtypeStruct((T // tm, E, 1), jnp.int32),
        ],
        compiler_params=_params("parallel"),
        name="outproj_ln_router",
    )(o2d, w_out_bf16, x2d, g, b, wr, rb)


def _routing_plan(eidxT, counts, *, tm):
    K, T = eidxT.shape
    n_experts = counts.shape[0]
    n_slots = T * K
    n_tiles = n_slots // tm + n_experts
    e_flat = eidxT.reshape(-1)
    order = jnp.argsort(e_flat, stable=True).astype(jnp.int32)
    tiles_per_e = (counts + tm - 1) // tm
    tile_end = jnp.cumsum(tiles_per_e)
    tile_start = tile_end - tiles_per_e
    run_start = jnp.cumsum(counts) - counts
    total_tiles = tile_end[-1]

    tile_ids = jnp.arange(n_tiles, dtype=jnp.int32)
    tile_valid = tile_ids < total_tiles
    last_tile = jnp.maximum(total_tiles - 1, 0)
    tile_e = jnp.sum((jnp.minimum(tile_ids, last_tile)[:, None] >= tile_end[None, :]).astype(jnp.int32), axis=1)
    tile_e = jnp.minimum(tile_e, n_experts - 1)

    tile_off = (tile_ids - tile_start[tile_e]) * tm
    n_valid = jnp.where(tile_valid, jnp.clip(counts[tile_e] - tile_off, 0, tm), 0).astype(jnp.int32)
    r = jnp.arange(tm, dtype=jnp.int32)[None, :]
    src = jnp.clip((run_start[tile_e] + tile_off)[:, None] + r, 0, n_slots - 1)
    slots = jnp.where(r < n_valid[:, None], order[src], 0).astype(jnp.int32)
    return tile_e, n_valid, lax.rem(slots, T), slots


def _experts_kernel(te_ref, nv_ref, used_ref, tok_hbm, slot_hbm, x_hbm, wg_ref, wu_ref, wd_ref, y_hbm,
                    tok_a, tok_b, slot_a, slot_b, xbuf, ybuf, wg_bf, wu_bf, wd_bf,
                    tok_sems, slot_sems, gather_sems, scatter_sems, *, n_sub):
    i = pl.program_id(0)
    last = used_ref[0] - 1
    tm = xbuf.shape[1] // n_sub
    toks = (tok_a, tok_b)
    slots = (slot_a, slot_b)
    n_valid = nv_ref[i]

    @pl.when((i == 0) | (te_ref[i] != te_ref[jnp.maximum(i - 1, 0)]))
    def _():
        wg_bf[...] = wg_ref[...].astype(BF16)
        wu_bf[...] = wu_ref[...].astype(BF16)
        wd_bf[...] = wd_ref[...].astype(BF16)

    def tok_copy(tile, b):
        return pltpu.make_async_copy(tok_hbm.at[jnp.minimum(tile, last)], toks[b], tok_sems.at[b])

    def slot_copy(tile, b):
        return pltpu.make_async_copy(slot_hbm.at[jnp.minimum(tile, last)], slots[b], slot_sems.at[b])

    def start_gather(b):
        for r in range(tm):
            src = pl.multiple_of(toks[b][r], n_sub)
            pltpu.make_async_copy(x_hbm.at[pl.ds(src, n_sub), :], xbuf.at[b, pl.ds(r * n_sub, n_sub), :],
                                  gather_sems.at[b]).start()

    def wait_gather(b):
        pltpu.make_async_copy(x_hbm.at[pl.ds(0, tm * n_sub), :], xbuf.at[b], gather_sems.at[b]).wait()

    def start_scatter(b, n):
        def row(r):
            src = pl.multiple_of(r * n_sub, n_sub)
            dst = pl.multiple_of(slots[b][r], n_sub)
            pltpu.make_async_copy(ybuf.at[b, pl.ds(src, n_sub), :], y_hbm.at[pl.ds(dst, n_sub), :],
                                  scatter_sems.at[b]).start()

        def group(g, carry):
            for u in range(ISSUE_UNROLL):
                row(g * ISSUE_UNROLL + u)
            return carry

        def single(r, carry):
            row(r)
            return carry

        n_groups = n // ISSUE_UNROLL
        lax.fori_loop(0, n_groups, group, 0)
        lax.fori_loop(n_groups * ISSUE_UNROLL, n, single, 0)

    def wait_scatter(b, n):
        @pl.when(n > 0)
        def _():
            rows = pl.multiple_of(n * n_sub, n_sub)
            pltpu.make_async_copy(ybuf.at[b, pl.ds(0, rows), :], y_hbm.at[pl.ds(0, rows), :],
                                  scatter_sems.at[b]).wait()

    def step(cur):
        nxt = 1 - cur

        @pl.when(i == 0)
        def _():
            first = tok_copy(0, cur)
            first.start()
            first.wait()
            start_gather(cur)
            tok_copy(1, nxt).start()
            slot_copy(0, cur).start()

        tok_copy(i + 2, cur).start()
        slot_copy(i + 1, nxt).start()
        wait_gather(cur)
        tok_copy(i + 1, nxt).wait()
        start_gather(nxt)

        x_lo, x_hi = _unpack_rows(xbuf.at[cur], n_sub)
        xb = jnp.concatenate([x_lo, x_hi], axis=1).astype(BF16)
        g = jnp.dot(xb, wg_bf[...], preferred_element_type=F32)
        u = jnp.dot(xb, wu_bf[...], preferred_element_type=F32)
        h = (g * _sigmoid(g) * u).astype(BF16)
        _pack_rows(jnp.dot(h, wd_bf[...], preferred_element_type=F32), ybuf.at[cur])

        @pl.when(i > 0)
        def _():
            wait_scatter(nxt, nv_ref[jnp.maximum(i - 1, 0)])

        slot_copy(i, cur).wait()
        start_scatter(cur, n_valid)

        @pl.when(i == last)
        def _():
            wait_scatter(cur, n_valid)
            wait_gather(nxt)
            tok_copy(i + 2, cur).wait()
            slot_copy(i + 1, nxt).wait()

    parity = lax.rem(i, 2)
    for b in range(2):
        pl.when((parity == b) & (i <= last))(functools.partial(step, b))


def _routed_experts(x1p, tile_e, tile_rows, toks, slots, wg, wu, wd, *, layer, tm):
    D = wg.shape[2]
    F = wg.shape[3]
    S = D // (2 * LANES)
    T = x1p.shape[0] // S
    n_tiles = slots.shape[0]
    grid_spec = pltpu.PrefetchScalarGridSpec(
        num_scalar_prefetch=3,
        grid=(n_tiles,),
        in_specs=[
            pl.BlockSpec(memory_space=pl.ANY),
            pl.BlockSpec(memory_space=pl.ANY),
            pl.BlockSpec(memory_space=pl.ANY),
            pl.BlockSpec((None, None, D, F), lambda i, te, nv, used: (layer, te[i], 0, 0)),
            pl.BlockSpec((None, None, D, F), lambda i, te, nv, used: (layer, te[i], 0, 0)),
            pl.BlockSpec((None, None, F, D), lambda i, te, nv, used: (layer, te[i], 0, 0)),
        ],
        out_specs=pl.BlockSpec(memory_space=pl.ANY),
        scratch_shapes=[
            pltpu.SMEM((tm,), jnp.int32),
            pltpu.SMEM((tm,), jnp.int32),
            pltpu.SMEM((tm,), jnp.int32),
            pltpu.SMEM((tm,), jnp.int32),
            pltpu.VMEM((2, tm * S, LANES), jnp.uint32),
            pltpu.VMEM((2, tm * S, LANES), jnp.uint32),
            pltpu.VMEM((D, F), BF16),
            pltpu.VMEM((D, F), BF16),
            pltpu.VMEM((F, D), BF16),
            pltpu.SemaphoreType.DMA((2,)),
            pltpu.SemaphoreType.DMA((2,)),
            pltpu.SemaphoreType.DMA((2,)),
            pltpu.SemaphoreType.DMA((2,)),
        ],
    )
    return pl.pallas_call(
        functools.partial(_experts_kernel, n_sub=S),
        grid_spec=grid_spec,
        out_shape=jax.ShapeDtypeStruct((T * TOP_K * S, LANES), jnp.uint32),
        compiler_params=_params("arbitrary"),
        name="routed_experts",
    )(tile_e, tile_rows, jnp.sum((tile_rows > 0).astype(jnp.int32)).reshape(1), toks * S, slots * S, x1p, wg, wu, wd)


def _combine_kernel(y_ref, w_ref, x1_ref, wsg_ref, wsu_ref, wsd_ref, g_ref, b_ref, out_ref, *, alpha):
    x1 = x1_ref[...]
    xb = x1.astype(BF16)
    gg = jnp.dot(xb, wsg_ref[...], preferred_element_type=F32)
    uu = jnp.dot(xb, wsu_ref[...], preferred_element_type=F32)
    hh = (gg * _sigmoid(gg) * uu).astype(BF16)
    shared = jnp.dot(hh, wsd_ref[...], preferred_element_type=F32)
    w = w_ref[...]
    routed = None
    for k in range(TOP_K):
        y_lo, y_hi = _unpack_rows(y_ref.at[k], y_ref.shape[1] // x1.shape[0])
        term = w[:, k:k + 1] * jnp.concatenate([y_lo, y_hi], axis=1)
        routed = term if routed is None else routed + term
    out_ref[...] = _layer_norm(alpha * x1 + (routed + shared), g_ref[...], b_ref[...])


def _combine(y_kt, w_tk, x1, wsg, wsu, wsd, g, b, *, alpha, tm):
    T, D = x1.shape
    Fs = wsg.shape[1]
    kern = functools.partial(_combine_kernel, alpha=alpha)
    const = lambda i: (0, 0)
    return pl.pallas_call(
        kern,
        grid=(T // tm,),
        in_specs=[
            pl.BlockSpec((TOP_K, tm * (D // (2 * LANES)), LANES), lambda i: (0, i, 0)),
            pl.BlockSpec((tm, TOP_K), lambda i: (i, 0)),
            pl.BlockSpec((tm, D), lambda i: (i, 0)),
            pl.BlockSpec((D, Fs), const),
            pl.BlockSpec((D, Fs), const),
            pl.BlockSpec((Fs, D), const),
            pl.BlockSpec((1, D), const),
            pl.BlockSpec((1, D), const),
        ],
        out_specs=pl.BlockSpec((tm, D), lambda i: (i, 0)),
        out_shape=jax.ShapeDtypeStruct((T, D), F32),
        compiler_params=_params("parallel"),
        name="shared_combine_ln",
    )(y_kt, w_tk, x1, wsg, wsu, wsd, g, b)


def _split_bf16(w):
    hi = w.astype(BF16)
    lo = (w - hi.astype(F32)).astype(BF16)
    return jnp.concatenate([hi, lo], axis=-1)


def _pick(n, pref):
    t = min(n, pref)
    while n % t:
        t //= 2
    return t


def _moe_layer(x_attn_in, o2d, w_out, ln1_g, ln1_b, w_router, router_bias, w_gate, w_up, w_down,
               ws_gate, ws_up, ws_down, ln2_g, ln2_b, *, alpha, layer):
    T, D = x_attn_in.shape
    x1, x1p, eidxT, wgtT, tile_counts = _proj_ln_router(
        o2d, w_out.astype(BF16), x_attn_in, ln1_g[None, :], ln1_b[None, :],
        _split_bf16(w_router), router_bias[:, None], alpha=alpha, tm=_pick(T, 256))
    tm_e = _pick(T * TOP_K, 256)
    tile_e, tile_rows, toks, slots = _routing_plan(eidxT, jnp.sum(tile_counts, axis=0)[:, 0], tm=tm_e)
    y = _routed_experts(x1p, tile_e, tile_rows, toks, slots, w_gate, w_up, w_down, layer=layer, tm=tm_e)
    return _combine(y.reshape(TOP_K, T * (D // (2 * LANES)), LANES), wgtT.T, x1, ws_gate.astype(BF16), ws_up.astype(BF16), ws_down.astype(BF16),
                    ln2_g[None, :], ln2_b[None, :], alpha=alpha, tm=_pick(T, 256))


def kernel(x, positions, ln1_g, ln1_b, ln2_g, ln2_b, da_w_in, da_w_out, da_lq1, da_lk1, da_lq2, da_lk2,
           da_subln_g, mb_w_in, mb_w_out, w_router, router_bias, w_gate, w_up, w_down, ws_gate, ws_up, ws_down):
    B, S, D = x.shape
    T = B * S
    depth = ln1_g.shape[0]
    alpha = (2 * depth) ** 0.25
    hd = HEAD_DIM
    cosf, sinf = _rope_tables(positions)
    xt = x.reshape(T, D)
    tm_proj = _pick(T, 1024)

    for i in range(depth):
        m = i // 2
        if i % 2 == 0:
            w_in = da_w_in[m]
            H = w_in.shape[1] // (6 * hd)
            qk_cols = 4 * H * hd
            tq = _pick(S, 512)
            qkv = _qkv_proj(xt, w_in.astype(BF16), cosf, sinf, rope_cols=qk_cols, q_cols=qk_cols // 2,
                            tm=tm_proj, tn=_pick(qk_cols // 2, 1024))
            tk = _pick(S, 1024)
            lambda_init = 0.8 - 0.6 * math.exp(-0.3 * i)
            lam = (jnp.exp(jnp.sum(da_lq1[m].astype(F32) * da_lk1[m].astype(F32)))
                   - jnp.exp(jnp.sum(da_lq2[m].astype(F32) * da_lk2[m].astype(F32))) + lambda_init)
            o2d = _diff_attention(qkv, lam.reshape(1), da_subln_g[m][None, :], B=B, S=S, H=H, tq=tq,
                                  tk=tk, lambda_init=lambda_init)
            w_out = da_w_out[m]
        else:
            w_in = mb_w_in[m]
            H = w_in.shape[1] // (3 * hd)
            qk_cols = 2 * H * hd
            qkv = _qkv_proj(xt, w_in.astype(BF16), cosf, sinf, rope_cols=qk_cols, q_cols=qk_cols // 2,
                            tm=tm_proj, tn=_pick(qk_cols // 2, 1024))
            o2d = _moba_attention(qkv, B=B, S=S, H=H, group=_pick(S // MB_BLOCK, MB_GROUP))
            w_out = mb_w_out[m]
        xt = _moe_layer(xt, o2d, w_out, ln1_g[i], ln1_b[i], w_router[i], router_bias[i], w_gate, w_up,
                        w_down, ws_gate[i], ws_up[i], ws_down[i], ln2_g[i], ln2_b[i], alpha=alpha, layer=i)
    return xt.reshape(B, S, D)
```
